```python
import math
import jax, jax.numpy as jnp
from jax import lax
import numpy as np

D_MODEL = 1024
BATCH = 4
SEQ = 4096
DEPTH = 2

GRID_W = 64
CTX_LEN = 256
Q_BLOCK = 128
ROPE_THETA = 10000.0
EPS = 1e-6
A_HEADS = 4
A_HEAD_DIM = 64
A_V_DIM = 2 * A_HEAD_DIM
B_HEADS = 8
B_Q_RANK = 256
B_KV_RANK = 128
B_NOPE_DIM = 64
B_ROPE_DIM = 32
B_V_DIM = 64
C_HEADS = 8
C_KV_HEADS = 2
C_HEAD_DIM = 64
D_HEADS = 8
D_HEAD_DIM = 64
NA_WIN_H = 8
NA_WIN_W = 16
FFN_HIDDEN = -(-8 * D_MODEL // (3 * 256)) * 256

AB_SPLITS = (A_HEADS * 2 * A_HEAD_DIM, A_HEADS * 2 * A_HEAD_DIM, A_HEADS * A_V_DIM, B_Q_RANK, B_KV_RANK, B_ROPE_DIM)
AB_IN = sum(AB_SPLITS)
AB_OUT = A_HEADS * A_V_DIM + B_HEADS * B_V_DIM
CD_SPLITS = (C_HEADS * C_HEAD_DIM, C_KV_HEADS * C_HEAD_DIM, C_KV_HEADS * C_HEAD_DIM, D_HEADS * D_HEAD_DIM, D_HEADS * D_HEAD_DIM, D_HEADS * D_HEAD_DIM)
CD_IN = sum(CD_SPLITS)
CD_OUT = C_HEADS * C_HEAD_DIM + D_HEADS * D_HEAD_DIM

kernel_name = 'hybrid_diffusion_backbone_diffmla_gqa_natten'


def split_points(sizes):
    return [int(s) for s in np.cumsum(sizes)[:-1]]


def rmsnorm(x, g):
    xf = x.astype(jnp.float32)
    y = xf * lax.rsqrt(jnp.mean(xf * xf, axis=-1, keepdims=True) + EPS)
    return (y * g.astype(jnp.float32)).astype(x.dtype)


def modulate(h, shift, scale):
    return h * (1 + scale[:, None, :]) + shift[:, None, :]


def ada_params(cond, w_mod, b_mod):
    return jnp.split(jax.nn.silu(cond) @ w_mod + b_mod, 6, axis=-1)


def softmax32(s):
    return jax.nn.softmax(s.astype(jnp.float32), axis=-1)


def axial_rope_tables(rot_dim, n_tokens):
    t = jnp.arange(n_tokens)
    rows = (t // GRID_W).astype(jnp.float32)
    cols = (t % GRID_W).astype(jnp.float32)
    axis_dim = rot_dim // 2
    inv = ROPE_THETA ** (-jnp.arange(0, axis_dim, 2, dtype=jnp.float32) / axis_dim)
    ang = jnp.concatenate([rows[:, None] * inv, cols[:, None] * inv], axis=-1)
    return jnp.cos(ang), jnp.sin(ang)


def apply_rope(x, cos, sin):
    extra = x.ndim - 3
    cos = cos.reshape(cos.shape[0], *([1] * extra), cos.shape[1])
    sin = sin.reshape(sin.shape[0], *([1] * extra), sin.shape[1])
    half = x.shape[-1] // 2
    x1 = x[..., :half].astype(jnp.float32)
    x2 = x[..., half:].astype(jnp.float32)
    return jnp.concatenate([x1 * cos - x2 * sin, x1 * sin + x2 * cos], axis=-1).astype(x.dtype)


def to_blocks(a):
    b, t = a.shape[:2]
    return jnp.moveaxis(a.reshape(b, t // Q_BLOCK, Q_BLOCK, *a.shape[2:]), 1, 0)


def from_blocks(a):
    nb, b, qb = a.shape[:3]
    return jnp.moveaxis(a, 0, 1).reshape(b, nb * qb, *a.shape[3:])


def sweep_query_blocks(fn, *qs):
    out = lax.map(lambda blk: fn(*blk), tuple(to_blocks(q) for q in qs))
    return from_blocks(out)


def diff_core(q1, q2, k1, k2, v, lam):
    scale = A_HEAD_DIM ** -0.5
    p1 = softmax32(jnp.einsum('bqhd,bkhd->bhqk', q1, k1) * scale)
    p2 = softmax32(jnp.einsum('bqhd,bkhd->bhqk', q2, k2) * scale)
    p = (p1 - lam * p2).astype(v.dtype)
    return jnp.einsum('bhqk,bkhd->bqhd', p, v)


def mla_core(qn, qr, kn, kr, v):
    scale = (B_NOPE_DIM + B_ROPE_DIM) ** -0.5
    s = jnp.einsum('bqhd,bkhd->bhqk', qn, kn) + jnp.einsum('bqhr,bkr->bhqk', qr, kr)
    p = softmax32(s * scale).astype(v.dtype)
    return jnp.einsum('bhqk,bkhd->bqhd', p, v)


def gqa_core(q, k, v):
    b, nq, h, d = q.shape
    g = k.shape[2]
    qg = q.reshape(b, nq, g, h // g, d)
    s = jnp.einsum('bqgrd,bkgd->bgrqk', qg, k) * d ** -0.5
    p = softmax32(s).astype(v.dtype)
    return jnp.einsum('bgrqk,bkgd->bqgrd', p, v).reshape(b, nq, h, d)


def neighbourhood_tables(n_tokens):
    rows_n = n_tokens // GRID_W
    wh = min(NA_WIN_H, rows_n)
    ww = NA_WIN_W
    t = jnp.arange(n_tokens)
    r = t // GRID_W
    col = t % GRID_W
    rs = jnp.clip(r - wh // 2, 0, rows_n - wh)
    cs = jnp.clip(col - ww // 2, 0, GRID_W - ww)
    krow = rs[:, None, None] + jnp.arange(wh)[None, :, None]
    kcol = cs[:, None, None] + jnp.arange(ww)[None, None, :]
    shape = (n_tokens, wh, ww)
    idx = jnp.broadcast_to(krow * GRID_W + kcol, shape).reshape(n_tokens, wh * ww)
    rel_r = jnp.broadcast_to(krow - r[:, None, None] + NA_WIN_H - 1, shape).reshape(n_tokens, wh * ww)
    rel_c = jnp.broadcast_to(kcol - col[:, None, None] + NA_WIN_W - 1, shape).reshape(n_tokens, wh * ww)
    return idx, rel_r, rel_c


def neighbourhood_attn(q, k, v, k_ctx, v_ctx, rpb):
    n_tokens = q.shape[1]
    idx, rel_r, rel_c = neighbourhood_tables(n_tokens)
    win = idx.shape[-1]
    nb = n_tokens // Q_BLOCK
    scale = D_HEAD_DIM ** -0.5

    def blk(args):
        qb, idxb, rrb, rcb = args
        kb = jnp.take(k, idxb, axis=1)
        vb = jnp.take(v, idxb, axis=1)
        bias = rpb[:, rrb, rcb]
        s_loc = jnp.einsum('bqhd,bqwhd->bhqw', qb, kb) * scale + bias[None]
        s_ctx = jnp.einsum('bqhd,bkhd->bhqk', qb, k_ctx) * scale
        p = softmax32(jnp.concatenate([s_loc, s_ctx], axis=-1)).astype(v.dtype)
        return (jnp.einsum('bhqw,bqwhd->bqhd', p[..., :win], vb)
                + jnp.einsum('bhqk,bkhd->bqhd', p[..., win:], v_ctx))

    out = lax.map(blk, (to_blocks(q), idx.reshape(nb, Q_BLOCK, win),
                        rel_r.reshape(nb, Q_BLOCK, win), rel_c.reshape(nb, Q_BLOCK, win)))
    return from_blocks(out)


def ffn_sublayer(x, g_norm, shift, scale, gate, w1, w3, w2):
    h = modulate(rmsnorm(x, g_norm), shift, scale)
    return x + gate[:, None, :] * ((jax.nn.silu(h @ w1) * (h @ w3)) @ w2)


def ab_layer(x, xc, c, c_ctx, w_mod, b_mod, g_attn, w_in, lam_q1, lam_k1, lam_q2, lam_k2, g_subln,
             g_cq, w_uq, g_ckv, w_ukv, w_out, g_ffn, w1, w3, w2, layer_idx, last):
    n_tokens = x.shape[1]
    cos_a, sin_a = axial_rope_tables(A_HEAD_DIM, n_tokens)
    cos_b, sin_b = axial_rope_tables(B_ROPE_DIM, n_tokens)
    sh_a, sc_a, gt_a, sh_f, sc_f, gt_f = ada_params(c, w_mod, b_mod)
    csh_a, csc_a, cgt_a, csh_f, csc_f, cgt_f = ada_params(c_ctx[None], w_mod, b_mod)

    def project(h, rotary):
        bp, tp, _ = h.shape
        qa, ka, va, cq, ckv, kr = jnp.split(h @ w_in, split_points(AB_SPLITS), axis=-1)
        qa = qa.reshape(bp, tp, A_HEADS, 2, A_HEAD_DIM)
        ka = ka.reshape(bp, tp, A_HEADS, 2, A_HEAD_DIM)
        va = va.reshape(bp, tp, A_HEADS, A_V_DIM)
        q = (rmsnorm(cq, g_cq) @ w_uq).reshape(bp, tp, B_HEADS, B_NOPE_DIM + B_ROPE_DIM)
        kv = (rmsnorm(ckv, g_ckv) @ w_ukv).reshape(bp, tp, B_HEADS, B_NOPE_DIM + B_V_DIM)
        qn, qr = q[..., :B_NOPE_DIM], q[..., B_NOPE_DIM:]
        kn, vb = kv[..., :B_NOPE_DIM], kv[..., B_NOPE_DIM:]
        if rotary:
            qa = apply_rope(qa, cos_a, sin_a)
            ka = apply_rope(ka, cos_a, sin_a)
            qr = apply_rope(qr, cos_b, sin_b)
            kr = apply_rope(kr, cos_b, sin_b)
        return qa, ka, va, qn, qr, kn, kr, vb

    qa, ka, va, qn, qr, kn, kr, vb = project(modulate(rmsnorm(x, g_attn), sh_a, sc_a), True)
    qac, kac, vac, qnc, qrc, knc, krc, vbc = project(modulate(rmsnorm(xc, g_attn), csh_a, csc_a), False)

    lambda_init = 0.8 - 0.6 * math.exp(-0.3 * layer_idx)
    lam = (jnp.exp(jnp.sum(lam_q1.astype(jnp.float32) * lam_k1.astype(jnp.float32)))
           - jnp.exp(jnp.sum(lam_q2.astype(jnp.float32) * lam_k2.astype(jnp.float32))) + lambda_init)

    k1_all = jnp.concatenate([kac[..., 0, :], ka[..., 0, :]], axis=1)
    k2_all = jnp.concatenate([kac[..., 1, :], ka[..., 1, :]], axis=1)
    va_all = jnp.concatenate([vac, va], axis=1)
    kn_all = jnp.concatenate([knc, kn], axis=1)
    kr_all = jnp.concatenate([krc, kr], axis=1)
    vb_all = jnp.concatenate([vbc, vb], axis=1)

    def merge(oa, ob):
        bo, to = oa.shape[:2]
        oa = rmsnorm(oa, g_subln) * (1.0 - lambda_init)
        return jnp.concatenate([oa.reshape(bo, to, -1), ob.reshape(bo, to, -1)], axis=-1) @ w_out

    o_a = sweep_query_blocks(lambda q: diff_core(q[..., 0, :], q[..., 1, :], k1_all, k2_all, va_all, lam), qa)
    o_b = sweep_query_blocks(lambda q_n, q_r: mla_core(q_n, q_r, kn_all, kr_all, vb_all), qn, qr)
    x = x + gt_a[:, None, :] * merge(o_a, o_b)
    x = ffn_sublayer(x, g_ffn, sh_f, sc_f, gt_f, w1, w3, w2)
    if last:
        return x, None
    o_ac = diff_core(qac[..., 0, :], qac[..., 1, :], kac[..., 0, :], kac[..., 1, :], vac, lam)
    o_bc = mla_core(qnc, qrc, knc, krc, vbc)
    xc = xc + cgt_a[:, None, :] * merge(o_ac, o_bc)
    xc = ffn_sublayer(xc, g_ffn, csh_f, csc_f, cgt_f, w1, w3, w2)
    return x, xc


def cd_layer(x, xc, c, c_ctx, w_mod, b_mod, g_attn, w_in, g_qc, g_kc, rpb, w_out, g_ffn, w1, w3, w2, last):
    n_tokens = x.shape[1]
    cos_c, sin_c = axial_rope_tables(C_HEAD_DIM, n_tokens)
    sh_a, sc_a, gt_a, sh_f, sc_f, gt_f = ada_params(c, w_mod, b_mod)
    csh_a, csc_a, cgt_a, csh_f, csc_f, cgt_f = ada_params(c_ctx[None], w_mod, b_mod)

    def project(h, rotary):
        bp, tp, _ = h.shape
        qc, kc, vc, qd, kd, vd = jnp.split(h @ w_in, split_points(CD_SPLITS), axis=-1)
        qc = rmsnorm(qc.reshape(bp, tp, C_HEADS, C_HEAD_DIM), g_qc)
        kc = rmsnorm(kc.reshape(bp, tp, C_KV_HEADS, C_HEAD_DIM), g_kc)
        vc = vc.reshape(bp, tp, C_KV_HEADS, C_HEAD_DIM)
        qd = qd.reshape(bp, tp, D_HEADS, D_HEAD_DIM)
        kd = kd.reshape(bp, tp, D_HEADS, D_HEAD_DIM)
        vd = vd.reshape(bp, tp, D_HEADS, D_HEAD_DIM)
        if rotary:
            qc = apply_rope(qc, cos_c, sin_c)
            kc = apply_rope(kc, cos_c, sin_c)
        return qc, kc, vc, qd, kd, vd

    qc, kc, vc, qd, kd, vd = project(modulate(rmsnorm(x, g_attn), sh_a, sc_a), True)
    qcc, kcc, vcc, qdc, kdc, vdc = project(modulate(rmsnorm(xc, g_attn), csh_a, csc_a), False)

    kc_all = jnp.concatenate([kcc, kc], axis=1)
    vc_all = jnp.concatenate([vcc, vc], axis=1)

    def merge(oc, od):
        bo, to = oc.shape[:2]
        return jnp.concatenate([oc.reshape(bo, to, -1), od.reshape(bo, to, -1)], axis=-1) @ w_out

    o_c = sweep_query_blocks(lambda q: gqa_core(q, kc_all, vc_all), qc)
    o_d = neighbourhood_attn(qd, kd, vd, kdc, vdc, rpb)
    x = x + gt_a[:, None, :] * merge(o_c, o_d)
    x = ffn_sublayer(x, g_ffn, sh_f, sc_f, gt_f, w1, w3, w2)
    if last:
        return x, None
    o_cc = gqa_core(qcc, kcc, vcc)
    o_dc = gqa_core(qdc, kdc, vdc)
    xc = xc + cgt_a[:, None, :] * merge(o_cc, o_dc)
    xc = ffn_sublayer(xc, g_ffn, csh_f, csc_f, cgt_f, w1, w3, w2)
    return x, xc


def setup_inputs(seed: int = 0) -> dict:
    key = jax.random.key(seed)
    ks = iter(jax.random.split(key, 40))

    def nrm(shape, scale=1.0):
        return scale * jax.random.normal(next(ks), shape, jnp.float32)

    def gain(n):
        return 1.0 + 0.1 * nrm((n,))

    d = D_MODEL
    mod_scale = 0.5 * d ** -0.5
    inp = {}
    inp['x'] = nrm((BATCH, SEQ, d))
    inp['c'] = nrm((BATCH, d))
    inp['ctx'] = nrm((BATCH, CTX_LEN, d))
    inp['c_ctx'] = nrm((d,))
    inp['l0_w_mod'] = nrm((d, 6 * d), mod_scale)
    inp['l0_b_mod'] = nrm((6 * d,), 0.02)
    inp['l0_g_attn'] = gain(d)
    inp['l0_w_in'] = nrm((d, AB_IN), d ** -0.5)
    inp['l0_lam_q1'] = nrm((A_HEAD_DIM,), 0.1)
    inp['l0_lam_k1'] = nrm((A_HEAD_DIM,), 0.1)
    inp['l0_lam_q2'] = nrm((A_HEAD_DIM,), 0.1)
    inp['l0_lam_k2'] = nrm((A_HEAD_DIM,), 0.1)
    inp['l0_g_subln'] = gain(A_V_DIM)
    inp['l0_g_cq'] = gain(B_Q_RANK)
    inp['l0_w_uq'] = nrm((B_Q_RANK, B_HEADS * (B_NOPE_DIM + B_ROPE_DIM)), B_Q_RANK ** -0.5)
    inp['l0_g_ckv'] = gain(B_KV_RANK)
    inp['l0_w_ukv'] = nrm((B_KV_RANK, B_HEADS * (B_NOPE_DIM + B_V_DIM)), B_KV_RANK ** -0.5)
    inp['l0_w_out'] = nrm((AB_OUT, d), AB_OUT ** -0.5)
    inp['l0_g_ffn'] = gain(d)
    inp['l0_w1'] = nrm((d, FFN_HIDDEN), d ** -0.5)
    inp['l0_w3'] = nrm((d, FFN_HIDDEN), d ** -0.5)
    inp['l0_w2'] = nrm((FFN_HIDDEN, d), FFN_HIDDEN ** -0.5)
    inp['l1_w_mod'] = nrm((d, 6 * d), mod_scale)
    inp['l1_b_mod'] = nrm((6 * d,), 0.02)
    inp['l1_g_attn'] = gain(d)
    inp['l1_w_in'] = nrm((d, CD_IN), d ** -0.5)
    inp['l1_g_qc'] = gain(C_HEAD_DIM)
    inp['l1_g_kc'] = gain(C_HEAD_DIM)
    inp['l1_rpb'] = nrm((D_HEADS, 2 * NA_WIN_H - 1, 2 * NA_WIN_W - 1), 0.1)
    inp['l1_w_out'] = nrm((CD_OUT, d), CD_OUT ** -0.5)
    inp['l1_g_ffn'] = gain(d)
    inp['l1_w1'] = nrm((d, FFN_HIDDEN), d ** -0.5)
    inp['l1_w3'] = nrm((d, FFN_HIDDEN), d ** -0.5)
    inp['l1_w2'] = nrm((FFN_HIDDEN, d), FFN_HIDDEN ** -0.5)
    inp['g_final'] = gain(d)
    return inp


def reference(x, c, ctx, c_ctx,
              l0_w_mod, l0_b_mod, l0_g_attn, l0_w_in, l0_lam_q1, l0_lam_k1, l0_lam_q2, l0_lam_k2,
              l0_g_subln, l0_g_cq, l0_w_uq, l0_g_ckv, l0_w_ukv, l0_w_out, l0_g_ffn, l0_w1, l0_w3, l0_w2,
              l1_w_mod, l1_b_mod, l1_g_attn, l1_w_in, l1_g_qc, l1_g_kc, l1_rpb, l1_w_out, l1_g_ffn,
              l1_w1, l1_w3, l1_w2, g_final):
    layers = (
        (l0_w_mod, l0_b_mod, l0_g_attn, l0_w_in, l0_lam_q1, l0_lam_k1, l0_lam_q2, l0_lam_k2, l0_g_subln,
         l0_g_cq, l0_w_uq, l0_g_ckv, l0_w_ukv, l0_w_out, l0_g_ffn, l0_w1, l0_w3, l0_w2),
        (l1_w_mod, l1_b_mod, l1_g_attn, l1_w_in, l1_g_qc, l1_g_kc, l1_rpb, l1_w_out, l1_g_ffn,
         l1_w1, l1_w3, l1_w2),
    )
    xc = ctx
    for i in range(DEPTH):
        last = i == DEPTH - 1
        if i % 2 == 0:
            x, xc = ab_layer(x, xc, c, c_ctx, *layers[i], layer_idx=i, last=last)
        else:
            x, xc = cd_layer(x, xc, c, c_ctx, *layers[i], last=last)
    return rmsnorm(x, g_final)
```

```python
import functools
import math

import numpy as np
import jax
import jax.numpy as jnp
from jax import lax
from jax.experimental import pallas as pl
from jax.experimental.pallas import tpu as pltpu

D_MODEL = 1024
BATCH = 4
SEQ = 4096
GRID_W = 64
CTX_LEN = 256
ROPE_THETA = 10000.0
EPS = 1e-6
A_HEADS = 4
A_HEAD_DIM = 64
B_HEADS = 8
B_Q_RANK = 256
B_KV_RANK = 128
B_NOPE_DIM = 64
B_ROPE_DIM = 32
B_V_DIM = 64
C_HEADS = 8
C_KV_HEADS = 2
C_HEAD_DIM = 64
D_HEADS = 8
D_HEAD_DIM = 64
NA_WIN_H = 8
NA_WIN_W = 16
FFN_HIDDEN = 2816

S_ALL = CTX_LEN + SEQ
LANES = 128
ROW_TILE = 256
N_TILES = S_ALL // ROW_TILE
N_LAT_TILES = SEQ // ROW_TILE
ROWS_PER_TILE = ROW_TILE // GRID_W
NA_KEY_ROWS = 12
NA_KEYS = NA_KEY_ROWS * GRID_W
LOG2E = math.log2(math.e)
NEG_BIG = -1e30
VMEM_LIMIT = 52 * 1024 * 1024

assert ROW_TILE == CTX_LEN and S_ALL % ROW_TILE == 0

_BF = jnp.bfloat16
_F32 = jnp.float32


def _dot(a, b):
    return jnp.dot(a, b, preferred_element_type=_F32)


def _rms(x):
    return x * lax.rsqrt(jnp.mean(x * x, axis=-1, keepdims=True) + EPS)


def _lane(shape):
    return lax.broadcasted_iota(jnp.int32, shape, len(shape) - 1)


def _swap_halves(x, half):
    n = x.shape[-1]
    up = pltpu.roll(x, n - half, axis=1)
    down = pltpu.roll(x, half, axis=1)
    return jnp.where((_lane(x.shape) % (2 * half)) < half, up, down)


def _rope(x, c, s, half):
    return x * c + _swap_halves(x, half) * s


def _group_mean_sq(x, bd):
    x2 = x * x
    hi = x2.astype(_BF)
    lo = (x2 - hi.astype(_F32)).astype(_BF)
    return _dot(hi, bd) + _dot(lo, bd)


def _ada_body(cond_ref, w_ref, b_ref, o_ref):
    c = cond_ref[...]
    a = (c * (1.0 / (1.0 + jnp.exp(-c)))).astype(_BF)
    o_ref[...] = _dot(a, w_ref[...].astype(_BF)) + b_ref[...]


def _ada_params(cond, w_mod, b_mod):
    n = w_mod.shape[1]
    tn = n // 4
    out = pl.pallas_call(
        _ada_body,
        grid=(n // tn,),
        in_specs=[pl.BlockSpec((8, D_MODEL), lambda j: (0, 0)),
                  pl.BlockSpec((D_MODEL, tn), lambda j: (0, j)),
                  pl.BlockSpec((1, tn), lambda j: (0, j))],
        out_specs=pl.BlockSpec((8, tn), lambda j: (0, j)),
        out_shape=jax.ShapeDtypeStruct((8, n), _F32),
        compiler_params=pltpu.CompilerParams(vmem_limit_bytes=VMEM_LIMIT),
        name="ada_params",
    )(cond, w_mod, b_mod.reshape(1, n))
    return out.reshape(8, 6, D_MODEL)


def _mod_row(b, t):
    return jnp.where(t == 0, BATCH, b)


def _const_spec(shape):
    nd = len(shape)
    return pl.BlockSpec(shape, lambda *_: (0,) * nd, pipeline_mode=pl.Buffered(1))


def _modulated(x, m, g, shift_row, scale_row):
    h = _rms(x) * g
    return h * (1.0 + m[scale_row:scale_row + 1]) + m[shift_row:shift_row + 1]


def _proj0_body(x_ref, mod_ref, g_ref, w_in_ref, ca_ref, sa_ref, cb_ref, sb_ref,
                g_cq_ref, w_uq_ref, g_ckv_ref, w_ukv_ref,
                qd_ref, kdt_ref, vd_ref, qm_ref, kmt_ref, vm_ref):
    h = _modulated(x_ref[0], mod_ref[0], g_ref[...], 0, 1).astype(_BF)
    y = _dot(h, w_in_ref[...])
    ca, sa, cb, sb = ca_ref[...], sa_ref[...], cb_ref[...], sb_ref[...]
    qscale = A_HEAD_DIM ** -0.5 * LOG2E
    for hd in range(A_HEADS):
        lo = hd * LANES
        q = _rope(y[:, lo:lo + LANES], ca, sa, A_HEAD_DIM // 2) * qscale
        qd_ref[0, hd] = q.astype(_BF)
        k = _rope(y[:, 512 + lo:512 + lo + LANES], ca, sa, A_HEAD_DIM // 2)
        kdt_ref[0, hd] = k.T.astype(_BF)
        vd_ref[0, hd] = y[:, 1024 + lo:1024 + lo + LANES].astype(_BF)

    cq = (_rms(y[:, 1536:1792]) * g_cq_ref[...]).astype(_BF)
    ckv = (_rms(y[:, 1792:1920]) * g_ckv_ref[...]).astype(_BF)
    kr = _rope(y[:, 1920:2048], cb, sb, B_ROPE_DIM // 2)
    qm = _dot(cq, w_uq_ref[...])
    kv = _dot(ckv, w_ukv_ref[...])
    mscale = (B_NOPE_DIM + B_ROPE_DIM) ** -0.5 * LOG2E
    for hd in range(B_HEADS):
        lo = hd * LANES
        q = _rope(qm[:, lo:lo + LANES], cb, sb, B_ROPE_DIM // 2) * mscale
        qm_ref[0, hd] = q.astype(_BF)
        kmt_ref[0, hd] = (kv[:, lo:lo + LANES] + kr).T.astype(_BF)
    for pr in range(B_HEADS // 2):
        lo = B_HEADS * LANES + pr * LANES
        vm_ref[0, pr] = kv[:, lo:lo + LANES].astype(_BF)


def _head_out(n_heads, transposed):
    if transposed:
        return (jax.ShapeDtypeStruct((BATCH, n_heads, LANES, S_ALL), _BF),
                pl.BlockSpec((1, n_heads, LANES, ROW_TILE), lambda b, t: (b, 0, 0, t)))
    return (jax.ShapeDtypeStruct((BATCH, n_heads, S_ALL, LANES), _BF),
            pl.BlockSpec((1, n_heads, ROW_TILE, LANES), lambda b, t: (b, 0, t, 0)))


def _row_spec(width):
    return pl.BlockSpec((1, ROW_TILE, width), lambda b, t: (b, t, 0))


def _mod_spec():
    return pl.BlockSpec((1, 6, D_MODEL), lambda b, t: (_mod_row(b, t), 0, 0))


def _table_spec():
    return pl.BlockSpec((ROW_TILE, LANES), lambda b, t: (t, 0))


def _project0(xs, mods, g_attn, w_in, tabs, g_cq, w_uq, g_ckv, w_ukv):
    outs = [_head_out(4, False), _head_out(4, True), _head_out(4, False),
            _head_out(8, False), _head_out(8, True), _head_out(4, False)]
    return pl.pallas_call(
        _proj0_body,
        grid=(BATCH, N_TILES),
        in_specs=[_row_spec(D_MODEL), _mod_spec(), _const_spec((1, D_MODEL)),
                  _const_spec(w_in.shape),
                  _table_spec(), _table_spec(), _table_spec(), _table_spec(),
                  _const_spec((1, B_Q_RANK)), _const_spec(w_uq.shape),
                  _const_spec((1, B_KV_RANK)), _const_spec(w_ukv.shape)],
        out_specs=[o[1] for o in outs],
        out_shape=[o[0] for o in outs],
        compiler_params=pltpu.CompilerParams(vmem_limit_bytes=VMEM_LIMIT),
        name="project0",
    )(xs, mods, g_attn.reshape(1, -1), w_in, *tabs,
      g_cq.reshape(1, -1), w_uq, g_ckv.reshape(1, -1), w_ukv)


def _proj1_body(x_ref, mod_ref, g_ref, w_in_ref, ca_ref, sa_ref, bd_ref,
                g_qc_ref, g_kc_ref,
                qc_ref, kct_ref, vc_ref, qn_ref, knt_ref, vn_ref):
    h = _modulated(x_ref[0], mod_ref[0], g_ref[...], 0, 1).astype(_BF)
    y = _dot(h, w_in_ref[...])
    ca, sa = ca_ref[...], sa_ref[...]
    bd = bd_ref[...]
    qscale = C_HEAD_DIM ** -0.5 * LOG2E

    qc = y[:, 0:512]
    qc = qc * lax.rsqrt(_group_mean_sq(qc, bd) + EPS) * g_qc_ref[...]
    kc = y[:, 512:640]
    kc = kc * lax.rsqrt(_group_mean_sq(kc, bd[:LANES, :LANES]) + EPS) * g_kc_ref[...]
    kct_ref[0, 0] = _rope(kc, ca, sa, C_HEAD_DIM // 2).T.astype(_BF)
    vc_ref[0, 0] = y[:, 640:768].astype(_BF)
    nscale = D_HEAD_DIM ** -0.5 * LOG2E
    for pr in range(4):
        lo = pr * LANES
        q = _rope(qc[:, lo:lo + LANES], ca, sa, C_HEAD_DIM // 2) * qscale
        qc_ref[0, pr] = q.astype(_BF)
        qn_ref[0, pr] = (y[:, 768 + lo:768 + lo + LANES] * nscale).astype(_BF)
        knt_ref[0, pr] = y[:, 1280 + lo:1280 + lo + LANES].T.astype(_BF)
        vn_ref[0, pr] = y[:, 1792 + lo:1792 + lo + LANES].astype(_BF)


def _project1(xs, mods, g_attn, w_in, tabs, bd, g_qc, g_kc):
    outs = [_head_out(4, False), _head_out(1, True), _head_out(1, False),
            _head_out(4, False), _head_out(4, True), _head_out(4, False)]
    return pl.pallas_call(
        _proj1_body,
        grid=(BATCH, N_TILES),
        in_specs=[_row_spec(D_MODEL), _mod_spec(), _const_spec((1, D_MODEL)),
                  _const_spec(w_in.shape), _table_spec(), _table_spec(),
                  _const_spec(bd.shape),
                  _const_spec((1, 512)), _const_spec((1, LANES))],
        out_specs=[o[1] for o in outs],
        out_shape=[o[0] for o in outs],
        compiler_params=pltpu.CompilerParams(vmem_limit_bytes=VMEM_LIMIT),
        name="project1",
    )(xs, mods, g_attn.reshape(1, -1), w_in, tabs[0], tabs[1], bd,
      jnp.tile(g_qc, 8).reshape(1, -1), jnp.tile(g_kc, 2).reshape(1, -1))


def _softmax_pv(q, kt, v):
    s = _dot(q, kt)
    e = jnp.exp2(s - jnp.max(s, axis=-1, keepdims=True))
    return _dot(e.astype(_BF), v), jnp.sum(e, axis=-1, keepdims=True)


def _split_halves(q):
    lane = _lane(q.shape)
    zero = jnp.zeros_like(q)
    return jnp.where(lane < 64, q, zero), jnp.where(lane >= 64, q, zero)


def _diff_body(q_ref, kt_ref, v_ref, lam_ref, g_ref, o_ref):
    i = pl.program_id(2)
    lam_init = 0.8 - 0.6 * math.exp(-0.3 * 0)
    lv = lam_ref[...]
    lam = (jnp.exp(jnp.sum(lv[0:1] * lv[1:2], axis=-1, keepdims=True))
           - jnp.exp(jnp.sum(lv[2:3] * lv[3:4], axis=-1, keepdims=True)) + lam_init)
    q1, q2 = _split_halves(q_ref[0, 0])

    def attend(kt, v):
        o1, l1 = _softmax_pv(q1, kt, v)
        o2, l2 = _softmax_pv(q2, kt, v)
        o = o1 / l1 - lam * (o2 / l2)
        o_ref[0] = (_rms(o) * g_ref[...] * (1.0 - lam_init)).astype(_BF)

    @pl.when(i == 0)
    def _():
        attend(kt_ref[0, 0, :, :CTX_LEN], v_ref[0, 0, :CTX_LEN, :])

    @pl.when(i > 0)
    def _():
        attend(kt_ref[0, 0], v_ref[0, 0])


def _pair_body(qa_ref, qb_ref, kat_ref, kbt_ref, v_ref, o_ref, *, split_q, ctx_tile):
    i = pl.program_id(2)
    if split_q:
        qa, qb = _split_halves(qa_ref[0, 0])
    else:
        qa, qb = qa_ref[0, 0], qb_ref[0, 0]

    def attend(sl_k, sl_v):
        oa, la = _softmax_pv(qa, kat_ref[0, 0][sl_k], v_ref[0, 0][sl_v])
        ob, lb = _softmax_pv(qb, kbt_ref[0, 0][sl_k], v_ref[0, 0][sl_v])
        o_ref[0] = jnp.where(_lane(oa.shape) < 64, oa / la, ob / lb).astype(_BF)

    full = (slice(None), slice(None))
    if ctx_tile:
        @pl.when(i == 0)
        def _():
            attend((slice(None), slice(0, CTX_LEN)), (slice(0, CTX_LEN), slice(None)))

        @pl.when(i > 0)
        def _():
            attend(full, full)
    else:
        attend(full, full)


def _na_body(q_ref, kt_ref, v_ref, bias_ref, o_ref):
    i = pl.program_id(2)
    start = pl.multiple_of(CTX_LEN + jnp.clip(i - 1, 0, N_LAT_TILES - 3) * ROW_TILE, ROW_TILE)
    qa, qb = _split_halves(q_ref[0, 0])
    kt_ctx = kt_ref[0, 0, :, 0:CTX_LEN]
    kt_loc = kt_ref[0, 0, :, pl.ds(start, NA_KEYS)]
    v_ctx = v_ref[0, 0, 0:CTX_LEN, :]
    v_loc = v_ref[0, 0, pl.ds(start, NA_KEYS), :]

    def one(q, bias):
        s_loc = _dot(q, kt_loc) + bias
        s_ctx = _dot(q, kt_ctx)
        mx = jnp.maximum(jnp.max(s_loc, axis=-1, keepdims=True),
                         jnp.max(s_ctx, axis=-1, keepdims=True))
        e_loc = jnp.exp2(s_loc - mx)
        e_ctx = jnp.exp2(s_ctx - mx)
        l = jnp.sum(e_loc, axis=-1, keepdims=True) + jnp.sum(e_ctx, axis=-1, keepdims=True)
        return (_dot(e_loc.astype(_BF), v_loc) + _dot(e_ctx.astype(_BF), v_ctx)) / l

    oa = one(qa, bias_ref[0, 0, 0])
    ob = one(qb, bias_ref[0, 0, 1])
    o_ref[0] = jnp.where(_lane(oa.shape) < 64, oa, ob).astype(_BF)


def _kv_specs(n):
    kt = [pl.BlockSpec((1, 1, LANES, S_ALL), f) for f in n[0]]
    v = pl.BlockSpec((1, 1, S_ALL, LANES), n[1])
    return kt, v


def _attn_call(body, name, q_tiles, q_off, q_maps, kt_maps, v_map, arrays, extra_specs=()):
    qs = [pl.BlockSpec((1, 1, ROW_TILE, LANES), f) for f in q_maps]
    kts = [pl.BlockSpec((1, 1, LANES, S_ALL), f) for f in kt_maps]
    v = pl.BlockSpec((1, 1, S_ALL, LANES), v_map)
    return pl.pallas_call(
        body,
        grid=(BATCH, 4, q_tiles),
        in_specs=qs + kts + [v] + list(extra_specs),
        out_specs=pl.BlockSpec((1, ROW_TILE, LANES), lambda b, g, i: (b, i, g)),
        out_shape=jax.ShapeDtypeStruct((BATCH, q_tiles * ROW_TILE, 4 * LANES), _BF),
        compiler_params=pltpu.CompilerParams(vmem_limit_bytes=VMEM_LIMIT),
        name=name,
    )(*arrays)


def _diff_attention(qd, kdt, vd, lamv, g_subln):
    same = lambda b, g, i: (b, g, i, 0)
    head = lambda b, g, i: (b, g, 0, 0)
    return _attn_call(_diff_body, "diff_attention", N_TILES, 0, [same], [head], head,
                      (qd, kdt, vd, lamv, g_subln.reshape(1, -1)),
                      extra_specs=[pl.BlockSpec((4, A_HEAD_DIM), lambda b, g, i: (0, 0)),
                                   pl.BlockSpec((1, LANES), lambda b, g, i: (0, 0))])


def _mla_attention(qm, kmt, vm):
    body = functools.partial(_pair_body, split_q=False, ctx_tile=True)
    return _attn_call(body, "mla_attention", N_TILES, 0,
                      [lambda b, g, i: (b, 2 * g, i, 0), lambda b, g, i: (b, 2 * g + 1, i, 0)],
                      [lambda b, g, i: (b, 2 * g, 0, 0), lambda b, g, i: (b, 2 * g + 1, 0, 0)],
                      lambda b, g, i: (b, g, 0, 0),
                      (qm, qm, kmt, kmt, vm))


def _gqa_attention(qc, kct, vc):
    body = functools.partial(_pair_body, split_q=True, ctx_tile=False)
    lat = lambda b, g, i: (b, g, i + 1, 0)
    shared = lambda b, g, i: (b, 0, 0, 0)
    return _attn_call(body, "gqa_attention", N_LAT_TILES, 1, [lat, lat], [shared, shared], shared,
                      (qc, qc, kct, kct, vc))


def _na_attention(qn, knt, vn, bias):
    lat = lambda b, g, i: (b, g, i + 1, 0)
    head = lambda b, g, i: (b, g, 0, 0)

    def bias_map(b, g, i):
        return (g, jnp.where(i == 0, 0, jnp.where(i == N_LAT_TILES - 1, 2, 1)), 0, 0, 0)

    return _attn_call(_na_body, "na_attention", N_LAT_TILES, 1, [lat], [head], head,
                      (qn, knt, vn, bias),
                      extra_specs=[pl.BlockSpec((1, 1, 2, ROW_TILE, NA_KEYS), bias_map)])


def _merge_ffn_body(x_ref, oa_ref, ob_ref, mod_ref, woa_ref, wob_ref, g_ref,
                    w1_ref, w3_ref, w2_ref, gf_ref, o_ref, *, final):
    m = mod_ref[0]
    attn = _dot(oa_ref[0], woa_ref[...]) + _dot(ob_ref[0], wob_ref[...])
    x1 = x_ref[0] + m[2:3] * attn
    h = _modulated(x1, m, g_ref[...], 3, 4).astype(_BF)
    a = _dot(h, w1_ref[...])
    b = _dot(h, w3_ref[...])
    u = (a * (1.0 / (1.0 + jnp.exp(-a))) * b).astype(_BF)
    x2 = x1 + m[5:6] * _dot(u, w2_ref[...])
    if final:
        x2 = _rms(x2) * gf_ref[...]
    o_ref[0] = x2


def _merge_ffn(xs, oa, ob, mods, woa, wob, g_ffn, w1, w3, w2, g_final, *, latent_only):
    off = 1 if latent_only else 0
    tiles = N_LAT_TILES if latent_only else N_TILES
    row = lambda w, o: pl.BlockSpec((1, ROW_TILE, w), lambda b, t: (b, t + o, 0))
    mod = pl.BlockSpec((1, 6, D_MODEL), lambda b, t: (_mod_row(b, t + off), 0, 0))
    return pl.pallas_call(
        functools.partial(_merge_ffn_body, final=latent_only),
        grid=(BATCH, tiles),
        in_specs=[row(D_MODEL, off), row(512, 0), row(512, 0), mod,
                  _const_spec(woa.shape), _const_spec(wob.shape), _const_spec((1, D_MODEL)),
                  _const_spec(w1.shape), _const_spec(w3.shape), _const_spec(w2.shape),
                  _const_spec((1, D_MODEL))],
        out_specs=row(D_MODEL, 0),
        out_shape=jax.ShapeDtypeStruct((BATCH, tiles * ROW_TILE, D_MODEL), _F32),
        compiler_params=pltpu.CompilerParams(vmem_limit_bytes=VMEM_LIMIT),
        name="merge_ffn_final" if latent_only else "merge_ffn",
    )(xs, oa, ob, mods, woa, wob, g_ffn.reshape(1, -1), w1, w3, w2, g_final.reshape(1, -1))


def _rope_tables():
    t = np.arange(SEQ)
    rows = jnp.asarray(t // GRID_W, _F32)
    cols = jnp.asarray(t % GRID_W, _F32)

    def cs(rot_dim):
        axis_dim = rot_dim // 2
        inv = ROPE_THETA ** (-jnp.arange(0, axis_dim, 2, dtype=_F32) / axis_dim)
        ang = jnp.concatenate([rows[:, None] * inv, cols[:, None] * inv], axis=-1)
        return jnp.cos(ang), jnp.sin(ang)

    def with_ctx(tab, fill):
        return jnp.concatenate([jnp.full((CTX_LEN, LANES), fill, _F32), tab], axis=0)

    cos, sin = cs(A_HEAD_DIM)
    ca = with_ctx(jnp.tile(cos, (1, 4)), 1.0)
    sa = with_ctx(jnp.tile(jnp.concatenate([-sin, sin], axis=-1), (1, 2)), 0.0)
    cos, sin = cs(B_ROPE_DIM)
    ones = jnp.ones((SEQ, 64), _F32)
    zeros = jnp.zeros((SEQ, 64), _F32)
    cb = with_ctx(jnp.concatenate([ones, cos, cos, ones[:, :32]], axis=-1), 1.0)
    sb = with_ctx(jnp.concatenate([zeros, -sin, sin, zeros[:, :32]], axis=-1), 0.0)
    return ca, sa, cb, sb


def _na_bias_table(rpb):
    rows_n = SEQ // GRID_W
    kinds = [(0, 0), (ROWS_PER_TILE, 0), (rows_n - ROWS_PER_TILE, rows_n - NA_KEY_ROWS)]
    rel_r, rel_c, valid = [], [], []
    ql = np.arange(ROW_TILE)
    kl = np.arange(NA_KEYS)
    for q_row0, k_row0 in kinds:
        r = (q_row0 + ql // GRID_W)[:, None]
        c = (ql % GRID_W)[:, None]
        kr = (k_row0 + kl // GRID_W)[None, :]
        kc = (kl % GRID_W)[None, :]
        rs = np.clip(r - NA_WIN_H // 2, 0, rows_n - NA_WIN_H)
        cs = np.clip(c - NA_WIN_W // 2, 0, GRID_W - NA_WIN_W)
        ok = (kr >= rs) & (kr < rs + NA_WIN_H) & (kc >= cs) & (kc < cs + NA_WIN_W)
        valid.append(ok)
        rel_r.append(np.where(ok, kr - r + NA_WIN_H - 1, 0))
        rel_c.append(np.where(ok, kc - c + NA_WIN_W - 1, 0))
    rel_r, rel_c, valid = np.stack(rel_r), np.stack(rel_c), np.stack(valid)
    bias = rpb[:, rel_r, rel_c] * LOG2E
    bias = jnp.where(valid[None], bias, NEG_BIG)
    return bias.reshape(4, 2, 3, ROW_TILE, NA_KEYS).transpose(0, 2, 1, 3, 4)


def _pad_cols(w, groups, width, total, offset=0):
    k = w.shape[0]
    w = w.reshape(k, groups, width)
    w = jnp.pad(w, ((0, 0), (0, 0), (offset, total - width - offset)))
    return w.reshape(k, groups * total)


def kernel(x, c, ctx, c_ctx, l0_w_mod, l0_b_mod, l0_g_attn, l0_w_in, l0_lam_q1, l0_lam_k1, l0_lam_q2, l0_lam_k2, l0_g_subln, l0_g_cq, l0_w_uq, l0_g_ckv, l0_w_ukv, l0_w_out, l0_g_ffn, l0_w1, l0_w3, l0_w2, l1_w_mod, l1_b_mod, l1_g_attn, l1_w_in, l1_g_qc, l1_g_kc, l1_rpb, l1_w_out, l1_g_ffn, l1_w1, l1_w3, l1_w2, g_final):
    bf = lambda w: w.astype(_BF)
    xs = jnp.concatenate([ctx, x], axis=1)
    cond = jnp.concatenate([c, c_ctx[None], jnp.zeros((3, D_MODEL), _F32)], axis=0)
    tabs = _rope_tables()

    mods0 = _ada_params(cond, l0_w_mod, l0_b_mod)
    w_in0 = bf(jnp.concatenate([l0_w_in[:, :1920],
                                _pad_cols(l0_w_in[:, 1920:], 1, B_ROPE_DIM, LANES, B_NOPE_DIM)], axis=1))
    w_uq = bf(_pad_cols(l0_w_uq, B_HEADS, B_NOPE_DIM + B_ROPE_DIM, LANES))
    ukv = l0_w_ukv.reshape(B_KV_RANK, B_HEADS, B_NOPE_DIM + B_V_DIM)
    w_ukv = bf(jnp.concatenate([_pad_cols(ukv[:, :, :B_NOPE_DIM].reshape(B_KV_RANK, -1), B_HEADS, B_NOPE_DIM, LANES),
                                ukv[:, :, B_NOPE_DIM:].reshape(B_KV_RANK, -1)], axis=1))
    qd, kdt, vd, qm, kmt, vm = _project0(xs, mods0, l0_g_attn, w_in0, tabs,
                                         l0_g_cq, w_uq, l0_g_ckv, w_ukv)
    lamv = jnp.stack([l0_lam_q1, l0_lam_k1, l0_lam_q2, l0_lam_k2])
    o_a = _diff_attention(qd, kdt, vd, lamv, l0_g_subln)
    o_b = _mla_attention(qm, kmt, vm)
    w_out0 = bf(l0_w_out)
    xs = _merge_ffn(xs, o_a, o_b, mods0, w_out0[:512], w_out0[512:], l0_g_ffn,
                    bf(l0_w1), bf(l0_w3), bf(l0_w2), g_final, latent_only=False)

    mods1 = _ada_params(cond, l1_w_mod, l1_b_mod)
    wq = l1_w_in[:, :512].reshape(D_MODEL, 2, 4, C_HEAD_DIM).transpose(0, 2, 1, 3).reshape(D_MODEL, 512)
    w_in1 = bf(jnp.concatenate([wq, l1_w_in[:, 512:]], axis=1))
    grp = np.arange(512) // 64
    bd = jnp.asarray((grp[:, None] == grp[None, :]) / 64.0, _BF)
    qc, kct, vc, qn, knt, vn = _project1(xs, mods1, l1_g_attn, w_in1, tabs, bd, l1_g_qc, l1_g_kc)
    o_c = _gqa_attention(qc, kct, vc)
    o_d = _na_attention(qn, knt, vn, _na_bias_table(l1_rpb))
    w_out1 = bf(l1_w_out)
    woc = w_out1[:512].reshape(2, 4, C_HEAD_DIM, D_MODEL).transpose(1, 0, 2, 3).reshape(512, D_MODEL)
    return _merge_ffn(xs, o_c, o_d, mods1, woc, w_out1[512:], l1_g_ffn,
                      bf(l1_w1), bf(l1_w3), bf(l1_w2), g_final, latent_only=True)
```

```python
import functools
import math

import numpy as np
import jax
import jax.numpy as jnp
from jax import lax
from jax.experimental import pallas as pl
from jax.experimental.pallas import tpu as pltpu

D_MODEL = 1024
BATCH = 4
SEQ = 4096
GRID_W = 64
CTX_LEN = 256
ROPE_THETA = 10000.0
EPS = 1e-6
A_HEADS = 4
A_HEAD_DIM = 64
B_HEADS = 8
B_Q_RANK = 256
B_KV_RANK = 128
B_NOPE_DIM = 64
B_ROPE_DIM = 32
B_V_DIM = 64
C_HEADS = 8
C_KV_HEADS = 2
C_HEAD_DIM = 64
D_HEADS = 8
D_HEAD_DIM = 64
NA_WIN_H = 8
NA_WIN_W = 16
FFN_HIDDEN = 2816

S_ALL = CTX_LEN + SEQ
LANES = 128
ROW_TILE = 256
N_TILES = S_ALL // ROW_TILE
N_LAT_TILES = SEQ // ROW_TILE
ROWS_PER_TILE = ROW_TILE // GRID_W
NA_KEY_ROWS = 12
NA_KEYS = NA_KEY_ROWS * GRID_W
LOG2E = math.log2(math.e)
NEG_BIG = -1e30
VMEM_LIMIT = 52 * 1024 * 1024

assert ROW_TILE == CTX_LEN and S_ALL % ROW_TILE == 0

_BF = jnp.bfloat16
_F32 = jnp.float32


def _dot(a, b):
    return jnp.dot(a, b, preferred_element_type=_F32)


def _rms(x):
    return x * lax.rsqrt(jnp.mean(x * x, axis=-1, keepdims=True) + EPS)


def _lane(shape):
    return lax.broadcasted_iota(jnp.int32, shape, len(shape) - 1)


def _swap_halves(x, half):
    n = x.shape[-1]
    up = pltpu.roll(x, n - half, axis=1)
    down = pltpu.roll(x, half, axis=1)
    return jnp.where((_lane(x.shape) % (2 * half)) < half, up, down)


def _rope(x, c, s, half):
    return x * c + _swap_halves(x, half) * s


def _group_mean_sq(x, bd):
    x2 = x * x
    hi = x2.astype(_BF)
    lo = (x2 - hi.astype(_F32)).astype(_BF)
    return _dot(hi, bd) + _dot(lo, bd)


def _ada_body(cond_ref, w_ref, b_ref, o_ref):
    c = cond_ref[...]
    a = (c * (1.0 / (1.0 + jnp.exp(-c)))).astype(_BF)
    o_ref[...] = _dot(a, w_ref[...].astype(_BF)) + b_ref[...]


def _ada_params(cond, w_mod, b_mod):
    n = w_mod.shape[1]
    tn = n // 4
    out = pl.pallas_call(
        _ada_body,
        grid=(n // tn,),
        in_specs=[pl.BlockSpec((8, D_MODEL), lambda j: (0, 0)),
                  pl.BlockSpec((D_MODEL, tn), lambda j: (0, j)),
                  pl.BlockSpec((1, tn), lambda j: (0, j))],
        out_specs=pl.BlockSpec((8, tn), lambda j: (0, j)),
        out_shape=jax.ShapeDtypeStruct((8, n), _F32),
        compiler_params=pltpu.CompilerParams(vmem_limit_bytes=VMEM_LIMIT),
        name="ada_params",
    )(cond, w_mod, b_mod.reshape(1, n))
    return out.reshape(8, 6, D_MODEL)


def _mod_row(b, t):
    return jnp.where(t == 0, BATCH, b)


def _const_spec(shape):
    nd = len(shape)
    return pl.BlockSpec(shape, lambda *_: (0,) * nd, pipeline_mode=pl.Buffered(1))


def _modulated(x, m, g, shift_row, scale_row):
    h = _rms(x) * g
    return h * (1.0 + m[scale_row:scale_row + 1]) + m[shift_row:shift_row + 1]


def _proj0_body(x_ref, mod_ref, g_ref, w_in_ref, ca_ref, sa_ref, cb_ref, sb_ref,
                g_cq_ref, w_uq_ref, g_ckv_ref, w_ukv_ref,
                qd_ref, kdt_ref, vd_ref, qm_ref, kmt_ref, vm_ref):
    h = _modulated(x_ref[0], mod_ref[0], g_ref[...], 0, 1).astype(_BF)
    y = _dot(h, w_in_ref[...])
    ca, sa, cb, sb = ca_ref[...], sa_ref[...], cb_ref[...], sb_ref[...]
    qscale = A_HEAD_DIM ** -0.5 * LOG2E
    for hd in range(A_HEADS):
        lo = hd * LANES
        q = _rope(y[:, lo:lo + LANES], ca, sa, A_HEAD_DIM // 2) * qscale
        qd_ref[0, hd] = q.astype(_BF)
        k = _rope(y[:, 512 + lo:512 + lo + LANES], ca, sa, A_HEAD_DIM // 2)
        kdt_ref[0, hd] = k.T.astype(_BF)
        vd_ref[0, hd] = y[:, 1024 + lo:1024 + lo + LANES].astype(_BF)

    cq = (_rms(y[:, 1536:1792]) * g_cq_ref[...]).astype(_BF)
    ckv = (_rms(y[:, 1792:1920]) * g_ckv_ref[...]).astype(_BF)
    kr = _rope(y[:, 1920:2048], cb, sb, B_ROPE_DIM // 2)
    qm = _dot(cq, w_uq_ref[...])
    kv = _dot(ckv, w_ukv_ref[...])
    mscale = (B_NOPE_DIM + B_ROPE_DIM) ** -0.5 * LOG2E
    for hd in range(B_HEADS):
        lo = hd * LANES
        q = _rope(qm[:, lo:lo + LANES], cb, sb, B_ROPE_DIM // 2) * mscale
        qm_ref[0, hd] = q.astype(_BF)
        kmt_ref[0, hd] = (kv[:, lo:lo + LANES] + kr).T.astype(_BF)
    for pr in range(B_HEADS // 2):
        lo = B_HEADS * LANES + pr * LANES
        vm_ref[0, pr] = kv[:, lo:lo + LANES].astype(_BF)


def _head_out(n_heads, transposed):
    if transposed:
        return (jax.ShapeDtypeStruct((BATCH, n_heads, LANES, S_ALL), _BF),
                pl.BlockSpec((1, n_heads, LANES, ROW_TILE), lambda b, t: (b, 0, 0, t)))
    return (jax.ShapeDtypeStruct((BATCH, n_heads, S_ALL, LANES), _BF),
            pl.BlockSpec((1, n_heads, ROW_TILE, LANES), lambda b, t: (b, 0, t, 0)))


def _row_spec(width):
    return pl.BlockSpec((1, ROW_TILE, width), lambda b, t: (b, t, 0))


def _mod_spec():
    return pl.BlockSpec((1, 6, D_MODEL), lambda b, t: (_mod_row(b, t), 0, 0))


def _table_spec():
    return pl.BlockSpec((ROW_TILE, LANES), lambda b, t: (t, 0))


def _project0(xs, mods, g_attn, w_in, tabs, g_cq, w_uq, g_ckv, w_ukv):
    outs = [_head_out(4, False), _head_out(4, True), _head_out(4, False),
            _head_out(8, False), _head_out(8, True), _head_out(4, False)]
    return pl.pallas_call(
        _proj0_body,
        grid=(BATCH, N_TILES),
        in_specs=[_row_spec(D_MODEL), _mod_spec(), _const_spec((1, D_MODEL)),
                  _const_spec(w_in.shape),
                  _table_spec(), _table_spec(), _table_spec(), _table_spec(),
                  _const_spec((1, B_Q_RANK)), _const_spec(w_uq.shape),
                  _const_spec((1, B_KV_RANK)), _const_spec(w_ukv.shape)],
        out_specs=[o[1] for o in outs],
        out_shape=[o[0] for o in outs],
        compiler_params=pltpu.CompilerParams(vmem_limit_bytes=VMEM_LIMIT),
        name="project0",
    )(xs, mods, g_attn.reshape(1, -1), w_in, *tabs,
      g_cq.reshape(1, -1), w_uq, g_ckv.reshape(1, -1), w_ukv)


def _proj1_body(x_ref, mod_ref, g_ref, w_in_ref, ca_ref, sa_ref, bd_ref,
                g_qc_ref, g_kc_ref,
                qc_ref, kct_ref, vc_ref, qn_ref, knt_ref, vn_ref):
    h = _modulated(x_ref[0], mod_ref[0], g_ref[...], 0, 1).astype(_BF)
    y = _dot(h, w_in_ref[...])
    ca, sa = ca_ref[...], sa_ref[...]
    bd = bd_ref[...]
    qscale = C_HEAD_DIM ** -0.5 * LOG2E

    qc = y[:, 0:512]
    qc = qc * lax.rsqrt(_group_mean_sq(qc, bd) + EPS) * g_qc_ref[...]
    kc = y[:, 512:640]
    kc = kc * lax.rsqrt(_group_mean_sq(kc, bd[:LANES, :LANES]) + EPS) * g_kc_ref[...]
    kct_ref[0, 0] = _rope(kc, ca, sa, C_HEAD_DIM // 2).T.astype(_BF)
    vc_ref[0, 0] = y[:, 640:768].astype(_BF)
    nscale = D_HEAD_DIM ** -0.5 * LOG2E
    for pr in range(4):
        lo = pr * LANES
        q = _rope(qc[:, lo:lo + LANES], ca, sa, C_HEAD_DIM // 2) * qscale
        qc_ref[0, pr] = q.astype(_BF)
        qn_ref[0, pr] = (y[:, 768 + lo:768 + lo + LANES] * nscale).astype(_BF)
        knt_ref[0, pr] = y[:, 1280 + lo:1280 + lo + LANES].T.astype(_BF)
        vn_ref[0, pr] = y[:, 1792 + lo:1792 + lo + LANES].astype(_BF)


def _project1(xs, mods, g_attn, w_in, tabs, bd, g_qc, g_kc):
    outs = [_head_out(4, False), _head_out(1, True), _head_out(1, False),
            _head_out(4, False), _head_out(4, True), _head_out(4, False)]
    return pl.pallas_call(
        _proj1_body,
        grid=(BATCH, N_TILES),
        in_specs=[_row_spec(D_MODEL), _mod_spec(), _const_spec((1, D_MODEL)),
                  _const_spec(w_in.shape), _table_spec(), _table_spec(),
                  _const_spec(bd.shape),
                  _const_spec((1, 512)), _const_spec((1, LANES))],
        out_specs=[o[1] for o in outs],
        out_shape=[o[0] for o in outs],
        compiler_params=pltpu.CompilerParams(vmem_limit_bytes=VMEM_LIMIT),
        name="project1",
    )(xs, mods, g_attn.reshape(1, -1), w_in, tabs[0], tabs[1], bd,
      jnp.tile(g_qc, 8).reshape(1, -1), jnp.tile(g_kc, 2).reshape(1, -1))


def _softmax_pv(q, kt, v):
    s = _dot(q, kt)
    e = jnp.exp2(s - jnp.max(s, axis=-1, keepdims=True))
    return _dot(e.astype(_BF), v), jnp.sum(e, axis=-1, keepdims=True)


def _split_halves(q):
    lane = _lane(q.shape)
    zero = jnp.zeros_like(q)
    return jnp.where(lane < 64, q, zero), jnp.where(lane >= 64, q, zero)


def _diff_body(q_ref, kt_ref, v_ref, lam_ref, g_ref, o_ref):
    i = pl.program_id(2)
    lam_init = 0.8 - 0.6 * math.exp(-0.3 * 0)
    lv = lam_ref[...]
    lam = (jnp.exp(jnp.sum(lv[0:1] * lv[1:2], axis=-1, keepdims=True))
           - jnp.exp(jnp.sum(lv[2:3] * lv[3:4], axis=-1, keepdims=True)) + lam_init)
    q1, q2 = _split_halves(q_ref[0, 0])

    def attend(kt, v):
        o1, l1 = _softmax_pv(q1, kt, v)
        o2, l2 = _softmax_pv(q2, kt, v)
        o = o1 / l1 - lam * (o2 / l2)
        o_ref[0] = (_rms(o) * g_ref[...] * (1.0 - lam_init)).astype(_BF)

    @pl.when(i == 0)
    def _():
        attend(kt_ref[0, 0, :, :CTX_LEN], v_ref[0, 0, :CTX_LEN, :])

    @pl.when(i > 0)
    def _():
        attend(kt_ref[0, 0], v_ref[0, 0])


def _pair_body(qa_ref, qb_ref, kat_ref, kbt_ref, v_ref, o_ref, *, split_q, ctx_tile):
    i = pl.program_id(2)
    if split_q:
        qa, qb = _split_halves(qa_ref[0, 0])
    else:
        qa, qb = qa_ref[0, 0], qb_ref[0, 0]

    def attend(sl_k, sl_v):
        oa, la = _softmax_pv(qa, kat_ref[0, 0][sl_k], v_ref[0, 0][sl_v])
        ob, lb = _softmax_pv(qb, kbt_ref[0, 0][sl_k], v_ref[0, 0][sl_v])
        o_ref[0] = jnp.where(_lane(oa.shape) < 64, oa / la, ob / lb).astype(_BF)

    full = (slice(None), slice(None))
    if ctx_tile:
        @pl.when(i == 0)
        def _():
            attend((slice(None), slice(0, CTX_LEN)), (slice(0, CTX_LEN), slice(None)))

        @pl.when(i > 0)
        def _():
            attend(full, full)
    else:
        attend(full, full)


def _na_body(q_ref, kt_ref, v_ref, bias_ref, o_ref):
    i = pl.program_id(2)
    start = pl.multiple_of(CTX_LEN + jnp.clip(i - 1, 0, N_LAT_TILES - 3) * ROW_TILE, ROW_TILE)
    qa, qb = _split_halves(q_ref[0, 0])
    kt_ctx = kt_ref[0, 0, :, 0:CTX_LEN]
    kt_loc = kt_ref[0, 0, :, pl.ds(start, NA_KEYS)]
    v_ctx = v_ref[0, 0, 0:CTX_LEN, :]
    v_loc = v_ref[0, 0, pl.ds(start, NA_KEYS), :]

    def one(q, bias):
        s_loc = _dot(q, kt_loc) + bias
        s_ctx = _dot(q, kt_ctx)
        mx = jnp.maximum(jnp.max(s_loc, axis=-1, keepdims=True),
                         jnp.max(s_ctx, axis=-1, keepdims=True))
        e_loc = jnp.exp2(s_loc - mx)
        e_ctx = jnp.exp2(s_ctx - mx)
        l = jnp.sum(e_loc, axis=-1, keepdims=True) + jnp.sum(e_ctx, axis=-1, keepdims=True)
        return (_dot(e_loc.astype(_BF), v_loc) + _dot(e_ctx.astype(_BF), v_ctx)) / l

    oa = one(qa, bias_ref[0, 0, 0])
    ob = one(qb, bias_ref[0, 0, 1])
    o_ref[0] = jnp.where(_lane(oa.shape) < 64, oa, ob).astype(_BF)


def _kv_specs(n):
    kt = [pl.BlockSpec((1, 1, LANES, S_ALL), f) for f in n[0]]
    v = pl.BlockSpec((1, 1, S_ALL, LANES), n[1])
    return kt, v


def _attn_call(body, name, q_tiles, q_off, q_maps, kt_maps, v_map, arrays, extra_specs=()):
    qs = [pl.BlockSpec((1, 1, ROW_TILE, LANES), f) for f in q_maps]
    kts = [pl.BlockSpec((1, 1, LANES, S_ALL), f) for f in kt_maps]
    v = pl.BlockSpec((1, 1, S_ALL, LANES), v_map)
    return pl.pallas_call(
        body,
        grid=(BATCH, 4, q_tiles),
        in_specs=qs + kts + [v] + list(extra_specs),
        out_specs=pl.BlockSpec((1, ROW_TILE, LANES), lambda b, g, i: (b, i, g)),
        out_shape=jax.ShapeDtypeStruct((BATCH, q_tiles * ROW_TILE, 4 * LANES), _BF),
        compiler_params=pltpu.CompilerParams(vmem_limit_bytes=VMEM_LIMIT),
        name=name,
    )(*arrays)


def _diff_attention(qd, kdt, vd, lamv, g_subln):
    same = lambda b, g, i: (b, g, i, 0)
    head = lambda b, g, i: (b, g, 0, 0)
    return _attn_call(_diff_body, "diff_attention", N_TILES, 0, [same], [head], head,
                      (qd, kdt, vd, lamv, g_subln.reshape(1, -1)),
                      extra_specs=[pl.BlockSpec((4, A_HEAD_DIM), lambda b, g, i: (0, 0)),
                                   pl.BlockSpec((1, LANES), lambda b, g, i: (0, 0))])


def _mla_attention(qm, kmt, vm):
    body = functools.partial(_pair_body, split_q=False, ctx_tile=True)
    return _attn_call(body, "mla_attention", N_TILES, 0,
                      [lambda b, g, i: (b, 2 * g, i, 0), lambda b, g, i: (b, 2 * g + 1, i, 0)],
                      [lambda b, g, i: (b, 2 * g, 0, 0), lambda b, g, i: (b, 2 * g + 1, 0, 0)],
                      lambda b, g, i: (b, g, 0, 0),
                      (qm, qm, kmt, kmt, vm))


def _gqa_attention(qc, kct, vc):
    body = functools.partial(_pair_body, split_q=True, ctx_tile=False)
    lat = lambda b, g, i: (b, g, i + 1, 0)
    shared = lambda b, g, i: (b, 0, 0, 0)
    return _attn_call(body, "gqa_attention", N_LAT_TILES, 1, [lat, lat], [shared, shared], shared,
                      (qc, qc, kct, kct, vc))


def _na_attention(qn, knt, vn, bias):
    lat = lambda b, g, i: (b, g, i + 1, 0)
    head = lambda b, g, i: (b, g, 0, 0)

    def bias_map(b, g, i):
        return (g, jnp.where(i == 0, 0, jnp.where(i == N_LAT_TILES - 1, 2, 1)), 0, 0, 0)

    return _attn_call(_na_body, "na_attention", N_LAT_TILES, 1, [lat], [head], head,
                      (qn, knt, vn, bias),
                      extra_specs=[pl.BlockSpec((1, 1, 2, ROW_TILE, NA_KEYS), bias_map)])


def _merge_ffn_body(x_ref, oa_ref, ob_ref, mod_ref, woa_ref, wob_ref, g_ref,
                    w1_ref, w3_ref, w2_ref, gf_ref, o_ref, *, final):
    m = mod_ref[0]
    attn = _dot(oa_ref[0], woa_ref[...]) + _dot(ob_ref[0], wob_ref[...])
    x1 = x_ref[0] + m[2:3] * attn
    h = _modulated(x1, m, g_ref[...], 3, 4).astype(_BF)
    a = _dot(h, w1_ref[...])
    b = _dot(h, w3_ref[...])
    u = (a * (1.0 / (1.0 + jnp.exp(-a))) * b).astype(_BF)
    x2 = x1 + m[5:6] * _dot(u, w2_ref[...])
    if final:
        x2 = _rms(x2) * gf_ref[...]
    o_ref[0] = x2


def _merge_ffn(xs, oa, ob, mods, woa, wob, g_ffn, w1, w3, w2, g_final, *, latent_only):
    off = 1 if latent_only else 0
    tiles = N_LAT_TILES if latent_only else N_TILES
    row = lambda w, o: pl.BlockSpec((1, ROW_TILE, w), lambda b, t: (b, t + o, 0))
    mod = pl.BlockSpec((1, 6, D_MODEL), lambda b, t: (_mod_row(b, t + off), 0, 0))
    return pl.pallas_call(
        functools.partial(_merge_ffn_body, final=latent_only),
        grid=(BATCH, tiles),
        in_specs=[row(D_MODEL, off), row(512, 0), row(512, 0), mod,
                  _const_spec(woa.shape), _const_spec(wob.shape), _const_spec((1, D_MODEL)),
                  _const_spec(w1.shape), _const_spec(w3.shape), _const_spec(w2.shape),
                  _const_spec((1, D_MODEL))],
        out_specs=row(D_MODEL, 0),
        out_shape=jax.ShapeDtypeStruct((BATCH, tiles * ROW_TILE, D_MODEL), _F32),
        compiler_params=pltpu.CompilerParams(vmem_limit_bytes=VMEM_LIMIT),
        name="merge_ffn_final" if latent_only else "merge_ffn",
    )(xs, oa, ob, mods, woa, wob, g_ffn.reshape(1, -1), w1, w3, w2, g_final.reshape(1, -1))


def _rope_tables():
    t = np.arange(SEQ)
    rows = jnp.asarray(t // GRID_W, _F32)
    cols = jnp.asarray(t % GRID_W, _F32)

    def cs(rot_dim):
        axis_dim = rot_dim // 2
        inv = ROPE_THETA ** (-jnp.arange(0, axis_dim, 2, dtype=_F32) / axis_dim)
        ang = jnp.concatenate([rows[:, None] * inv, cols[:, None] * inv], axis=-1)
        return jnp.cos(ang), jnp.sin(ang)

    def with_ctx(tab, fill):
        return jnp.concatenate([jnp.full((CTX_LEN, LANES), fill, _F32), tab], axis=0)

    cos, sin = cs(A_HEAD_DIM)
    ca = with_ctx(jnp.tile(cos, (1, 4)), 1.0)
    sa = with_ctx(jnp.tile(jnp.concatenate([-sin, sin], axis=-1), (1, 2)), 0.0)
    cos, sin = cs(B_ROPE_DIM)
    ones = jnp.ones((SEQ, 64), _F32)
    zeros = jnp.zeros((SEQ, 64), _F32)
    cb = with_ctx(jnp.concatenate([ones, cos, cos, ones[:, :32]], axis=-1), 1.0)
    sb = with_ctx(jnp.concatenate([zeros, -sin, sin, zeros[:, :32]], axis=-1), 0.0)
    return ca, sa, cb, sb


def _na_bias_table(rpb):
    rows_n = SEQ // GRID_W
    span = 2 * GRID_W - 1
    left = GRID_W - NA_WIN_W
    p = jnp.pad(rpb * LOG2E, ((0, 0), (0, 0), (left, span - left - (2 * NA_WIN_W - 1))))
    flat = jnp.tile(p, (1, 1, GRID_W + 1))[..., :GRID_W * (span + 1)]
    toe = flat.reshape(D_HEADS, 2 * NA_WIN_H - 1, GRID_W, span + 1)[:, :, ::-1, :GRID_W]
    qc = np.arange(GRID_W)[:, None]
    kc = np.arange(GRID_W)[None, :]
    cs = np.clip(qc - NA_WIN_W // 2, 0, GRID_W - NA_WIN_W)
    toe = jnp.where((kc >= cs) & (kc < cs + NA_WIN_W), toe, NEG_BIG)
    masked = jnp.full((D_HEADS, GRID_W, GRID_W), NEG_BIG, _F32)

    kinds = [(0, 0), (ROWS_PER_TILE, 0), (rows_n - ROWS_PER_TILE, rows_n - NA_KEY_ROWS)]
    tiles = []
    for q_row0, k_row0 in kinds:
        band = []
        for qr in range(ROWS_PER_TILE):
            r = q_row0 + qr
            rs = min(max(r - NA_WIN_H // 2, 0), rows_n - NA_WIN_H)
            blocks = []
            for kr in range(NA_KEY_ROWS):
                krow = k_row0 + kr
                inside = rs <= krow < rs + NA_WIN_H
                blocks.append(toe[:, krow - r + NA_WIN_H - 1] if inside else masked)
            band.append(jnp.concatenate(blocks, axis=-1))
        tiles.append(jnp.concatenate(band, axis=-2))
    bias = jnp.stack(tiles, axis=1)
    return bias.reshape(4, 2, 3, ROW_TILE, NA_KEYS).transpose(0, 2, 1, 3, 4)


def _pad_cols(w, groups, width, total, offset=0):
    k = w.shape[0]
    w = w.reshape(k, groups, width)
    w = jnp.pad(w, ((0, 0), (0, 0), (offset, total - width - offset)))
    return w.reshape(k, groups * total)


def kernel(x, c, ctx, c_ctx, l0_w_mod, l0_b_mod, l0_g_attn, l0_w_in, l0_lam_q1, l0_lam_k1, l0_lam_q2, l0_lam_k2, l0_g_subln, l0_g_cq, l0_w_uq, l0_g_ckv, l0_w_ukv, l0_w_out, l0_g_ffn, l0_w1, l0_w3, l0_w2, l1_w_mod, l1_b_mod, l1_g_attn, l1_w_in, l1_g_qc, l1_g_kc, l1_rpb, l1_w_out, l1_g_ffn, l1_w1, l1_w3, l1_w2, g_final):
    bf = lambda w: w.astype(_BF)
    xs = jnp.concatenate([ctx, x], axis=1)
    cond = jnp.concatenate([c, c_ctx[None], jnp.zeros((3, D_MODEL), _F32)], axis=0)
    tabs = _rope_tables()

    mods0 = _ada_params(cond, l0_w_mod, l0_b_mod)
    w_in0 = bf(jnp.concatenate([l0_w_in[:, :1920],
                                _pad_cols(l0_w_in[:, 1920:], 1, B_ROPE_DIM, LANES, B_NOPE_DIM)], axis=1))
    w_uq = bf(_pad_cols(l0_w_uq, B_HEADS, B_NOPE_DIM + B_ROPE_DIM, LANES))
    ukv = l0_w_ukv.reshape(B_KV_RANK, B_HEADS, B_NOPE_DIM + B_V_DIM)
    w_ukv = bf(jnp.concatenate([_pad_cols(ukv[:, :, :B_NOPE_DIM].reshape(B_KV_RANK, -1), B_HEADS, B_NOPE_DIM, LANES),
                                ukv[:, :, B_NOPE_DIM:].reshape(B_KV_RANK, -1)], axis=1))
    qd, kdt, vd, qm, kmt, vm = _project0(xs, mods0, l0_g_attn, w_in0, tabs,
                                         l0_g_cq, w_uq, l0_g_ckv, w_ukv)
    lamv = jnp.stack([l0_lam_q1, l0_lam_k1, l0_lam_q2, l0_lam_k2])
    o_a = _diff_attention(qd, kdt, vd, lamv, l0_g_subln)
    o_b = _mla_attention(qm, kmt, vm)
    w_out0 = bf(l0_w_out)
    xs = _merge_ffn(xs, o_a, o_b, mods0, w_out0[:512], w_out0[512:], l0_g_ffn,
                    bf(l0_w1), bf(l0_w3), bf(l0_w2), g_final, latent_only=False)

    mods1 = _ada_params(cond, l1_w_mod, l1_b_mod)
    wq = l1_w_in[:, :512].reshape(D_MODEL, 2, 4, C_HEAD_DIM).transpose(0, 2, 1, 3).reshape(D_MODEL, 512)
    w_in1 = bf(jnp.concatenate([wq, l1_w_in[:, 512:]], axis=1))
    grp = np.arange(512) // 64
    bd = jnp.asarray((grp[:, None] == grp[None, :]) / 64.0, _BF)
    qc, kct, vc, qn, knt, vn = _project1(xs, mods1, l1_g_attn, w_in1, tabs, bd, l1_g_qc, l1_g_kc)
    o_c = _gqa_attention(qc, kct, vc)
    o_d = _na_attention(qn, knt, vn, _na_bias_table(l1_rpb))
    w_out1 = bf(l1_w_out)
    woc = w_out1[:512].reshape(2, 4, C_HEAD_DIM, D_MODEL).transpose(1, 0, 2, 3).reshape(512, D_MODEL)
    return _merge_ffn(xs, o_c, o_d, mods1, woc, w_out1[512:], l1_g_ffn,
                      bf(l1_w1), bf(l1_w3), bf(l1_w2), g_final, latent_only=True)
```

```python
import functools
import math

import numpy as np
import jax
import jax.numpy as jnp
from jax import lax
from jax.experimental import pallas as pl
from jax.experimental.pallas import tpu as pltpu

D_MODEL = 1024
BATCH = 4
SEQ = 4096
GRID_W = 64
CTX_LEN = 256
ROPE_THETA = 10000.0
EPS = 1e-6
A_HEADS = 4
A_HEAD_DIM = 64
B_HEADS = 8
B_Q_RANK = 256
B_KV_RANK = 128
B_NOPE_DIM = 64
B_ROPE_DIM = 32
B_V_DIM = 64
C_HEADS = 8
C_KV_HEADS = 2
C_HEAD_DIM = 64
D_HEADS = 8
D_HEAD_DIM = 64
NA_WIN_H = 8
NA_WIN_W = 16
FFN_HIDDEN = 2816
LAMBDA_INIT = 0.8 - 0.6 * math.exp(-0.3 * 0)

S_ALL = CTX_LEN + SEQ
LANES = 128
HALF = LANES // 2
ROW_TILE = 256
N_TILES = S_ALL // ROW_TILE
N_LAT_TILES = SEQ // ROW_TILE
Q2 = 2 * ROW_TILE
KEY_CHUNK = 256
N_KEY_CHUNKS = S_ALL // KEY_CHUNK
ROWS_PER_TILE = ROW_TILE // GRID_W
NA_KEY_ROWS = 12
NA_KEYS = NA_KEY_ROWS * GRID_W
LOG2E = math.log2(math.e)
NEG_BIG = -1e30
VMEM_LIMIT = 52 * 1024 * 1024

assert ROW_TILE == CTX_LEN and S_ALL % ROW_TILE == 0 and N_LAT_TILES % 2 == 0

_BF = jnp.bfloat16
_F32 = jnp.float32


def _dot(a, b):
    return jnp.dot(a, b, preferred_element_type=_F32)


def _rms(x):
    return x * lax.rsqrt(jnp.mean(x * x, axis=-1, keepdims=True) + EPS)


def _lane(shape):
    return lax.broadcasted_iota(jnp.int32, shape, len(shape) - 1)


def _swap_halves(x, half):
    n = x.shape[-1]
    up = pltpu.roll(x, n - half, axis=1)
    down = pltpu.roll(x, half, axis=1)
    return jnp.where((_lane(x.shape) % (2 * half)) < half, up, down)


def _rope(x, c, s, half):
    return x * c + _swap_halves(x, half) * s


def _group_mean_sq(x, bd):
    x2 = x * x
    hi = x2.astype(_BF)
    lo = (x2 - hi.astype(_F32)).astype(_BF)
    return _dot(hi, bd) + _dot(lo, bd)


def _t_bf(x):
    return x.T.astype(_BF)


def _ada_body(cond_ref, w_ref, b_ref, o_ref):
    c = cond_ref[...]
    a = (c * (1.0 / (1.0 + jnp.exp(-c)))).astype(_BF)
    o_ref[...] = _dot(a, w_ref[...].astype(_BF)) + b_ref[...]


def _ada_params(cond, w_mod, b_mod):
    n = w_mod.shape[1]
    tn = n // 4
    out = pl.pallas_call(
        _ada_body,
        grid=(n // tn,),
        in_specs=[pl.BlockSpec((8, D_MODEL), lambda j: (0, 0)),
                  pl.BlockSpec((D_MODEL, tn), lambda j: (0, j)),
                  pl.BlockSpec((1, tn), lambda j: (0, j))],
        out_specs=pl.BlockSpec((8, tn), lambda j: (0, j)),
        out_shape=jax.ShapeDtypeStruct((8, n), _F32),
        compiler_params=pltpu.CompilerParams(vmem_limit_bytes=VMEM_LIMIT),
        name="ada_params",
    )(cond, w_mod, b_mod.reshape(1, n))
    return out.reshape(8, 6, D_MODEL)


def _mod_row(b, t):
    return jnp.where(t == 0, BATCH, b)


def _const_spec(shape):
    nd = len(shape)
    return pl.BlockSpec(shape, lambda *_: (0,) * nd, pipeline_mode=pl.Buffered(1))


def _modulated(x, m, g, shift_row, scale_row):
    h = _rms(x) * g
    return h * (1.0 + m[scale_row:scale_row + 1]) + m[shift_row:shift_row + 1]


def _proj0_body(x_ref, mod_ref, g_ref, w_in_ref, ca_ref, sa_ref, cb_ref, sb_ref,
                g_cq_ref, w_uq_ref, g_ckv_ref, w_ukv_ref,
                qdt_ref, kd_ref, vdt_ref, qmt_ref, km_ref, vmt_ref):
    h = _modulated(x_ref[0], mod_ref[0], g_ref[...], 0, 1).astype(_BF)
    y = _dot(h, w_in_ref[...])
    ca, sa, cb, sb = ca_ref[...], sa_ref[...], cb_ref[...], sb_ref[...]
    qscale = A_HEAD_DIM ** -0.5 * LOG2E
    for hd in range(A_HEADS):
        lo = hd * LANES
        qdt_ref[0, hd] = _t_bf(_rope(y[:, lo:lo + LANES], ca, sa, A_HEAD_DIM // 2) * qscale)
        kd_ref[0, hd] = _rope(y[:, 512 + lo:512 + lo + LANES], ca, sa, A_HEAD_DIM // 2).astype(_BF)
        vdt_ref[0, hd] = _t_bf(y[:, 1024 + lo:1024 + lo + LANES])

    cq = (_rms(y[:, 1536:1792]) * g_cq_ref[...]).astype(_BF)
    ckv = (_rms(y[:, 1792:1920]) * g_ckv_ref[...]).astype(_BF)
    kr = _rope(y[:, 1920:2048], cb, sb, B_ROPE_DIM // 2)
    qm = _dot(cq, w_uq_ref[...])
    kv = _dot(ckv, w_ukv_ref[...])
    mscale = (B_NOPE_DIM + B_ROPE_DIM) ** -0.5 * LOG2E
    for hd in range(B_HEADS):
        lo = hd * LANES
        qmt_ref[0, hd] = _t_bf(_rope(qm[:, lo:lo + LANES], cb, sb, B_ROPE_DIM // 2) * mscale)
        km_ref[0, hd] = (kv[:, lo:lo + LANES] + kr).astype(_BF)
    for pr in range(B_HEADS // 2):
        lo = B_HEADS * LANES + pr * LANES
        vmt_ref[0, pr] = _t_bf(kv[:, lo:lo + LANES])


def _head_out(n_heads, transposed):
    if transposed:
        return (jax.ShapeDtypeStruct((BATCH, n_heads, LANES, S_ALL), _BF),
                pl.BlockSpec((1, n_heads, LANES, ROW_TILE), lambda b, t: (b, 0, 0, t)))
    return (jax.ShapeDtypeStruct((BATCH, n_heads, S_ALL, LANES), _BF),
            pl.BlockSpec((1, n_heads, ROW_TILE, LANES), lambda b, t: (b, 0, t, 0)))


def _row_spec(width):
    return pl.BlockSpec((1, ROW_TILE, width), lambda b, t: (b, t, 0))


def _mod_spec():
    return pl.BlockSpec((1, 6, D_MODEL), lambda b, t: (_mod_row(b, t), 0, 0))


def _table_spec():
    return pl.BlockSpec((ROW_TILE, LANES), lambda b, t: (t, 0))


def _project0(xs, mods, g_attn, w_in, tabs, g_cq, w_uq, g_ckv, w_ukv):
    outs = [_head_out(4, True), _head_out(4, False), _head_out(4, True),
            _head_out(8, True), _head_out(8, False), _head_out(4, True)]
    return pl.pallas_call(
        _proj0_body,
        grid=(BATCH, N_TILES),
        in_specs=[_row_spec(D_MODEL), _mod_spec(), _const_spec((1, D_MODEL)),
                  _const_spec(w_in.shape),
                  _table_spec(), _table_spec(), _table_spec(), _table_spec(),
                  _const_spec((1, B_Q_RANK)), _const_spec(w_uq.shape),
                  _const_spec((1, B_KV_RANK)), _const_spec(w_ukv.shape)],
        out_specs=[o[1] for o in outs],
        out_shape=[o[0] for o in outs],
        compiler_params=pltpu.CompilerParams(vmem_limit_bytes=VMEM_LIMIT),
        name="project0",
    )(xs, mods, g_attn.reshape(1, -1), w_in, *tabs,
      g_cq.reshape(1, -1), w_uq, g_ckv.reshape(1, -1), w_ukv)


def _proj1_body(x_ref, mod_ref, g_ref, w_in_ref, ca_ref, sa_ref, bd_ref,
                g_qc_ref, g_kc_ref,
                qct_ref, kc_ref, vct_ref, qnt_ref, kn_ref, vnt_ref):
    h = _modulated(x_ref[0], mod_ref[0], g_ref[...], 0, 1).astype(_BF)
    y = _dot(h, w_in_ref[...])
    ca, sa = ca_ref[...], sa_ref[...]
    bd = bd_ref[...]
    qscale = C_HEAD_DIM ** -0.5 * LOG2E

    qc = y[:, 0:512]
    qc = qc * lax.rsqrt(_group_mean_sq(qc, bd) + EPS) * g_qc_ref[...]
    kc = y[:, 512:640]
    kc = kc * lax.rsqrt(_group_mean_sq(kc, bd[:LANES, :LANES]) + EPS) * g_kc_ref[...]
    kc_ref[0, 0] = _rope(kc, ca, sa, C_HEAD_DIM // 2).astype(_BF)
    vct_ref[0, 0] = _t_bf(y[:, 640:768])
    nscale = D_HEAD_DIM ** -0.5 * LOG2E
    for pr in range(4):
        lo = pr * LANES
        qct_ref[0, pr] = _t_bf(_rope(qc[:, lo:lo + LANES], ca, sa, C_HEAD_DIM // 2) * qscale)
        qnt_ref[0, pr] = _t_bf(y[:, 768 + lo:768 + lo + LANES] * nscale)
        kn_ref[0, pr] = y[:, 1280 + lo:1280 + lo + LANES].astype(_BF)
        vnt_ref[0, pr] = _t_bf(y[:, 1792 + lo:1792 + lo + LANES])


def _project1(xs, mods, g_attn, w_in, tabs, bd, g_qc, g_kc):
    outs = [_head_out(4, True), _head_out(1, False), _head_out(1, True),
            _head_out(4, True), _head_out(4, False), _head_out(4, True)]
    return pl.pallas_call(
        _proj1_body,
        grid=(BATCH, N_TILES),
        in_specs=[_row_spec(D_MODEL), _mod_spec(), _const_spec((1, D_MODEL)),
                  _const_spec(w_in.shape), _table_spec(), _table_spec(),
                  _const_spec(bd.shape),
                  _const_spec((1, 512)), _const_spec((1, LANES))],
        out_specs=[o[1] for o in outs],
        out_shape=[o[0] for o in outs],
        compiler_params=pltpu.CompilerParams(vmem_limit_bytes=VMEM_LIMIT),
        name="project1",
    )(xs, mods, g_attn.reshape(1, -1), w_in, tabs[0], tabs[1], bd,
      jnp.tile(g_qc, 8).reshape(1, -1), jnp.tile(g_kc, 2).reshape(1, -1))


def _split_q_cols(qt):
    z = jnp.zeros((HALF, qt.shape[1]), qt.dtype)
    return jnp.concatenate([jnp.concatenate([qt[:HALF], z], axis=0),
                            jnp.concatenate([z, qt[HALF:]], axis=0)], axis=1)


def _pipelined_tiles(load_q, score_chunk, value_chunk, finalize, n_chunks, bufs):
    def step(t_next, nxt, t_cur, cur):
        mx = None
        if t_next is not None:
            qb = load_q(t_next)
        if t_cur is not None:
            m_cur = cur[1][0:1, :]
            ls = jnp.zeros((1, Q2), _F32)
            acc = jnp.zeros((LANES, Q2), _F32)
        for c in range(n_chunks):
            rows = slice(c * KEY_CHUNK, (c + 1) * KEY_CHUNK)
            if t_next is not None:
                s = score_chunk(t_next, qb, c)
                nxt[0][rows, :] = s
                cm = jnp.max(s, axis=0, keepdims=True)
                mx = cm if mx is None else jnp.maximum(mx, cm)
            if t_cur is not None:
                e = jnp.exp2(cur[0][rows, :] - m_cur)
                ls = ls + jnp.sum(e, axis=0, keepdims=True)
                acc = acc + value_chunk(t_cur, c, e.astype(_BF))
        if t_next is not None:
            nxt[1][...] = jnp.broadcast_to(mx, nxt[1].shape)
        if t_cur is not None:
            finalize(t_cur, acc, ls)

    step(0, bufs[0], None, None)

    def pair(j, carry):
        step(2 * j + 1, bufs[1], 2 * j, bufs[0])
        step(jnp.minimum(2 * j + 2, N_LAT_TILES - 1), bufs[0], 2 * j + 1, bufs[1])
        return carry

    lax.fori_loop(0, N_LAT_TILES // 2, pair, 0)


def _dense_body(*refs, n_q, diff, ctx_tile, o_off):
    q_refs, k_refs, vt_ref = refs[:n_q], refs[n_q:2 * n_q], refs[2 * n_q]
    rest = refs[2 * n_q + 1:]
    if diff:
        lam_ref, g_ref, o_ref, s0, s1, m0, m1 = rest
        lv = lam_ref[...]
        lam = (jnp.exp(jnp.sum(lv[0:1] * lv[1:2], axis=-1, keepdims=True))
               - jnp.exp(jnp.sum(lv[2:3] * lv[3:4], axis=-1, keepdims=True)) + LAMBDA_INIT)
    else:
        o_ref, s0, s1, m0, m1 = rest

    def q_blocks(col0):
        if n_q == 1:
            return [_split_q_cols(q_refs[0][0, 0, :, pl.ds(col0, ROW_TILE)])]
        return [r[0, 0, :, pl.ds(col0, ROW_TILE)] for r in q_refs]

    def scores(qb, row0, rows):
        parts = [_dot(k_refs[i][0, 0, pl.ds(row0, rows), :], qb[i]) for i in range(n_q)]
        return parts[0] if n_q == 1 else jnp.concatenate(parts, axis=1)

    def finalize_rows(row0, acc, ls):
        ot = acc / ls
        if diff:
            o = (ot[:, :ROW_TILE] - lam * ot[:, ROW_TILE:]).T
            o = _rms(o) * g_ref[...] * (1.0 - LAMBDA_INIT)
        else:
            o = jnp.concatenate([ot[:HALF, :ROW_TILE], ot[HALF:, ROW_TILE:]], axis=0).T
        o_ref[0, pl.ds(row0, ROW_TILE), :] = o.astype(_BF)

    def load_q(t):
        return q_blocks(pl.multiple_of(CTX_LEN + t * ROW_TILE, ROW_TILE))

    def score_chunk(t, qb, c):
        return scores(qb, c * KEY_CHUNK, KEY_CHUNK)

    def value_chunk(t, c, e):
        return _dot(vt_ref[0, 0, :, c * KEY_CHUNK:(c + 1) * KEY_CHUNK], e)

    def finalize(t, acc, ls):
        finalize_rows(pl.multiple_of(o_off + t * ROW_TILE, ROW_TILE), acc, ls)

    _pipelined_tiles(load_q, score_chunk, value_chunk, finalize, N_KEY_CHUNKS, ((s0, m0), (s1, m1)))

    if ctx_tile:
        s = scores(q_blocks(0), 0, CTX_LEN)
        e = jnp.exp2(s - jnp.max(s, axis=0, keepdims=True))
        finalize_rows(0, _dot(vt_ref[0, 0, :, 0:CTX_LEN], e.astype(_BF)),
                      jnp.sum(e, axis=0, keepdims=True))


def _na_body(qt_ref, k_ref, vt_ref, bias_ref, o_ref, s0, s1, m0, m1):
    def key_row0(t, c):
        first = jnp.clip(t - 1, 0, N_LAT_TILES - 3)
        return 0 if c == 0 else pl.multiple_of(CTX_LEN + (first + c - 1) * KEY_CHUNK, KEY_CHUNK)

    def load_q(t):
        return _split_q_cols(qt_ref[0, 0, :, pl.ds(pl.multiple_of(CTX_LEN + t * ROW_TILE, ROW_TILE), ROW_TILE)])

    def score_chunk(t, qb, c):
        s = _dot(k_ref[0, 0, pl.ds(key_row0(t, c), KEY_CHUNK), :], qb)
        if c > 0:
            kind = jnp.where(t == 0, 0, jnp.where(t == N_LAT_TILES - 1, 2, 1))
            s = s + bias_ref[0, kind, (c - 1) * KEY_CHUNK:c * KEY_CHUNK, :]
        return s

    def value_chunk(t, c, e):
        return _dot(vt_ref[0, 0, :, pl.ds(key_row0(t, c), KEY_CHUNK)], e)

    def finalize(t, acc, ls):
        ot = acc / ls
        o = jnp.concatenate([ot[:HALF, :ROW_TILE], ot[HALF:, ROW_TILE:]], axis=0).T
        o_ref[0, pl.ds(pl.multiple_of(t * ROW_TILE, ROW_TILE), ROW_TILE), :] = o.astype(_BF)

    _pipelined_tiles(load_q, score_chunk, value_chunk, finalize, 1 + NA_KEYS // KEY_CHUNK,
                     ((s0, m0), (s1, m1)))


def _attn_call(body, name, in_specs, arrays, out_rows, n_keys):
    return pl.pallas_call(
        body,
        grid=(BATCH, 4),
        in_specs=in_specs,
        out_specs=pl.BlockSpec((1, out_rows, LANES), lambda b, g: (b, 0, g)),
        out_shape=jax.ShapeDtypeStruct((BATCH, out_rows, 4 * LANES), _BF),
        scratch_shapes=[pltpu.VMEM((n_keys, Q2), _F32), pltpu.VMEM((n_keys, Q2), _F32),
                        pltpu.VMEM((8, Q2), _F32), pltpu.VMEM((8, Q2), _F32)],
        compiler_params=pltpu.CompilerParams(vmem_limit_bytes=VMEM_LIMIT),
        name=name,
    )(*arrays)


def _t_spec(f):
    return pl.BlockSpec((1, 1, LANES, S_ALL), f)


def _n_spec(f):
    return pl.BlockSpec((1, 1, S_ALL, LANES), f)


def _diff_attention(qdt, kd, vdt, lamv, g_subln):
    head = lambda b, g: (b, g, 0, 0)
    body = functools.partial(_dense_body, n_q=1, diff=True, ctx_tile=True, o_off=CTX_LEN)
    return _attn_call(body, "diff_attention",
                      [_t_spec(head), _n_spec(head), _t_spec(head),
                       pl.BlockSpec((4, A_HEAD_DIM), lambda b, g: (0, 0)),
                       pl.BlockSpec((1, LANES), lambda b, g: (0, 0))],
                      (qdt, kd, vdt, lamv, g_subln.reshape(1, -1)), S_ALL, S_ALL)


def _mla_attention(qmt, km, vmt):
    ha = lambda b, g: (b, 2 * g, 0, 0)
    hb = lambda b, g: (b, 2 * g + 1, 0, 0)
    body = functools.partial(_dense_body, n_q=2, diff=False, ctx_tile=True, o_off=CTX_LEN)
    return _attn_call(body, "mla_attention",
                      [_t_spec(ha), _t_spec(hb), _n_spec(ha), _n_spec(hb),
                       _t_spec(lambda b, g: (b, g, 0, 0))],
                      (qmt, qmt, km, km, vmt), S_ALL, S_ALL)


def _gqa_attention(qct, kc, vct):
    shared = lambda b, g: (b, 0, 0, 0)
    body = functools.partial(_dense_body, n_q=1, diff=False, ctx_tile=False, o_off=0)
    return _attn_call(body, "gqa_attention",
                      [_t_spec(lambda b, g: (b, g, 0, 0)), _n_spec(shared), _t_spec(shared)],
                      (qct, kc, vct), SEQ, S_ALL)


def _na_attention(qnt, kn, vnt, bias_t):
    head = lambda b, g: (b, g, 0, 0)
    return _attn_call(_na_body, "na_attention",
                      [_t_spec(head), _n_spec(head), _t_spec(head),
                       pl.BlockSpec((1, 3, NA_KEYS, Q2), lambda b, g: (g, 0, 0, 0))],
                      (qnt, kn, vnt, bias_t), SEQ, CTX_LEN + NA_KEYS)


def _merge_ffn_body(x_ref, oa_ref, ob_ref, mod_ref, woa_ref, wob_ref, g_ref,
                    w1_ref, w3_ref, w2_ref, gf_ref, o_ref, *, final):
    m = mod_ref[0]
    attn = _dot(oa_ref[0], woa_ref[...]) + _dot(ob_ref[0], wob_ref[...])
    x1 = x_ref[0] + m[2:3] * attn
    h = _modulated(x1, m, g_ref[...], 3, 4).astype(_BF)
    a = _dot(h, w1_ref[...])
    b = _dot(h, w3_ref[...])
    u = (a * (1.0 / (1.0 + jnp.exp(-a))) * b).astype(_BF)
    x2 = x1 + m[5:6] * _dot(u, w2_ref[...])
    if final:
        x2 = _rms(x2) * gf_ref[...]
    o_ref[0] = x2


def _merge_ffn(xs, oa, ob, mods, woa, wob, g_ffn, w1, w3, w2, g_final, *, latent_only):
    off = 1 if latent_only else 0
    tiles = N_LAT_TILES if latent_only else N_TILES
    row = lambda w, o: pl.BlockSpec((1, ROW_TILE, w), lambda b, t: (b, t + o, 0))
    mod = pl.BlockSpec((1, 6, D_MODEL), lambda b, t: (_mod_row(b, t + off), 0, 0))
    return pl.pallas_call(
        functools.partial(_merge_ffn_body, final=latent_only),
        grid=(BATCH, tiles),
        in_specs=[row(D_MODEL, off), row(512, 0), row(512, 0), mod,
                  _const_spec(woa.shape), _const_spec(wob.shape), _const_spec((1, D_MODEL)),
                  _const_spec(w1.shape), _const_spec(w3.shape), _const_spec(w2.shape),
                  _const_spec((1, D_MODEL))],
        out_specs=row(D_MODEL, 0),
        out_shape=jax.ShapeDtypeStruct((BATCH, tiles * ROW_TILE, D_MODEL), _F32),
        compiler_params=pltpu.CompilerParams(vmem_limit_bytes=VMEM_LIMIT),
        name="merge_ffn_final" if latent_only else "merge_ffn",
    )(xs, oa, ob, mods, woa, wob, g_ffn.reshape(1, -1), w1, w3, w2, g_final.reshape(1, -1))


def _rope_tables():
    t = np.arange(SEQ)
    rows = jnp.asarray(t // GRID_W, _F32)
    cols = jnp.asarray(t % GRID_W, _F32)

    def cs(rot_dim):
        axis_dim = rot_dim // 2
        inv = ROPE_THETA ** (-jnp.arange(0, axis_dim, 2, dtype=_F32) / axis_dim)
        ang = jnp.concatenate([rows[:, None] * inv, cols[:, None] * inv], axis=-1)
        return jnp.cos(ang), jnp.sin(ang)

    def with_ctx(tab, fill):
        return jnp.concatenate([jnp.full((CTX_LEN, LANES), fill, _F32), tab], axis=0)

    cos, sin = cs(A_HEAD_DIM)
    ca = with_ctx(jnp.tile(cos, (1, 4)), 1.0)
    sa = with_ctx(jnp.tile(jnp.concatenate([-sin, sin], axis=-1), (1, 2)), 0.0)
    cos, sin = cs(B_ROPE_DIM)
    ones = jnp.ones((SEQ, 64), _F32)
    zeros = jnp.zeros((SEQ, 64), _F32)
    cb = with_ctx(jnp.concatenate([ones, cos, cos, ones[:, :32]], axis=-1), 1.0)
    sb = with_ctx(jnp.concatenate([zeros, -sin, sin, zeros[:, :32]], axis=-1), 0.0)
    return ca, sa, cb, sb


def _na_bias_table(rpb):
    rows_n = SEQ // GRID_W
    span = 2 * GRID_W - 1
    left = GRID_W - NA_WIN_W
    p = jnp.pad(rpb * LOG2E, ((0, 0), (0, 0), (left, span - left - (2 * NA_WIN_W - 1))))
    flat = jnp.tile(p[:, :, ::-1], (1, 1, GRID_W + 1))[..., :GRID_W * (span + 1)]
    toe = flat.reshape(D_HEADS, 2 * NA_WIN_H - 1, GRID_W, span + 1)[:, :, ::-1, :GRID_W]
    kc = np.arange(GRID_W)[:, None]
    qc = np.arange(GRID_W)[None, :]
    cs = np.clip(qc - NA_WIN_W // 2, 0, GRID_W - NA_WIN_W)
    toe = jnp.where((kc >= cs) & (kc < cs + NA_WIN_W), toe, NEG_BIG)
    masked = jnp.full((D_HEADS, GRID_W, GRID_W), NEG_BIG, _F32)

    kinds = [(0, 0), (ROWS_PER_TILE, 0), (rows_n - ROWS_PER_TILE, rows_n - NA_KEY_ROWS)]
    tiles = []
    for q_row0, k_row0 in kinds:
        band = []
        for kr in range(NA_KEY_ROWS):
            krow = k_row0 + kr
            blocks = []
            for qr in range(ROWS_PER_TILE):
                r = q_row0 + qr
                rs = min(max(r - NA_WIN_H // 2, 0), rows_n - NA_WIN_H)
                inside = rs <= krow < rs + NA_WIN_H
                blocks.append(toe[:, krow - r + NA_WIN_H - 1] if inside else masked)
            band.append(jnp.concatenate(blocks, axis=-1))
        tiles.append(jnp.concatenate(band, axis=-2))
    bias = jnp.stack(tiles, axis=1)
    bias = bias.reshape(4, 2, 3, NA_KEYS, ROW_TILE).transpose(0, 2, 3, 1, 4)
    return bias.reshape(4, 3, NA_KEYS, Q2)


def _pad_cols(w, groups, width, total, offset=0):
    k = w.shape[0]
    w = w.reshape(k, groups, width)
    w = jnp.pad(w, ((0, 0), (0, 0), (offset, total - width - offset)))
    return w.reshape(k, groups * total)


def kernel(x, c, ctx, c_ctx, l0_w_mod, l0_b_mod, l0_g_attn, l0_w_in, l0_lam_q1, l0_lam_k1, l0_lam_q2, l0_lam_k2, l0_g_subln, l0_g_cq, l0_w_uq, l0_g_ckv, l0_w_ukv, l0_w_out, l0_g_ffn, l0_w1, l0_w3, l0_w2, l1_w_mod, l1_b_mod, l1_g_attn, l1_w_in, l1_g_qc, l1_g_kc, l1_rpb, l1_w_out, l1_g_ffn, l1_w1, l1_w3, l1_w2, g_final):
    bf = lambda w: w.astype(_BF)
    xs = jnp.concatenate([ctx, x], axis=1)
    cond = jnp.concatenate([c, c_ctx[None], jnp.zeros((3, D_MODEL), _F32)], axis=0)
    tabs = _rope_tables()

    mods0 = _ada_params(cond, l0_w_mod, l0_b_mod)
    w_in0 = bf(jnp.concatenate([l0_w_in[:, :1920],
                                _pad_cols(l0_w_in[:, 1920:], 1, B_ROPE_DIM, LANES, B_NOPE_DIM)], axis=1))
    w_uq = bf(_pad_cols(l0_w_uq, B_HEADS, B_NOPE_DIM + B_ROPE_DIM, LANES))
    ukv = l0_w_ukv.reshape(B_KV_RANK, B_HEADS, B_NOPE_DIM + B_V_DIM)
    w_ukv = bf(jnp.concatenate([_pad_cols(ukv[:, :, :B_NOPE_DIM].reshape(B_KV_RANK, -1), B_HEADS, B_NOPE_DIM, LANES),
                                ukv[:, :, B_NOPE_DIM:].reshape(B_KV_RANK, -1)], axis=1))
    qdt, kd, vdt, qmt, km, vmt = _project0(xs, mods0, l0_g_attn, w_in0, tabs,
                                           l0_g_cq, w_uq, l0_g_ckv, w_ukv)
    lamv = jnp.stack([l0_lam_q1, l0_lam_k1, l0_lam_q2, l0_lam_k2])
    o_a = _diff_attention(qdt, kd, vdt, lamv, l0_g_subln)
    o_b = _mla_attention(qmt, km, vmt)
    w_out0 = bf(l0_w_out)
    xs = _merge_ffn(xs, o_a, o_b, mods0, w_out0[:512], w_out0[512:], l0_g_ffn,
                    bf(l0_w1), bf(l0_w3), bf(l0_w2), g_final, latent_only=False)

    mods1 = _ada_params(cond, l1_w_mod, l1_b_mod)
    wq = l1_w_in[:, :512].reshape(D_MODEL, 2, 4, C_HEAD_DIM).transpose(0, 2, 1, 3).reshape(D_MODEL, 512)
    w_in1 = bf(jnp.concatenate([wq, l1_w_in[:, 512:]], axis=1))
    grp = np.arange(512) // 64
    bd = jnp.asarray((grp[:, None] == grp[None, :]) / 64.0, _BF)
    qct, kc, vct, qnt, kn, vnt = _project1(xs, mods1, l1_g_attn, w_in1, tabs, bd, l1_g_qc, l1_g_kc)
    o_c = _gqa_attention(qct, kc, vct)
    o_d = _na_attention(qnt, kn, vnt, _na_bias_table(l1_rpb))
    w_out1 = bf(l1_w_out)
    woc = w_out1[:512].reshape(2, 4, C_HEAD_DIM, D_MODEL).transpose(1, 0, 2, 3).reshape(512, D_MODEL)
    return _merge_ffn(xs, o_c, o_d, mods1, woc, w_out1[512:], l1_g_ffn,
                      bf(l1_w1), bf(l1_w3), bf(l1_w2), g_final, latent_only=True)
```

```python
import functools
import math

import numpy as np
import jax
import jax.numpy as jnp
from jax import lax
from jax.experimental import pallas as pl
from jax.experimental.pallas import tpu as pltpu

D_MODEL = 1024
BATCH = 4
SEQ = 4096
GRID_W = 64
CTX_LEN = 256
ROPE_THETA = 10000.0
EPS = 1e-6
A_HEADS = 4
A_HEAD_DIM = 64
B_HEADS = 8
B_Q_RANK = 256
B_KV_RANK = 128
B_NOPE_DIM = 64
B_ROPE_DIM = 32
B_V_DIM = 64
C_HEADS = 8
C_KV_HEADS = 2
C_HEAD_DIM = 64
D_HEADS = 8
D_HEAD_DIM = 64
NA_WIN_H = 8
NA_WIN_W = 16
FFN_HIDDEN = 2816
LAMBDA_INIT = 0.8 - 0.6 * math.exp(-0.3 * 0)

S_ALL = CTX_LEN + SEQ
LANES = 128
HALF = LANES // 2
ROW_TILE = 256
N_TILES = S_ALL // ROW_TILE
N_LAT_TILES = SEQ // ROW_TILE
Q2 = 2 * ROW_TILE
KEY_CHUNK = 256
N_KEY_CHUNKS = S_ALL // KEY_CHUNK
ROWS_PER_TILE = ROW_TILE // GRID_W
NA_KEY_ROWS = 12
NA_KEYS = NA_KEY_ROWS * GRID_W
LOG2E = math.log2(math.e)
NEG_BIG = -1e30
VMEM_LIMIT = 52 * 1024 * 1024

assert ROW_TILE == CTX_LEN and S_ALL % ROW_TILE == 0 and N_LAT_TILES % 2 == 0

_BF = jnp.bfloat16
_F32 = jnp.float32


def _dot(a, b):
    return jnp.dot(a, b, preferred_element_type=_F32)


def _rms(x):
    return x * lax.rsqrt(jnp.mean(x * x, axis=-1, keepdims=True) + EPS)


def _lane(shape):
    return lax.broadcasted_iota(jnp.int32, shape, len(shape) - 1)


def _swap_halves(x, half):
    n = x.shape[-1]
    up = pltpu.roll(x, n - half, axis=1)
    down = pltpu.roll(x, half, axis=1)
    return jnp.where((_lane(x.shape) % (2 * half)) < half, up, down)


def _rope(x, c, s, half):
    return x * c + _swap_halves(x, half) * s


def _group_mean_sq(x, bd):
    x2 = x * x
    hi = x2.astype(_BF)
    lo = (x2 - hi.astype(_F32)).astype(_BF)
    return _dot(hi, bd) + _dot(lo, bd)


def _t_bf(x):
    return x.T.astype(_BF)


def _ada_body(cond_ref, w_ref, b_ref, o_ref):
    c = cond_ref[...]
    a = (c * (1.0 / (1.0 + jnp.exp(-c)))).astype(_BF)
    o_ref[...] = _dot(a, w_ref[...].astype(_BF)) + b_ref[...]


def _ada_params(cond, w_mod, b_mod):
    n = w_mod.shape[1]
    tn = n // 4
    out = pl.pallas_call(
        _ada_body,
        grid=(n // tn,),
        in_specs=[pl.BlockSpec((8, D_MODEL), lambda j: (0, 0)),
                  pl.BlockSpec((D_MODEL, tn), lambda j: (0, j)),
                  pl.BlockSpec((1, tn), lambda j: (0, j))],
        out_specs=pl.BlockSpec((8, tn), lambda j: (0, j)),
        out_shape=jax.ShapeDtypeStruct((8, n), _F32),
        compiler_params=pltpu.CompilerParams(vmem_limit_bytes=VMEM_LIMIT),
        name="ada_params",
    )(cond, w_mod, b_mod.reshape(1, n))
    return out.reshape(8, 6, D_MODEL)


def _mod_row(b, t):
    return jnp.where(t == 0, BATCH, b)


def _const_spec(shape):
    nd = len(shape)
    return pl.BlockSpec(shape, lambda *_: (0,) * nd, pipeline_mode=pl.Buffered(1))


def _modulated(x, m, g, shift_row, scale_row):
    h = _rms(x) * g
    return h * (1.0 + m[scale_row:scale_row + 1]) + m[shift_row:shift_row + 1]


def _stream_tile(x_ref, ctx_ref):
    return jnp.where(pl.program_id(1) == 0, ctx_ref[0], x_ref[0])


def _stream_specs():
    return [pl.BlockSpec((1, ROW_TILE, D_MODEL), lambda b, t: (b, jnp.maximum(t - 1, 0), 0)),
            pl.BlockSpec((1, CTX_LEN, D_MODEL), lambda b, t: (b, 0, 0))]


def _proj0_body(x_ref, ctx_ref, mod_ref, g_ref, w_in_ref, ca_ref, sa_ref, cb_ref, sb_ref,
                g_cq_ref, w_uq_ref, g_ckv_ref, w_ukv_ref,
                qdt_ref, kd_ref, vdt_ref, qmt_ref, km_ref, vmt_ref):
    h = _modulated(_stream_tile(x_ref, ctx_ref), mod_ref[0], g_ref[...], 0, 1).astype(_BF)
    y = _dot(h, w_in_ref[...])
    ca, sa, cb, sb = ca_ref[...], sa_ref[...], cb_ref[...], sb_ref[...]
    qscale = A_HEAD_DIM ** -0.5 * LOG2E
    for hd in range(A_HEADS):
        lo = hd * LANES
        qdt_ref[0, hd] = _t_bf(_rope(y[:, lo:lo + LANES], ca, sa, A_HEAD_DIM // 2) * qscale)
        kd_ref[0, hd] = _rope(y[:, 512 + lo:512 + lo + LANES], ca, sa, A_HEAD_DIM // 2).astype(_BF)
        vdt_ref[0, hd] = _t_bf(y[:, 1024 + lo:1024 + lo + LANES])

    cq = (_rms(y[:, 1536:1792]) * g_cq_ref[...]).astype(_BF)
    ckv = (_rms(y[:, 1792:1920]) * g_ckv_ref[...]).astype(_BF)
    kr = _rope(y[:, 1920:2048], cb, sb, B_ROPE_DIM // 2)
    qm = _dot(cq, w_uq_ref[...])
    kv = _dot(ckv, w_ukv_ref[...])
    mscale = (B_NOPE_DIM + B_ROPE_DIM) ** -0.5 * LOG2E
    for hd in range(B_HEADS):
        lo = hd * LANES
        qmt_ref[0, hd] = _t_bf(_rope(qm[:, lo:lo + LANES], cb, sb, B_ROPE_DIM // 2) * mscale)
        km_ref[0, hd] = (kv[:, lo:lo + LANES] + kr).astype(_BF)
    for pr in range(B_HEADS // 2):
        lo = B_HEADS * LANES + pr * LANES
        vmt_ref[0, pr] = _t_bf(kv[:, lo:lo + LANES])


def _head_out(n_heads, transposed):
    if transposed:
        return (jax.ShapeDtypeStruct((BATCH, n_heads, LANES, S_ALL), _BF),
                pl.BlockSpec((1, n_heads, LANES, ROW_TILE), lambda b, t: (b, 0, 0, t)))
    return (jax.ShapeDtypeStruct((BATCH, n_heads, S_ALL, LANES), _BF),
            pl.BlockSpec((1, n_heads, ROW_TILE, LANES), lambda b, t: (b, 0, t, 0)))


def _row_spec(width):
    return pl.BlockSpec((1, ROW_TILE, width), lambda b, t: (b, t, 0))


def _mod_spec():
    return pl.BlockSpec((1, 6, D_MODEL), lambda b, t: (_mod_row(b, t), 0, 0))


def _table_spec():
    return pl.BlockSpec((ROW_TILE, LANES), lambda b, t: (t, 0))


def _project0(x, ctx, mods, g_attn, w_in, tabs, g_cq, w_uq, g_ckv, w_ukv):
    outs = [_head_out(4, True), _head_out(4, False), _head_out(4, True),
            _head_out(8, True), _head_out(8, False), _head_out(4, True)]
    return pl.pallas_call(
        _proj0_body,
        grid=(BATCH, N_TILES),
        in_specs=_stream_specs() + [_mod_spec(), _const_spec((1, D_MODEL)),
                  _const_spec(w_in.shape),
                  _table_spec(), _table_spec(), _table_spec(), _table_spec(),
                  _const_spec((1, B_Q_RANK)), _const_spec(w_uq.shape),
                  _const_spec((1, B_KV_RANK)), _const_spec(w_ukv.shape)],
        out_specs=[o[1] for o in outs],
        out_shape=[o[0] for o in outs],
        compiler_params=pltpu.CompilerParams(vmem_limit_bytes=VMEM_LIMIT),
        name="project0",
    )(x, ctx, mods, g_attn.reshape(1, -1), w_in, *tabs,
      g_cq.reshape(1, -1), w_uq, g_ckv.reshape(1, -1), w_ukv)


def _proj1_body(x_ref, mod_ref, g_ref, w_in_ref, ca_ref, sa_ref, bd_ref,
                g_qc_ref, g_kc_ref,
                qct_ref, kc_ref, vct_ref, qnt_ref, kn_ref, vnt_ref):
    h = _modulated(x_ref[0], mod_ref[0], g_ref[...], 0, 1).astype(_BF)
    y = _dot(h, w_in_ref[...])
    ca, sa = ca_ref[...], sa_ref[...]
    bd = bd_ref[...]
    qscale = C_HEAD_DIM ** -0.5 * LOG2E

    qc = y[:, 0:512]
    qc = qc * lax.rsqrt(_group_mean_sq(qc, bd) + EPS) * g_qc_ref[...]
    kc = y[:, 512:640]
    kc = kc * lax.rsqrt(_group_mean_sq(kc, bd[:LANES, :LANES]) + EPS) * g_kc_ref[...]
    kc_ref[0, 0] = _rope(kc, ca, sa, C_HEAD_DIM // 2).astype(_BF)
    vct_ref[0, 0] = _t_bf(y[:, 640:768])
    nscale = D_HEAD_DIM ** -0.5 * LOG2E
    for pr in range(4):
        lo = pr * LANES
        qct_ref[0, pr] = _t_bf(_rope(qc[:, lo:lo + LANES], ca, sa, C_HEAD_DIM // 2) * qscale)
        qnt_ref[0, pr] = _t_bf(y[:, 768 + lo:768 + lo + LANES] * nscale)
        kn_ref[0, pr] = y[:, 1280 + lo:1280 + lo + LANES].astype(_BF)
        vnt_ref[0, pr] = _t_bf(y[:, 1792 + lo:1792 + lo + LANES])


def _project1(xs, mods, g_attn, w_in, tabs, bd, g_qc, g_kc):
    outs = [_head_out(4, True), _head_out(1, False), _head_out(1, True),
            _head_out(4, True), _head_out(4, False), _head_out(4, True)]
    return pl.pallas_call(
        _proj1_body,
        grid=(BATCH, N_TILES),
        in_specs=[_row_spec(D_MODEL), _mod_spec(), _const_spec((1, D_MODEL)),
                  _const_spec(w_in.shape), _table_spec(), _table_spec(),
                  _const_spec(bd.shape),
                  _const_spec((1, 512)), _const_spec((1, LANES))],
        out_specs=[o[1] for o in outs],
        out_shape=[o[0] for o in outs],
        compiler_params=pltpu.CompilerParams(vmem_limit_bytes=VMEM_LIMIT),
        name="project1",
    )(xs, mods, g_attn.reshape(1, -1), w_in, tabs[0], tabs[1], bd,
      jnp.tile(g_qc, 8).reshape(1, -1), jnp.tile(g_kc, 2).reshape(1, -1))


def _split_q_cols(qt):
    z = jnp.zeros((HALF, qt.shape[1]), qt.dtype)
    return jnp.concatenate([jnp.concatenate([qt[:HALF], z], axis=0),
                            jnp.concatenate([z, qt[HALF:]], axis=0)], axis=1)


def _pipelined_tiles(load_q, score_chunk, value_chunk, finalize, n_chunks, bufs):
    def step(t_next, nxt, t_cur, cur):
        mx = None
        if t_next is not None:
            qb = load_q(t_next)
        if t_cur is not None:
            m_cur = cur[1][0:1, :]
            ls = jnp.zeros((1, Q2), _F32)
            acc = jnp.zeros((LANES, Q2), _F32)
        for c in range(n_chunks):
            rows = slice(c * KEY_CHUNK, (c + 1) * KEY_CHUNK)
            if t_next is not None:
                s = score_chunk(t_next, qb, c)
                nxt[0][rows, :] = s
                cm = jnp.max(s, axis=0, keepdims=True)
                mx = cm if mx is None else jnp.maximum(mx, cm)
            if t_cur is not None:
                e = jnp.exp2(cur[0][rows, :] - m_cur)
                ls = ls + jnp.sum(e, axis=0, keepdims=True)
                acc = acc + value_chunk(t_cur, c, e.astype(_BF))
        if t_next is not None:
            nxt[1][...] = jnp.broadcast_to(mx, nxt[1].shape)
        return (acc, ls) if t_cur is not None else None

    last = N_LAT_TILES - 1
    step(0, bufs[0], None, None)

    def pair(j, carry):
        finalize(jnp.maximum(2 * j - 1, 0), *carry)
        finalize(2 * j, *step(2 * j + 1, bufs[1], 2 * j, bufs[0]))
        return step(2 * j + 2, bufs[0], 2 * j + 1, bufs[1])

    carry = lax.fori_loop(0, N_LAT_TILES // 2 - 1, pair,
                          (jnp.zeros((LANES, Q2), _F32), jnp.ones((1, Q2), _F32)))
    finalize(last - 2, *carry)
    finalize(last - 1, *step(last, bufs[1], last - 1, bufs[0]))
    finalize(last, *step(None, None, last, bufs[1]))


def _dense_body(*refs, n_q, diff, ctx_tile, o_off):
    q_refs, k_refs, vt_ref = refs[:n_q], refs[n_q:2 * n_q], refs[2 * n_q]
    rest = refs[2 * n_q + 1:]
    if diff:
        lam_ref, g_ref, o_ref, s0, s1, m0, m1 = rest
        lv = lam_ref[...]
        lam = (jnp.exp(jnp.sum(lv[0:1] * lv[1:2], axis=-1, keepdims=True))
               - jnp.exp(jnp.sum(lv[2:3] * lv[3:4], axis=-1, keepdims=True)) + LAMBDA_INIT)
    else:
        o_ref, s0, s1, m0, m1 = rest

    def q_blocks(col0):
        if n_q == 1:
            return [_split_q_cols(q_refs[0][0, 0, :, pl.ds(col0, ROW_TILE)])]
        return [r[0, 0, :, pl.ds(col0, ROW_TILE)] for r in q_refs]

    def scores(qb, row0, rows):
        parts = [_dot(k_refs[i][0, 0, pl.ds(row0, rows), :], qb[i]) for i in range(n_q)]
        return parts[0] if n_q == 1 else jnp.concatenate(parts, axis=1)

    def finalize_rows(row0, acc, ls):
        ot = acc / ls
        if diff:
            o = (ot[:, :ROW_TILE] - lam * ot[:, ROW_TILE:]).T
            o = _rms(o) * g_ref[...] * (1.0 - LAMBDA_INIT)
        else:
            o = jnp.concatenate([ot[:HALF, :ROW_TILE], ot[HALF:, ROW_TILE:]], axis=0).T
        o_ref[0, pl.ds(row0, ROW_TILE), :] = o.astype(_BF)

    def load_q(t):
        return q_blocks(pl.multiple_of(CTX_LEN + t * ROW_TILE, ROW_TILE))

    def score_chunk(t, qb, c):
        return scores(qb, c * KEY_CHUNK, KEY_CHUNK)

    def value_chunk(t, c, e):
        return _dot(vt_ref[0, 0, :, c * KEY_CHUNK:(c + 1) * KEY_CHUNK], e)

    def finalize(t, acc, ls):
        finalize_rows(pl.multiple_of(o_off + t * ROW_TILE, ROW_TILE), acc, ls)

    _pipelined_tiles(load_q, score_chunk, value_chunk, finalize, N_KEY_CHUNKS, ((s0, m0), (s1, m1)))

    if ctx_tile:
        s = scores(q_blocks(0), 0, CTX_LEN)
        e = jnp.exp2(s - jnp.max(s, axis=0, keepdims=True))
        finalize_rows(0, _dot(vt_ref[0, 0, :, 0:CTX_LEN], e.astype(_BF)),
                      jnp.sum(e, axis=0, keepdims=True))


def _na_body(qt_ref, k_ref, vt_ref, bias_ref, o_ref, s0, s1, m0, m1):
    def key_row0(t, c):
        first = jnp.clip(t - 1, 0, N_LAT_TILES - 3)
        return 0 if c == 0 else pl.multiple_of(CTX_LEN + (first + c - 1) * KEY_CHUNK, KEY_CHUNK)

    def load_q(t):
        return _split_q_cols(qt_ref[0, 0, :, pl.ds(pl.multiple_of(CTX_LEN + t * ROW_TILE, ROW_TILE), ROW_TILE)])

    def score_chunk(t, qb, c):
        s = _dot(k_ref[0, 0, pl.ds(key_row0(t, c), KEY_CHUNK), :], qb)
        if c > 0:
            kind = jnp.where(t == 0, 0, jnp.where(t == N_LAT_TILES - 1, 2, 1))
            s = s + bias_ref[0, kind, (c - 1) * KEY_CHUNK:c * KEY_CHUNK, :]
        return s

    def value_chunk(t, c, e):
        return _dot(vt_ref[0, 0, :, pl.ds(key_row0(t, c), KEY_CHUNK)], e)

    def finalize(t, acc, ls):
        ot = acc / ls
        o = jnp.concatenate([ot[:HALF, :ROW_TILE], ot[HALF:, ROW_TILE:]], axis=0).T
        o_ref[0, pl.ds(pl.multiple_of(t * ROW_TILE, ROW_TILE), ROW_TILE), :] = o.astype(_BF)

    _pipelined_tiles(load_q, score_chunk, value_chunk, finalize, 1 + NA_KEYS // KEY_CHUNK,
                     ((s0, m0), (s1, m1)))


def _attn_call(body, name, in_specs, arrays, out_rows, n_keys):
    return pl.pallas_call(
        body,
        grid=(BATCH, 4),
        in_specs=in_specs,
        out_specs=pl.BlockSpec((1, out_rows, LANES), lambda b, g: (b, 0, g)),
        out_shape=jax.ShapeDtypeStruct((BATCH, out_rows, 4 * LANES), _BF),
        scratch_shapes=[pltpu.VMEM((n_keys, Q2), _F32), pltpu.VMEM((n_keys, Q2), _F32),
                        pltpu.VMEM((8, Q2), _F32), pltpu.VMEM((8, Q2), _F32)],
        compiler_params=pltpu.CompilerParams(vmem_limit_bytes=VMEM_LIMIT),
        name=name,
    )(*arrays)


def _t_spec(f):
    return pl.BlockSpec((1, 1, LANES, S_ALL), f)


def _n_spec(f):
    return pl.BlockSpec((1, 1, S_ALL, LANES), f)


def _diff_attention(qdt, kd, vdt, lamv, g_subln):
    head = lambda b, g: (b, g, 0, 0)
    body = functools.partial(_dense_body, n_q=1, diff=True, ctx_tile=True, o_off=CTX_LEN)
    return _attn_call(body, "diff_attention",
                      [_t_spec(head), _n_spec(head), _t_spec(head),
                       pl.BlockSpec((4, A_HEAD_DIM), lambda b, g: (0, 0)),
                       pl.BlockSpec((1, LANES), lambda b, g: (0, 0))],
                      (qdt, kd, vdt, lamv, g_subln.reshape(1, -1)), S_ALL, S_ALL)


def _mla_attention(qmt, km, vmt):
    ha = lambda b, g: (b, 2 * g, 0, 0)
    hb = lambda b, g: (b, 2 * g + 1, 0, 0)
    body = functools.partial(_dense_body, n_q=2, diff=False, ctx_tile=True, o_off=CTX_LEN)
    return _attn_call(body, "mla_attention",
                      [_t_spec(ha), _t_spec(hb), _n_spec(ha), _n_spec(hb),
                       _t_spec(lambda b, g: (b, g, 0, 0))],
                      (qmt, qmt, km, km, vmt), S_ALL, S_ALL)


def _gqa_attention(qct, kc, vct):
    shared = lambda b, g: (b, 0, 0, 0)
    body = functools.partial(_dense_body, n_q=1, diff=False, ctx_tile=False, o_off=0)
    return _attn_call(body, "gqa_attention",
                      [_t_spec(lambda b, g: (b, g, 0, 0)), _n_spec(shared), _t_spec(shared)],
                      (qct, kc, vct), SEQ, S_ALL)


def _na_attention(qnt, kn, vnt, bias_t):
    head = lambda b, g: (b, g, 0, 0)
    return _attn_call(_na_body, "na_attention",
                      [_t_spec(head), _n_spec(head), _t_spec(head),
                       pl.BlockSpec((1, 3, NA_KEYS, Q2), lambda b, g: (g, 0, 0, 0))],
                      (qnt, kn, vnt, bias_t), SEQ, CTX_LEN + NA_KEYS)


def _merge_ffn_body(*refs, final):
    if final:
        x_ref = refs[0]
        x = x_ref[0]
    else:
        x = _stream_tile(refs[0], refs[1])
    (oa_ref, ob_ref, mod_ref, woa_ref, wob_ref, g_ref,
     w1_ref, w3_ref, w2_ref, gf_ref, o_ref) = refs[1 if final else 2:]
    m = mod_ref[0]
    attn = _dot(oa_ref[0], woa_ref[...]) + _dot(ob_ref[0], wob_ref[...])
    x1 = x + m[2:3] * attn
    h = _modulated(x1, m, g_ref[...], 3, 4).astype(_BF)
    a = _dot(h, w1_ref[...])
    b = _dot(h, w3_ref[...])
    u = (a * (1.0 / (1.0 + jnp.exp(-a))) * b).astype(_BF)
    x2 = x1 + m[5:6] * _dot(u, w2_ref[...])
    if final:
        x2 = _rms(x2) * gf_ref[...]
    o_ref[0] = x2


def _merge_ffn(streams, oa, ob, mods, woa, wob, g_ffn, w1, w3, w2, g_final, *, latent_only):
    off = 1 if latent_only else 0
    tiles = N_LAT_TILES if latent_only else N_TILES
    row = lambda w, o: pl.BlockSpec((1, ROW_TILE, w), lambda b, t: (b, t + o, 0))
    mod = pl.BlockSpec((1, 6, D_MODEL), lambda b, t: (_mod_row(b, t + off), 0, 0))
    x_specs = [row(D_MODEL, off)] if latent_only else _stream_specs()
    return pl.pallas_call(
        functools.partial(_merge_ffn_body, final=latent_only),
        grid=(BATCH, tiles),
        in_specs=x_specs + [row(512, 0), row(512, 0), mod,
                  _const_spec(woa.shape), _const_spec(wob.shape), _const_spec((1, D_MODEL)),
                  _const_spec(w1.shape), _const_spec(w3.shape), _const_spec(w2.shape),
                  _const_spec((1, D_MODEL))],
        out_specs=row(D_MODEL, 0),
        out_shape=jax.ShapeDtypeStruct((BATCH, tiles * ROW_TILE, D_MODEL), _F32),
        compiler_params=pltpu.CompilerParams(vmem_limit_bytes=VMEM_LIMIT),
        name="merge_ffn_final" if latent_only else "merge_ffn",
    )(*streams, oa, ob, mods, woa, wob, g_ffn.reshape(1, -1), w1, w3, w2, g_final.reshape(1, -1))


def _rope_tables():
    t = np.arange(SEQ)
    rows = jnp.asarray(t // GRID_W, _F32)
    cols = jnp.asarray(t % GRID_W, _F32)

    def cs(rot_dim):
        axis_dim = rot_dim // 2
        inv = ROPE_THETA ** (-jnp.arange(0, axis_dim, 2, dtype=_F32) / axis_dim)
        ang = jnp.concatenate([rows[:, None] * inv, cols[:, None] * inv], axis=-1)
        return jnp.cos(ang), jnp.sin(ang)

    def with_ctx(tab, fill):
        return jnp.concatenate([jnp.full((CTX_LEN, LANES), fill, _F32), tab], axis=0)

    cos, sin = cs(A_HEAD_DIM)
    ca = with_ctx(jnp.tile(cos, (1, 4)), 1.0)
    sa = with_ctx(jnp.tile(jnp.concatenate([-sin, sin], axis=-1), (1, 2)), 0.0)
    cos, sin = cs(B_ROPE_DIM)
    ones = jnp.ones((SEQ, 64), _F32)
    zeros = jnp.zeros((SEQ, 64), _F32)
    cb = with_ctx(jnp.concatenate([ones, cos, cos, ones[:, :32]], axis=-1), 1.0)
    sb = with_ctx(jnp.concatenate([zeros, -sin, sin, zeros[:, :32]], axis=-1), 0.0)
    return ca, sa, cb, sb


def _na_bias_table(rpb):
    rows_n = SEQ // GRID_W
    span = 2 * GRID_W - 1
    left = GRID_W - NA_WIN_W
    p = jnp.pad(rpb * LOG2E, ((0, 0), (0, 0), (left, span - left - (2 * NA_WIN_W - 1))))
    flat = jnp.tile(p[:, :, ::-1], (1, 1, GRID_W + 1))[..., :GRID_W * (span + 1)]
    toe = flat.reshape(D_HEADS, 2 * NA_WIN_H - 1, GRID_W, span + 1)[:, :, ::-1, :GRID_W]
    kc = np.arange(GRID_W)[:, None]
    qc = np.arange(GRID_W)[None, :]
    cs = np.clip(qc - NA_WIN_W // 2, 0, GRID_W - NA_WIN_W)
    toe = jnp.where((kc >= cs) & (kc < cs + NA_WIN_W), toe, NEG_BIG)
    n_dr = 2 * NA_WIN_H - 1
    edge = ROWS_PER_TILE - 1
    n_quads = n_dr + edge

    def quads(t):
        fill = jnp.full((D_HEADS, edge, GRID_W, GRID_W), NEG_BIG, _F32)
        tp = jnp.concatenate([fill, t, fill], axis=1).reshape(4, 2, n_dr + 2 * edge, GRID_W, GRID_W)
        return jnp.concatenate([tp[:, w, edge - qr:edge - qr + n_quads]
                                for w in (0, 1) for qr in range(ROWS_PER_TILE)], axis=-1)

    dr = np.arange(n_dr)[None, :, None, None]
    lo = NA_WIN_H - 1 - NA_WIN_H // 2
    q_int = quads(jnp.where((dr >= lo) & (dr < lo + NA_WIN_H), toe, NEG_BIG))
    q_all = quads(toe)

    def tile(q, q_row0, k_row0):
        rs = [min(max(q_row0 + qr - NA_WIN_H // 2, 0), rows_n - NA_WIN_H) for qr in range(ROWS_PER_TILE)]
        parts = []
        for kr in range(NA_KEY_ROWS):
            inside = [r0 <= k_row0 + kr < r0 + NA_WIN_H for r0 in rs]
            d = k_row0 + kr - q_row0 + NA_WIN_H - 1
            assert q is q_int or all(inside) or not any(inside)
            if q is q_int or all(inside):
                parts.append(q[:, d])
            else:
                parts.append(jnp.full((4, GRID_W, Q2), NEG_BIG, _F32))
        return jnp.concatenate(parts, axis=1)

    return jnp.stack([tile(q_all, 0, 0),
                      tile(q_int, ROWS_PER_TILE, 0),
                      tile(q_all, rows_n - ROWS_PER_TILE, rows_n - NA_KEY_ROWS)], axis=1)


def _pad_cols(w, groups, width, total, offset=0):
    k = w.shape[0]
    w = w.reshape(k, groups, width)
    w = jnp.pad(w, ((0, 0), (0, 0), (offset, total - width - offset)))
    return w.reshape(k, groups * total)


def kernel(x, c, ctx, c_ctx, l0_w_mod, l0_b_mod, l0_g_attn, l0_w_in, l0_lam_q1, l0_lam_k1, l0_lam_q2, l0_lam_k2, l0_g_subln, l0_g_cq, l0_w_uq, l0_g_ckv, l0_w_ukv, l0_w_out, l0_g_ffn, l0_w1, l0_w3, l0_w2, l1_w_mod, l1_b_mod, l1_g_attn, l1_w_in, l1_g_qc, l1_g_kc, l1_rpb, l1_w_out, l1_g_ffn, l1_w1, l1_w3, l1_w2, g_final):
    bf = lambda w: w.astype(_BF)
    cond = jnp.concatenate([c, c_ctx[None], jnp.zeros((3, D_MODEL), _F32)], axis=0)
    tabs = _rope_tables()

    mods0 = _ada_params(cond, l0_w_mod, l0_b_mod)
    w_in0 = bf(jnp.concatenate([l0_w_in[:, :1920],
                                _pad_cols(l0_w_in[:, 1920:], 1, B_ROPE_DIM, LANES, B_NOPE_DIM)], axis=1))
    w_uq = bf(_pad_cols(l0_w_uq, B_HEADS, B_NOPE_DIM + B_ROPE_DIM, LANES))
    ukv = l0_w_ukv.reshape(B_KV_RANK, B_HEADS, B_NOPE_DIM + B_V_DIM)
    w_ukv = bf(jnp.concatenate([_pad_cols(ukv[:, :, :B_NOPE_DIM].reshape(B_KV_RANK, -1), B_HEADS, B_NOPE_DIM, LANES),
                                ukv[:, :, B_NOPE_DIM:].reshape(B_KV_RANK, -1)], axis=1))
    qdt, kd, vdt, qmt, km, vmt = _project0(x, ctx, mods0, l0_g_attn, w_in0, tabs,
                                           l0_g_cq, w_uq, l0_g_ckv, w_ukv)
    lamv = jnp.stack([l0_lam_q1, l0_lam_k1, l0_lam_q2, l0_lam_k2])
    o_a = _diff_attention(qdt, kd, vdt, lamv, l0_g_subln)
    o_b = _mla_attention(qmt, km, vmt)
    w_out0 = bf(l0_w_out)
    xs = _merge_ffn((x, ctx), o_a, o_b, mods0, w_out0[:512], w_out0[512:], l0_g_ffn,
                    bf(l0_w1), bf(l0_w3), bf(l0_w2), g_final, latent_only=False)

    mods1 = _ada_params(cond, l1_w_mod, l1_b_mod)
    wq = l1_w_in[:, :512].reshape(D_MODEL, 2, 4, C_HEAD_DIM).transpose(0, 2, 1, 3).reshape(D_MODEL, 512)
    w_in1 = bf(jnp.concatenate([wq, l1_w_in[:, 512:]], axis=1))
    grp = np.arange(512) // 64
    bd = jnp.asarray((grp[:, None] == grp[None, :]) / 64.0, _BF)
    qct, kc, vct, qnt, kn, vnt = _project1(xs, mods1, l1_g_attn, w_in1, tabs, bd, l1_g_qc, l1_g_kc)
    o_c = _gqa_attention(qct, kc, vct)
    o_d = _na_attention(qnt, kn, vnt, _na_bias_table(l1_rpb))
    w_out1 = bf(l1_w_out)
    woc = w_out1[:512].reshape(2, 4, C_HEAD_DIM, D_MODEL).transpose(1, 0, 2, 3).reshape(512, D_MODEL)
    return _merge_ffn((xs,), o_c, o_d, mods1, woc, w_out1[512:], l1_g_ffn,
                      bf(l1_w1), bf(l1_w3), bf(l1_w2), g_final, latent_only=True)
```

```python
import functools
import math

import numpy as np
import jax
import jax.numpy as jnp
from jax import lax
from jax.experimental import pallas as pl
from jax.experimental.pallas import tpu as pltpu

D_MODEL = 1024
BATCH = 4
SEQ = 4096
GRID_W = 64
CTX_LEN = 256
ROPE_THETA = 10000.0
EPS = 1e-6
A_HEADS = 4
A_HEAD_DIM = 64
B_HEADS = 8
B_Q_RANK = 256
B_KV_RANK = 128
B_NOPE_DIM = 64
B_ROPE_DIM = 32
B_V_DIM = 64
C_HEADS = 8
C_KV_HEADS = 2
C_HEAD_DIM = 64
D_HEADS = 8
D_HEAD_DIM = 64
NA_WIN_H = 8
NA_WIN_W = 16
FFN_HIDDEN = 2816
LAMBDA_INIT = 0.8 - 0.6 * math.exp(-0.3 * 0)

S_ALL = CTX_LEN + SEQ
LANES = 128
HALF = LANES // 2
ROW_TILE = 256
NB = 2
N_TILES = S_ALL // ROW_TILE
N_LAT_TILES = SEQ // ROW_TILE
Q2 = 2 * ROW_TILE
KEY_CHUNK = 256
N_KEY_CHUNKS = S_ALL // KEY_CHUNK
ROWS_PER_TILE = ROW_TILE // GRID_W
NA_KEY_ROWS = 12
NA_KEYS = NA_KEY_ROWS * GRID_W
LOG2E = math.log2(math.e)
NEG_BIG = -1e30
VMEM_LIMIT = 52 * 1024 * 1024

assert ROW_TILE == CTX_LEN and S_ALL % ROW_TILE == 0 and N_LAT_TILES % 2 == 0

_BF = jnp.bfloat16
_F32 = jnp.float32


def _dot(a, b):
    return jnp.dot(a, b, preferred_element_type=_F32)


def _rms(x):
    return x * lax.rsqrt(jnp.mean(x * x, axis=-1, keepdims=True) + EPS)


def _lane(shape):
    return lax.broadcasted_iota(jnp.int32, shape, len(shape) - 1)


def _swap_halves(x, half):
    n = x.shape[-1]
    up = pltpu.roll(x, n - half, axis=1)
    down = pltpu.roll(x, half, axis=1)
    return jnp.where((_lane(x.shape) % (2 * half)) < half, up, down)


def _rope(x, c, s, half):
    return x * c + _swap_halves(x, half) * s


def _group_mean_sq(x, bd):
    x2 = x * x
    hi = x2.astype(_BF)
    lo = (x2 - hi.astype(_F32)).astype(_BF)
    return _dot(hi, bd) + _dot(lo, bd)


def _t_bf(x):
    return x.T.astype(_BF)


def _ada_body(cond_ref, w_ref, b_ref, o_ref):
    c = cond_ref[...]
    a = (c * (1.0 / (1.0 + jnp.exp(-c)))).astype(_BF)
    o_ref[...] = _dot(a, w_ref[...].astype(_BF)) + b_ref[...]


def _ada_params(cond, w_mod, b_mod):
    n = w_mod.shape[1]
    tn = n // 4
    out = pl.pallas_call(
        _ada_body,
        grid=(n // tn,),
        in_specs=[pl.BlockSpec((8, D_MODEL), lambda j: (0, 0)),
                  pl.BlockSpec((D_MODEL, tn), lambda j: (0, j)),
                  pl.BlockSpec((1, tn), lambda j: (0, j))],
        out_specs=pl.BlockSpec((8, tn), lambda j: (0, j)),
        out_shape=jax.ShapeDtypeStruct((8, n), _F32),
        compiler_params=pltpu.CompilerParams(vmem_limit_bytes=VMEM_LIMIT),
        name="ada_params",
    )(cond, w_mod, b_mod.reshape(1, n))
    return out.reshape(8, 6, D_MODEL)


def _mod_row(b, t):
    return jnp.where(t == 0, BATCH // NB, b)


def _const_spec(shape):
    nd = len(shape)
    return pl.BlockSpec(shape, lambda *_: (0,) * nd, pipeline_mode=pl.Buffered(1))


def _modulated(x, m, g, shift_row, scale_row):
    h = _rms(x) * g
    return h * (1.0 + m[scale_row:scale_row + 1]) + m[shift_row:shift_row + 1]


def _modulated_rows(xs, m, g, shift_row, scale_row):
    return jnp.concatenate([_modulated(xs[i], m[i], g, shift_row, scale_row) for i in range(NB)],
                           axis=0).astype(_BF)


def _stream_tile(x_ref, ctx_ref):
    return jnp.where(pl.program_id(1) == 0, ctx_ref[...], x_ref[...])


def _stream_specs():
    return [pl.BlockSpec((NB, ROW_TILE, D_MODEL), lambda b, t: (b, jnp.maximum(t - 1, 0), 0)),
            pl.BlockSpec((NB, CTX_LEN, D_MODEL), lambda b, t: (b, 0, 0))]


def _proj0_body(x_ref, ctx_ref, mod_ref, g_ref, w_in_ref, ca_ref, sa_ref, cb_ref, sb_ref,
                g_cq_ref, w_uq_ref, g_ckv_ref, w_ukv_ref,
                qdt_ref, kd_ref, vdt_ref, qmt_ref, km_ref, vmt_ref):
    h = _modulated_rows(_stream_tile(x_ref, ctx_ref), mod_ref[...], g_ref[...], 0, 1)
    y = _dot(h, w_in_ref[...])
    cq = (_rms(y[:, 1536:1792]) * g_cq_ref[...]).astype(_BF)
    ckv = (_rms(y[:, 1792:1920]) * g_ckv_ref[...]).astype(_BF)
    qm = _dot(cq, w_uq_ref[...])
    kv = _dot(ckv, w_ukv_ref[...])
    ca, sa, cb, sb = ca_ref[...], sa_ref[...], cb_ref[...], sb_ref[...]
    qscale = A_HEAD_DIM ** -0.5 * LOG2E
    mscale = (B_NOPE_DIM + B_ROPE_DIM) ** -0.5 * LOG2E
    for i in range(NB):
        rows = slice(i * ROW_TILE, (i + 1) * ROW_TILE)
        for hd in range(A_HEADS):
            lo = hd * LANES
            qdt_ref[i, hd] = _t_bf(_rope(y[rows, lo:lo + LANES], ca, sa, A_HEAD_DIM // 2) * qscale)
            kd_ref[i, hd] = _rope(y[rows, 512 + lo:512 + lo + LANES], ca, sa, A_HEAD_DIM // 2).astype(_BF)
            vdt_ref[i, hd] = _t_bf(y[rows, 1024 + lo:1024 + lo + LANES])
        kr = _rope(y[rows, 1920:2048], cb, sb, B_ROPE_DIM // 2)
        for hd in range(B_HEADS):
            lo = hd * LANES
            qmt_ref[i, hd] = _t_bf(_rope(qm[rows, lo:lo + LANES], cb, sb, B_ROPE_DIM // 2) * mscale)
            km_ref[i, hd] = (kv[rows, lo:lo + LANES] + kr).astype(_BF)
        for pr in range(B_HEADS // 2):
            lo = B_HEADS * LANES + pr * LANES
            vmt_ref[i, pr] = _t_bf(kv[rows, lo:lo + LANES])


def _head_out(n_heads, transposed):
    if transposed:
        return (jax.ShapeDtypeStruct((BATCH, n_heads, LANES, S_ALL), _BF),
                pl.BlockSpec((NB, n_heads, LANES, ROW_TILE), lambda b, t: (b, 0, 0, t)))
    return (jax.ShapeDtypeStruct((BATCH, n_heads, S_ALL, LANES), _BF),
            pl.BlockSpec((NB, n_heads, ROW_TILE, LANES), lambda b, t: (b, 0, t, 0)))


def _row_spec(width):
    return pl.BlockSpec((NB, ROW_TILE, width), lambda b, t: (b, t, 0))


def _mod_spec():
    return pl.BlockSpec((NB, 6, D_MODEL), lambda b, t: (_mod_row(b, t), 0, 0))


def _table_spec():
    return pl.BlockSpec((ROW_TILE, LANES), lambda b, t: (t, 0))


def _project0(x, ctx, mods, g_attn, w_in, tabs, g_cq, w_uq, g_ckv, w_ukv):
    outs = [_head_out(4, True), _head_out(4, False), _head_out(4, True),
            _head_out(8, True), _head_out(8, False), _head_out(4, True)]
    return pl.pallas_call(
        _proj0_body,
        grid=(BATCH // NB, N_TILES),
        in_specs=_stream_specs() + [_mod_spec(), _const_spec((1, D_MODEL)),
                  _const_spec(w_in.shape),
                  _table_spec(), _table_spec(), _table_spec(), _table_spec(),
                  _const_spec((1, B_Q_RANK)), _const_spec(w_uq.shape),
                  _const_spec((1, B_KV_RANK)), _const_spec(w_ukv.shape)],
        out_specs=[o[1] for o in outs],
        out_shape=[o[0] for o in outs],
        compiler_params=pltpu.CompilerParams(vmem_limit_bytes=VMEM_LIMIT),
        name="project0",
    )(x, ctx, mods, g_attn.reshape(1, -1), w_in, *tabs,
      g_cq.reshape(1, -1), w_uq, g_ckv.reshape(1, -1), w_ukv)


def _proj1_body(x_ref, mod_ref, g_ref, w_in_ref, ca_ref, sa_ref, bd_ref,
                g_qc_ref, g_kc_ref,
                qct_ref, kc_ref, vct_ref, qnt_ref, kn_ref, vnt_ref):
    h = _modulated_rows(x_ref[...], mod_ref[...], g_ref[...], 0, 1)
    y = _dot(h, w_in_ref[...])
    ca, sa = ca_ref[...], sa_ref[...]
    bd = bd_ref[...]
    qscale = C_HEAD_DIM ** -0.5 * LOG2E
    nscale = D_HEAD_DIM ** -0.5 * LOG2E

    qc_all = y[:, 0:512]
    qc_all = qc_all * lax.rsqrt(_group_mean_sq(qc_all, bd) + EPS) * g_qc_ref[...]
    kc_all = y[:, 512:640]
    kc_all = kc_all * lax.rsqrt(_group_mean_sq(kc_all, bd[:LANES, :LANES]) + EPS) * g_kc_ref[...]
    for i in range(NB):
        rows = slice(i * ROW_TILE, (i + 1) * ROW_TILE)
        kc_ref[i, 0] = _rope(kc_all[rows], ca, sa, C_HEAD_DIM // 2).astype(_BF)
        vct_ref[i, 0] = _t_bf(y[rows, 640:768])
        for pr in range(4):
            lo = pr * LANES
            qct_ref[i, pr] = _t_bf(_rope(qc_all[rows, lo:lo + LANES], ca, sa, C_HEAD_DIM // 2) * qscale)
            qnt_ref[i, pr] = _t_bf(y[rows, 768 + lo:768 + lo + LANES] * nscale)
            kn_ref[i, pr] = y[rows, 1280 + lo:1280 + lo + LANES].astype(_BF)
            vnt_ref[i, pr] = _t_bf(y[rows, 1792 + lo:1792 + lo + LANES])


def _project1(xs, mods, g_attn, w_in, tabs, bd, g_qc, g_kc):
    outs = [_head_out(4, True), _head_out(1, False), _head_out(1, True),
            _head_out(4, True), _head_out(4, False), _head_out(4, True)]
    return pl.pallas_call(
        _proj1_body,
        grid=(BATCH // NB, N_TILES),
        in_specs=[_row_spec(D_MODEL), _mod_spec(), _const_spec((1, D_MODEL)),
                  _const_spec(w_in.shape), _table_spec(), _table_spec(),
                  _const_spec(bd.shape),
                  _const_spec((1, 512)), _const_spec((1, LANES))],
        out_specs=[o[1] for o in outs],
        out_shape=[o[0] for o in outs],
        compiler_params=pltpu.CompilerParams(vmem_limit_bytes=VMEM_LIMIT),
        name="project1",
    )(xs, mods, g_attn.reshape(1, -1), w_in, tabs[0], tabs[1], bd,
      jnp.tile(g_qc, 8).reshape(1, -1), jnp.tile(g_kc, 2).reshape(1, -1))


def _split_q_cols(qt):
    z = jnp.zeros((HALF, qt.shape[1]), qt.dtype)
    return jnp.concatenate([jnp.concatenate([qt[:HALF], z], axis=0),
                            jnp.concatenate([z, qt[HALF:]], axis=0)], axis=1)


def _pipelined_tiles(load_q, score_chunk, value_chunk, finalize, n_chunks, bufs, score_group=1):
    def step(t_next, nxt, t_cur, cur):
        mx = None
        if t_next is not None:
            qb = load_q(t_next)
        if t_cur is not None:
            m_cur = cur[1][0:1, :]
            ls = jnp.zeros((1, Q2), _F32)
            acc = jnp.zeros((LANES, Q2), _F32)
        for c in range(n_chunks):
            rows = slice(c * KEY_CHUNK, (c + 1) * KEY_CHUNK)
            if t_next is not None and c % score_group == 0:
                n = min(score_group, n_chunks - c)
                s = score_chunk(t_next, qb, c, n)
                nxt[0][c * KEY_CHUNK:(c + n) * KEY_CHUNK, :] = s
                cm = jnp.max(s, axis=0, keepdims=True)
                mx = cm if mx is None else jnp.maximum(mx, cm)
            if t_cur is not None:
                e = jnp.exp2(cur[0][rows, :] - m_cur)
                ls = ls + jnp.sum(e, axis=0, keepdims=True)
                acc = acc + value_chunk(t_cur, c, e.astype(_BF))
        if t_next is not None:
            nxt[1][...] = jnp.broadcast_to(mx, nxt[1].shape)
        return (acc, ls) if t_cur is not None else None

    last = N_LAT_TILES - 1
    step(0, bufs[0], None, None)

    def pair(j, carry):
        finalize(jnp.maximum(2 * j - 1, 0), *carry)
        finalize(2 * j, *step(2 * j + 1, bufs[1], 2 * j, bufs[0]))
        return step(2 * j + 2, bufs[0], 2 * j + 1, bufs[1])

    carry = lax.fori_loop(0, N_LAT_TILES // 2 - 1, pair,
                          (jnp.zeros((LANES, Q2), _F32), jnp.ones((1, Q2), _F32)))
    finalize(last - 2, *carry)
    finalize(last - 1, *step(last, bufs[1], last - 1, bufs[0]))
    finalize(last, *step(None, None, last, bufs[1]))


def _dense_body(*refs, n_q, diff, ctx_tile, o_off):
    q_refs, k_refs, vt_ref = refs[:n_q], refs[n_q:2 * n_q], refs[2 * n_q]
    rest = refs[2 * n_q + 1:]
    if diff:
        lam_ref, g_ref, o_ref, s0, s1, m0, m1 = rest
        lv = lam_ref[...]
        lam = (jnp.exp(jnp.sum(lv[0:1] * lv[1:2], axis=-1, keepdims=True))
               - jnp.exp(jnp.sum(lv[2:3] * lv[3:4], axis=-1, keepdims=True)) + LAMBDA_INIT)
    else:
        o_ref, s0, s1, m0, m1 = rest

    def q_blocks(col0):
        if n_q == 1:
            return [_split_q_cols(q_refs[0][0, 0, :, pl.ds(col0, ROW_TILE)])]
        return [r[0, 0, :, pl.ds(col0, ROW_TILE)] for r in q_refs]

    def scores(qb, row0, rows):
        parts = [_dot(k_refs[i][0, 0, pl.ds(row0, rows), :], qb[i]) for i in range(n_q)]
        return parts[0] if n_q == 1 else jnp.concatenate(parts, axis=1)

    def finalize_rows(row0, acc, ls):
        ot = acc / ls
        if diff:
            o = (ot[:, :ROW_TILE] - lam * ot[:, ROW_TILE:]).T
            o = _rms(o) * g_ref[...] * (1.0 - LAMBDA_INIT)
        else:
            o = jnp.concatenate([ot[:HALF, :ROW_TILE], ot[HALF:, ROW_TILE:]], axis=0).T
        o_ref[0, pl.ds(row0, ROW_TILE), :] = o.astype(_BF)

    def load_q(t):
        return q_blocks(pl.multiple_of(CTX_LEN + t * ROW_TILE, ROW_TILE))

    def score_chunk(t, qb, c, n):
        return scores(qb, c * KEY_CHUNK, n * KEY_CHUNK)

    def value_chunk(t, c, e):
        return _dot(vt_ref[0, 0, :, c * KEY_CHUNK:(c + 1) * KEY_CHUNK], e)

    def finalize(t, acc, ls):
        finalize_rows(pl.multiple_of(o_off + t * ROW_TILE, ROW_TILE), acc, ls)

    _pipelined_tiles(load_q, score_chunk, value_chunk, finalize, N_KEY_CHUNKS, ((s0, m0), (s1, m1)))

    if ctx_tile:
        s = scores(q_blocks(0), 0, CTX_LEN)
        e = jnp.exp2(s - jnp.max(s, axis=0, keepdims=True))
        finalize_rows(0, _dot(vt_ref[0, 0, :, 0:CTX_LEN], e.astype(_BF)),
                      jnp.sum(e, axis=0, keepdims=True))


def _na_body(qt_ref, k_ref, vt_ref, bias_ref, o_ref, s0, s1, m0, m1):
    def key_row0(t, c):
        first = jnp.clip(t - 1, 0, N_LAT_TILES - 3)
        return 0 if c == 0 else pl.multiple_of(CTX_LEN + (first + c - 1) * KEY_CHUNK, KEY_CHUNK)

    def load_q(t):
        return _split_q_cols(qt_ref[0, 0, :, pl.ds(pl.multiple_of(CTX_LEN + t * ROW_TILE, ROW_TILE), ROW_TILE)])

    def score_chunk(t, qb, c, n):
        s = _dot(k_ref[0, 0, pl.ds(key_row0(t, c), KEY_CHUNK), :], qb)
        if c > 0:
            kind = jnp.where(t == 0, 0, jnp.where(t == N_LAT_TILES - 1, 2, 1))
            s = s + bias_ref[0, kind, (c - 1) * KEY_CHUNK:c * KEY_CHUNK, :]
        return s

    def value_chunk(t, c, e):
        return _dot(vt_ref[0, 0, :, pl.ds(key_row0(t, c), KEY_CHUNK)], e)

    def finalize(t, acc, ls):
        ot = acc / ls
        o = jnp.concatenate([ot[:HALF, :ROW_TILE], ot[HALF:, ROW_TILE:]], axis=0).T
        o_ref[0, pl.ds(pl.multiple_of(t * ROW_TILE, ROW_TILE), ROW_TILE), :] = o.astype(_BF)

    _pipelined_tiles(load_q, score_chunk, value_chunk, finalize, 1 + NA_KEYS // KEY_CHUNK,
                     ((s0, m0), (s1, m1)))


def _attn_call(body, name, in_specs, arrays, out_rows, n_keys):
    return pl.pallas_call(
        body,
        grid=(BATCH, 4),
        in_specs=in_specs,
        out_specs=pl.BlockSpec((1, out_rows, LANES), lambda b, g: (b, 0, g)),
        out_shape=jax.ShapeDtypeStruct((BATCH, out_rows, 4 * LANES), _BF),
        scratch_shapes=[pltpu.VMEM((n_keys, Q2), _F32), pltpu.VMEM((n_keys, Q2), _F32),
                        pltpu.VMEM((8, Q2), _F32), pltpu.VMEM((8, Q2), _F32)],
        compiler_params=pltpu.CompilerParams(vmem_limit_bytes=VMEM_LIMIT),
        name=name,
    )(*arrays)


def _t_spec(f):
    return pl.BlockSpec((1, 1, LANES, S_ALL), f)


def _n_spec(f):
    return pl.BlockSpec((1, 1, S_ALL, LANES), f)


def _diff_attention(qdt, kd, vdt, lamv, g_subln):
    head = lambda b, g: (b, g, 0, 0)
    body = functools.partial(_dense_body, n_q=1, diff=True, ctx_tile=True, o_off=CTX_LEN)
    return _attn_call(body, "diff_attention",
                      [_t_spec(head), _n_spec(head), _t_spec(head),
                       pl.BlockSpec((4, A_HEAD_DIM), lambda b, g: (0, 0)),
                       pl.BlockSpec((1, LANES), lambda b, g: (0, 0))],
                      (qdt, kd, vdt, lamv, g_subln.reshape(1, -1)), S_ALL, S_ALL)


def _mla_attention(qmt, km, vmt):
    ha = lambda b, g: (b, 2 * g, 0, 0)
    hb = lambda b, g: (b, 2 * g + 1, 0, 0)
    body = functools.partial(_dense_body, n_q=2, diff=False, ctx_tile=True, o_off=CTX_LEN)
    return _attn_call(body, "mla_attention",
                      [_t_spec(ha), _t_spec(hb), _n_spec(ha), _n_spec(hb),
                       _t_spec(lambda b, g: (b, g, 0, 0))],
                      (qmt, qmt, km, km, vmt), S_ALL, S_ALL)


def _gqa_attention(qct, kc, vct):
    shared = lambda b, g: (b, 0, 0, 0)
    body = functools.partial(_dense_body, n_q=1, diff=False, ctx_tile=False, o_off=0)
    return _attn_call(body, "gqa_attention",
                      [_t_spec(lambda b, g: (b, g, 0, 0)), _n_spec(shared), _t_spec(shared)],
                      (qct, kc, vct), SEQ, S_ALL)


def _na_attention(qnt, kn, vnt, bias_t):
    head = lambda b, g: (b, g, 0, 0)
    return _attn_call(_na_body, "na_attention",
                      [_t_spec(head), _n_spec(head), _t_spec(head),
                       pl.BlockSpec((1, 3, NA_KEYS, Q2), lambda b, g: (g, 0, 0, 0))],
                      (qnt, kn, vnt, bias_t), SEQ, CTX_LEN + NA_KEYS)


def _merge_ffn_body(*refs, final):
    x = refs[0][...] if final else _stream_tile(refs[0], refs[1])
    (oa_ref, ob_ref, mod_ref, woa_ref, wob_ref, g_ref,
     w1_ref, w3_ref, w2_ref, gf_ref, o_ref) = refs[1 if final else 2:]
    m = mod_ref[...]
    stack = lambda r: r[...].reshape(NB * ROW_TILE, r.shape[-1])
    rows = lambda v, i: v[i * ROW_TILE:(i + 1) * ROW_TILE]
    attn = _dot(stack(oa_ref), woa_ref[...]) + _dot(stack(ob_ref), wob_ref[...])
    x1 = [x[i] + m[i, 2:3] * rows(attn, i) for i in range(NB)]
    h = _modulated_rows(x1, m, g_ref[...], 3, 4)
    a = _dot(h, w1_ref[...])
    b = _dot(h, w3_ref[...])
    u = (a * (1.0 / (1.0 + jnp.exp(-a))) * b).astype(_BF)
    f = _dot(u, w2_ref[...])
    for i in range(NB):
        x2 = x1[i] + m[i, 5:6] * rows(f, i)
        if final:
            x2 = _rms(x2) * gf_ref[...]
        o_ref[i] = x2


def _merge_ffn(streams, oa, ob, mods, woa, wob, g_ffn, w1, w3, w2, g_final, *, latent_only):
    off = 1 if latent_only else 0
    tiles = N_LAT_TILES if latent_only else N_TILES
    row = lambda w, o: pl.BlockSpec((NB, ROW_TILE, w), lambda b, t: (b, t + o, 0))
    mod = pl.BlockSpec((NB, 6, D_MODEL), lambda b, t: (_mod_row(b, t + off), 0, 0))
    x_specs = [row(D_MODEL, off)] if latent_only else _stream_specs()
    return pl.pallas_call(
        functools.partial(_merge_ffn_body, final=latent_only),
        grid=(BATCH // NB, tiles),
        in_specs=x_specs + [row(512, 0), row(512, 0), mod,
                  _const_spec(woa.shape), _const_spec(wob.shape), _const_spec((1, D_MODEL)),
                  _const_spec(w1.shape), _const_spec(w3.shape), _const_spec(w2.shape),
                  _const_spec((1, D_MODEL))],
        out_specs=row(D_MODEL, 0),
        out_shape=jax.ShapeDtypeStruct((BATCH, tiles * ROW_TILE, D_MODEL), _F32),
        compiler_params=pltpu.CompilerParams(vmem_limit_bytes=VMEM_LIMIT),
        name="merge_ffn_final" if latent_only else "merge_ffn",
    )(*streams, oa, ob, mods, woa, wob, g_ffn.reshape(1, -1), w1, w3, w2, g_final.reshape(1, -1))


def _rope_tables():
    t = np.arange(SEQ)
    rows = jnp.asarray(t // GRID_W, _F32)
    cols = jnp.asarray(t % GRID_W, _F32)

    def cs(rot_dim):
        axis_dim = rot_dim // 2
        inv = ROPE_THETA ** (-jnp.arange(0, axis_dim, 2, dtype=_F32) / axis_dim)
        ang = jnp.concatenate([rows[:, None] * inv, cols[:, None] * inv], axis=-1)
        return jnp.cos(ang), jnp.sin(ang)

    def with_ctx(tab, fill):
        return jnp.concatenate([jnp.full((CTX_LEN, LANES), fill, _F32), tab], axis=0)

    cos, sin = cs(A_HEAD_DIM)
    ca = with_ctx(jnp.tile(cos, (1, 4)), 1.0)
    sa = with_ctx(jnp.tile(jnp.concatenate([-sin, sin], axis=-1), (1, 2)), 0.0)
    cos, sin = cs(B_ROPE_DIM)
    ones = jnp.ones((SEQ, 64), _F32)
    zeros = jnp.zeros((SEQ, 64), _F32)
    cb = with_ctx(jnp.concatenate([ones, cos, cos, ones[:, :32]], axis=-1), 1.0)
    sb = with_ctx(jnp.concatenate([zeros, -sin, sin, zeros[:, :32]], axis=-1), 0.0)
    return ca, sa, cb, sb


def _na_bias_table(rpb):
    rows_n = SEQ // GRID_W
    span = 2 * GRID_W - 1
    left = GRID_W - NA_WIN_W
    p = jnp.pad(rpb * LOG2E, ((0, 0), (0, 0), (left, span - left - (2 * NA_WIN_W - 1))))
    flat = jnp.tile(p[:, :, ::-1], (1, 1, GRID_W + 1))[..., :GRID_W * (span + 1)]
    toe = flat.reshape(D_HEADS, 2 * NA_WIN_H - 1, GRID_W, span + 1)[:, :, ::-1, :GRID_W]
    kc = np.arange(GRID_W)[:, None]
    qc = np.arange(GRID_W)[None, :]
    cs = np.clip(qc - NA_WIN_W // 2, 0, GRID_W - NA_WIN_W)
    toe = jnp.where((kc >= cs) & (kc < cs + NA_WIN_W), toe, NEG_BIG)
    n_dr = 2 * NA_WIN_H - 1
    edge = ROWS_PER_TILE - 1
    n_quads = n_dr + edge

    def quads(t):
        fill = jnp.full((D_HEADS, edge, GRID_W, GRID_W), NEG_BIG, _F32)
        tp = jnp.concatenate([fill, t, fill], axis=1).reshape(4, 2, n_dr + 2 * edge, GRID_W, GRID_W)
        return jnp.concatenate([tp[:, w, edge - qr:edge - qr + n_quads]
                                for w in (0, 1) for qr in range(ROWS_PER_TILE)], axis=-1)

    dr = np.arange(n_dr)[None, :, None, None]
    lo = NA_WIN_H - 1 - NA_WIN_H // 2
    q_int = quads(jnp.where((dr >= lo) & (dr < lo + NA_WIN_H), toe, NEG_BIG))
    q_all = quads(toe)

    def tile(q, q_row0, k_row0):
        rs = [min(max(q_row0 + qr - NA_WIN_H // 2, 0), rows_n - NA_WIN_H) for qr in range(ROWS_PER_TILE)]
        parts = []
        for kr in range(NA_KEY_ROWS):
            inside = [r0 <= k_row0 + kr < r0 + NA_WIN_H for r0 in rs]
            d = k_row0 + kr - q_row0 + NA_WIN_H - 1
            assert q is q_int or all(inside) or not any(inside)
            if q is q_int or all(inside):
                parts.append(q[:, d])
            else:
                parts.append(jnp.full((4, GRID_W, Q2), NEG_BIG, _F32))
        return jnp.concatenate(parts, axis=1)

    return jnp.stack([tile(q_all, 0, 0),
                      tile(q_int, ROWS_PER_TILE, 0),
                      tile(q_all, rows_n - ROWS_PER_TILE, rows_n - NA_KEY_ROWS)], axis=1)


def _pad_cols(w, groups, width, total, offset=0):
    k = w.shape[0]
    w = w.reshape(k, groups, width)
    w = jnp.pad(w, ((0, 0), (0, 0), (offset, total - width - offset)))
    return w.reshape(k, groups * total)


def kernel(x, c, ctx, c_ctx, l0_w_mod, l0_b_mod, l0_g_attn, l0_w_in, l0_lam_q1, l0_lam_k1, l0_lam_q2, l0_lam_k2, l0_g_subln, l0_g_cq, l0_w_uq, l0_g_ckv, l0_w_ukv, l0_w_out, l0_g_ffn, l0_w1, l0_w3, l0_w2, l1_w_mod, l1_b_mod, l1_g_attn, l1_w_in, l1_g_qc, l1_g_kc, l1_rpb, l1_w_out, l1_g_ffn, l1_w1, l1_w3, l1_w2, g_final):
    bf = lambda w: w.astype(_BF)
    cond = jnp.concatenate([c, jnp.tile(c_ctx[None], (NB, 1)),
                            jnp.zeros((8 - BATCH - NB, D_MODEL), _F32)], axis=0)
    tabs = _rope_tables()

    mods0 = _ada_params(cond, l0_w_mod, l0_b_mod)
    w_in0 = bf(jnp.concatenate([l0_w_in[:, :1920],
                                _pad_cols(l0_w_in[:, 1920:], 1, B_ROPE_DIM, LANES, B_NOPE_DIM)], axis=1))
    w_uq = bf(_pad_cols(l0_w_uq, B_HEADS, B_NOPE_DIM + B_ROPE_DIM, LANES))
    ukv = l0_w_ukv.reshape(B_KV_RANK, B_HEADS, B_NOPE_DIM + B_V_DIM)
    w_ukv = bf(jnp.concatenate([_pad_cols(ukv[:, :, :B_NOPE_DIM].reshape(B_KV_RANK, -1), B_HEADS, B_NOPE_DIM, LANES),
                                ukv[:, :, B_NOPE_DIM:].reshape(B_KV_RANK, -1)], axis=1))
    qdt, kd, vdt, qmt, km, vmt = _project0(x, ctx, mods0, l0_g_attn, w_in0, tabs,
                                           l0_g_cq, w_uq, l0_g_ckv, w_ukv)
    lamv = jnp.stack([l0_lam_q1, l0_lam_k1, l0_lam_q2, l0_lam_k2])
    o_a = _diff_attention(qdt, kd, vdt, lamv, l0_g_subln)
    o_b = _mla_attention(qmt, km, vmt)
    w_out0 = bf(l0_w_out)
    xs = _merge_ffn((x, ctx), o_a, o_b, mods0, w_out0[:512], w_out0[512:], l0_g_ffn,
                    bf(l0_w1), bf(l0_w3), bf(l0_w2), g_final, latent_only=False)

    mods1 = _ada_params(cond, l1_w_mod, l1_b_mod)
    wq = l1_w_in[:, :512].reshape(D_MODEL, 2, 4, C_HEAD_DIM).transpose(0, 2, 1, 3).reshape(D_MODEL, 512)
    w_in1 = bf(jnp.concatenate([wq, l1_w_in[:, 512:]], axis=1))
    grp = np.arange(512) // 64
    bd = jnp.asarray((grp[:, None] == grp[None, :]) / 64.0, _BF)
    qct, kc, vct, qnt, kn, vnt = _project1(xs, mods1, l1_g_attn, w_in1, tabs, bd, l1_g_qc, l1_g_kc)
    o_c = _gqa_attention(qct, kc, vct)
    o_d = _na_attention(qnt, kn, vnt, _na_bias_table(l1_rpb))
    w_out1 = bf(l1_w_out)
    woc = w_out1[:512].reshape(2, 4, C_HEAD_DIM, D_MODEL).transpose(1, 0, 2, 3).reshape(512, D_MODEL)
    return _merge_ffn((xs,), o_c, o_d, mods1, woc, w_out1[512:], l1_g_ffn,
                      bf(l1_w1), bf(l1_w3), bf(l1_w2), g_final, latent_only=True)
```

```python
import functools
import math

import numpy as np
import jax
import jax.numpy as jnp
from jax import lax
from jax.experimental import pallas as pl
from jax.experimental.pallas import tpu as pltpu

D_MODEL = 1024
BATCH = 4
SEQ = 4096
GRID_W = 64
CTX_LEN = 256
ROPE_THETA = 10000.0
EPS = 1e-6
A_HEADS = 4
A_HEAD_DIM = 64
B_HEADS = 8
B_Q_RANK = 256
B_KV_RANK = 128
B_NOPE_DIM = 64
B_ROPE_DIM = 32
B_V_DIM = 64
C_HEADS = 8
C_KV_HEADS = 2
C_HEAD_DIM = 64
D_HEADS = 8
D_HEAD_DIM = 64
NA_WIN_H = 8
NA_WIN_W = 16
FFN_HIDDEN = 2816
LAMBDA_INIT = 0.8 - 0.6 * math.exp(-0.3 * 0)

S_ALL = CTX_LEN + SEQ
LANES = 128
HALF = LANES // 2
ROW_TILE = 256
NB = 2
N_TILES = S_ALL // ROW_TILE
N_LAT_TILES = SEQ // ROW_TILE
Q2 = 2 * ROW_TILE
KEY_CHUNK = 256
N_KEY_CHUNKS = S_ALL // KEY_CHUNK
ROWS_PER_TILE = ROW_TILE // GRID_W
NA_KEY_ROWS = 12
NA_KEYS = NA_KEY_ROWS * GRID_W
LOG2E = math.log2(math.e)
NEG_BIG = -1e30
VMEM_LIMIT = 52 * 1024 * 1024

assert ROW_TILE == CTX_LEN and S_ALL % ROW_TILE == 0 and N_LAT_TILES % 2 == 0

_BF = jnp.bfloat16
_F32 = jnp.float32


def _dot(a, b):
    return jnp.dot(a, b, preferred_element_type=_F32)


def _rms(x):
    return x * lax.rsqrt(jnp.mean(x * x, axis=-1, keepdims=True) + EPS)


def _lane(shape):
    return lax.broadcasted_iota(jnp.int32, shape, len(shape) - 1)


def _swap_halves(x, half):
    n = x.shape[-1]
    up = pltpu.roll(x, n - half, axis=1)
    down = pltpu.roll(x, half, axis=1)
    return jnp.where((_lane(x.shape) % (2 * half)) < half, up, down)


def _rope(x, c, s, half):
    return x * c + _swap_halves(x, half) * s


def _rope_t(x, c, s, half):
    parts = []
    for r in range(0, x.shape[0], 2 * half):
        parts += [x[r + half:r + 2 * half], x[r:r + half]]
    return x * c + jnp.concatenate(parts, axis=0) * s


def _dot_nt(a, b):
    return lax.dot_general(a, b, (((1,), (1,)), ((), ())), preferred_element_type=_F32)


def _group_mean_sq(x, bd):
    x2 = x * x
    hi = x2.astype(_BF)
    lo = (x2 - hi.astype(_F32)).astype(_BF)
    return _dot(hi, bd) + _dot(lo, bd)


def _ada_body(cond_ref, w_ref, b_ref, o_ref):
    c = cond_ref[...]
    a = (c * (1.0 / (1.0 + jnp.exp(-c)))).astype(_BF)
    o_ref[...] = _dot(a, w_ref[...].astype(_BF)) + b_ref[...]


def _ada_params(cond, w_mod, b_mod):
    n = w_mod.shape[1]
    tn = n // 4
    out = pl.pallas_call(
        _ada_body,
        grid=(n // tn,),
        in_specs=[pl.BlockSpec((8, D_MODEL), lambda j: (0, 0)),
                  pl.BlockSpec((D_MODEL, tn), lambda j: (0, j)),
                  pl.BlockSpec((1, tn), lambda j: (0, j))],
        out_specs=pl.BlockSpec((8, tn), lambda j: (0, j)),
        out_shape=jax.ShapeDtypeStruct((8, n), _F32),
        compiler_params=pltpu.CompilerParams(vmem_limit_bytes=VMEM_LIMIT),
        name="ada_params",
    )(cond, w_mod, b_mod.reshape(1, n))
    return out.reshape(8, 6, D_MODEL)


def _mod_row(b, t):
    return jnp.where(t == 0, BATCH // NB, b)


def _const_spec(shape):
    nd = len(shape)
    return pl.BlockSpec(shape, lambda *_: (0,) * nd, pipeline_mode=pl.Buffered(1))


def _modulated(x, m, g, shift_row, scale_row):
    h = _rms(x) * g
    return h * (1.0 + m[scale_row:scale_row + 1]) + m[shift_row:shift_row + 1]


def _modulated_rows(xs, m, g, shift_row, scale_row):
    return jnp.concatenate([_modulated(xs[i], m[i], g, shift_row, scale_row) for i in range(NB)],
                           axis=0).astype(_BF)


def _stream_tile(x_ref, ctx_ref):
    return jnp.where(pl.program_id(1) == 0, ctx_ref[...], x_ref[...])


def _stream_specs():
    return [pl.BlockSpec((NB, ROW_TILE, D_MODEL), lambda b, t: (b, jnp.maximum(t - 1, 0), 0)),
            pl.BlockSpec((NB, CTX_LEN, D_MODEL), lambda b, t: (b, 0, 0))]


def _proj0_body(x_ref, ctx_ref, mod_ref, g_ref, wn_ref, wt_ref,
                ca_ref, sa_ref, cb_ref, sb_ref, cat_ref, sat_ref, cbt_ref, sbt_ref,
                g_cq_ref, w_uqt_ref, g_ckv_ref, w_uk_ref, w_uvt_ref,
                qdt_ref, kd_ref, vdt_ref, qmt_ref, km_ref, vmt_ref):
    h = _modulated_rows(_stream_tile(x_ref, ctx_ref), mod_ref[...], g_ref[...], 0, 1)
    yn = _dot(h, wn_ref[...])
    yt = _dot_nt(wt_ref[...], h)
    cq = (_rms(yn[:, 512:768]) * g_cq_ref[...]).astype(_BF)
    ckv = (_rms(yn[:, 768:896]) * g_ckv_ref[...]).astype(_BF)
    qmt = _dot_nt(w_uqt_ref[...], cq)
    kn = _dot(ckv, w_uk_ref[...])
    vmt = _dot_nt(w_uvt_ref[...], ckv)
    ca, sa, cb, sb = ca_ref[...], sa_ref[...], cb_ref[...], sb_ref[...]
    cat, sat, cbt, sbt = cat_ref[...], sat_ref[...], cbt_ref[...], sbt_ref[...]
    qscale = A_HEAD_DIM ** -0.5 * LOG2E
    mscale = (B_NOPE_DIM + B_ROPE_DIM) ** -0.5 * LOG2E
    for i in range(NB):
        rows = slice(i * ROW_TILE, (i + 1) * ROW_TILE)
        for hd in range(A_HEADS):
            lo = hd * LANES
            qdt_ref[i, hd] = (_rope_t(yt[lo:lo + LANES, rows], cat, sat, A_HEAD_DIM // 2) * qscale).astype(_BF)
            kd_ref[i, hd] = _rope(yn[rows, lo:lo + LANES], ca, sa, A_HEAD_DIM // 2).astype(_BF)
            vdt_ref[i, hd] = yt[512 + lo:512 + lo + LANES, rows].astype(_BF)
        kr = _rope(yn[rows, 896:1024], cb, sb, B_ROPE_DIM // 2)
        for hd in range(B_HEADS):
            lo = hd * LANES
            qmt_ref[i, hd] = (_rope_t(qmt[lo:lo + LANES, rows], cbt, sbt, B_ROPE_DIM // 2) * mscale).astype(_BF)
            km_ref[i, hd] = (kn[rows, lo:lo + LANES] + kr).astype(_BF)
        for pr in range(B_HEADS // 2):
            vmt_ref[i, pr] = vmt[pr * LANES:(pr + 1) * LANES, rows].astype(_BF)


def _head_out(n_heads, transposed):
    if transposed:
        return (jax.ShapeDtypeStruct((BATCH, n_heads, LANES, S_ALL), _BF),
                pl.BlockSpec((NB, n_heads, LANES, ROW_TILE), lambda b, t: (b, 0, 0, t)))
    return (jax.ShapeDtypeStruct((BATCH, n_heads, S_ALL, LANES), _BF),
            pl.BlockSpec((NB, n_heads, ROW_TILE, LANES), lambda b, t: (b, 0, t, 0)))


def _row_spec(width):
    return pl.BlockSpec((NB, ROW_TILE, width), lambda b, t: (b, t, 0))


def _mod_spec():
    return pl.BlockSpec((NB, 6, D_MODEL), lambda b, t: (_mod_row(b, t), 0, 0))


def _table_spec():
    return pl.BlockSpec((ROW_TILE, LANES), lambda b, t: (t, 0))


def _table_t_spec():
    return pl.BlockSpec((LANES, ROW_TILE), lambda b, t: (0, t))


def _project0(x, ctx, mods, g_attn, w_in, tabs, tabs_t, g_cq, w_uq, g_ckv, w_ukv):
    wn = jnp.concatenate([w_in[:, 512:1024], w_in[:, 1536:2048]], axis=1)
    wt = jnp.concatenate([w_in[:, 0:512], w_in[:, 1024:1536]], axis=1).T
    w_uqt = w_uq.T
    w_uk, w_uvt = w_ukv[:, :B_HEADS * LANES], w_ukv[:, B_HEADS * LANES:].T
    outs = [_head_out(4, True), _head_out(4, False), _head_out(4, True),
            _head_out(8, True), _head_out(8, False), _head_out(4, True)]
    return pl.pallas_call(
        _proj0_body,
        grid=(BATCH // NB, N_TILES),
        in_specs=_stream_specs() + [_mod_spec(), _const_spec((1, D_MODEL)),
                  _const_spec(wn.shape), _const_spec(wt.shape)]
                 + [_table_spec()] * 4 + [_table_t_spec()] * 4
                 + [_const_spec((1, B_Q_RANK)), _const_spec(w_uqt.shape),
                    _const_spec((1, B_KV_RANK)), _const_spec(w_uk.shape), _const_spec(w_uvt.shape)],
        out_specs=[o[1] for o in outs],
        out_shape=[o[0] for o in outs],
        compiler_params=pltpu.CompilerParams(vmem_limit_bytes=VMEM_LIMIT),
        name="project0",
    )(x, ctx, mods, g_attn.reshape(1, -1), wn, wt, *tabs, *tabs_t,
      g_cq.reshape(1, -1), w_uqt, g_ckv.reshape(1, -1), w_uk, w_uvt)


def _proj1_body(x_ref, mod_ref, g_ref, wn_ref, wt_ref, ca_ref, sa_ref, cat_ref, sat_ref, bd_ref,
                g_qc_ref, g_kc_ref,
                qct_ref, kc_ref, vct_ref, qnt_ref, kn_ref, vnt_ref):
    h = _modulated_rows(x_ref[...], mod_ref[...], g_ref[...], 0, 1)
    yn = _dot(h, wn_ref[...])
    yt = _dot_nt(wt_ref[...], h)
    ca, sa, cat, sat = ca_ref[...], sa_ref[...], cat_ref[...], sat_ref[...]
    qscale = C_HEAD_DIM ** -0.5 * LOG2E
    nscale = D_HEAD_DIM ** -0.5 * LOG2E

    kc_all = yn[:, 0:LANES]
    kc_all = kc_all * lax.rsqrt(_group_mean_sq(kc_all, bd_ref[...]) + EPS) * g_kc_ref[...]
    q3 = yt[0:512].reshape(C_HEADS, C_HEAD_DIM, NB * ROW_TILE)
    q3 = q3 * lax.rsqrt(jnp.mean(q3 * q3, axis=1, keepdims=True) + EPS)
    qc_all = q3.reshape(512, NB * ROW_TILE) * g_qc_ref[...]
    for i in range(NB):
        rows = slice(i * ROW_TILE, (i + 1) * ROW_TILE)
        kc_ref[i, 0] = _rope(kc_all[rows], ca, sa, C_HEAD_DIM // 2).astype(_BF)
        vct_ref[i, 0] = yt[512:640, rows].astype(_BF)
        for pr in range(4):
            lo = pr * LANES
            qct_ref[i, pr] = (_rope_t(qc_all[lo:lo + LANES, rows], cat, sat, C_HEAD_DIM // 2) * qscale).astype(_BF)
            qnt_ref[i, pr] = (yt[640 + lo:640 + lo + LANES, rows] * nscale).astype(_BF)
            kn_ref[i, pr] = yn[rows, LANES + lo:2 * LANES + lo].astype(_BF)
            vnt_ref[i, pr] = yt[1152 + lo:1152 + lo + LANES, rows].astype(_BF)


def _project1(xs, mods, g_attn, w_in, tabs, tabs_t, bd, g_qc, g_kc):
    wn = jnp.concatenate([w_in[:, 512:640], w_in[:, 1280:1792]], axis=1)
    wt = jnp.concatenate([w_in[:, 0:512], w_in[:, 640:1280], w_in[:, 1792:2304]], axis=1).T
    outs = [_head_out(4, True), _head_out(1, False), _head_out(1, True),
            _head_out(4, True), _head_out(4, False), _head_out(4, True)]
    return pl.pallas_call(
        _proj1_body,
        grid=(BATCH // NB, N_TILES),
        in_specs=[_row_spec(D_MODEL), _mod_spec(), _const_spec((1, D_MODEL)),
                  _const_spec(wn.shape), _const_spec(wt.shape),
                  _table_spec(), _table_spec(), _table_t_spec(), _table_t_spec(),
                  _const_spec(bd.shape),
                  _const_spec((512, 1)), _const_spec((1, LANES))],
        out_specs=[o[1] for o in outs],
        out_shape=[o[0] for o in outs],
        compiler_params=pltpu.CompilerParams(vmem_limit_bytes=VMEM_LIMIT),
        name="project1",
    )(xs, mods, g_attn.reshape(1, -1), wn, wt, tabs[0], tabs[1], tabs_t[0], tabs_t[1], bd,
      jnp.tile(g_qc, 8).reshape(-1, 1), jnp.tile(g_kc, 2).reshape(1, -1))


def _split_q_cols(qt):
    z = jnp.zeros((HALF, qt.shape[1]), qt.dtype)
    return jnp.concatenate([jnp.concatenate([qt[:HALF], z], axis=0),
                            jnp.concatenate([z, qt[HALF:]], axis=0)], axis=1)


def _pipelined_tiles(load_q, score_chunk, value_chunk, finalize, n_chunks, bufs, score_group=1):
    def step(t_next, nxt, t_cur, cur):
        mx = None
        if t_next is not None:
            qb = load_q(t_next)
        if t_cur is not None:
            m_cur = cur[1][0:1, :]
            ls = jnp.zeros((1, Q2), _F32)
            acc = jnp.zeros((LANES, Q2), _F32)
        for c in range(n_chunks):
            rows = slice(c * KEY_CHUNK, (c + 1) * KEY_CHUNK)
            if t_next is not None and c % score_group == 0:
                n = min(score_group, n_chunks - c)
                s = score_chunk(t_next, qb, c, n)
                nxt[0][c * KEY_CHUNK:(c + n) * KEY_CHUNK, :] = s
                cm = jnp.max(s, axis=0, keepdims=True)
                mx = cm if mx is None else jnp.maximum(mx, cm)
            if t_cur is not None:
                e = jnp.exp2(cur[0][rows, :] - m_cur)
                ls = ls + jnp.sum(e, axis=0, keepdims=True)
                acc = acc + value_chunk(t_cur, c, e.astype(_BF))
        if t_next is not None:
            nxt[1][...] = jnp.broadcast_to(mx, nxt[1].shape)
        return (acc, ls) if t_cur is not None else None

    last = N_LAT_TILES - 1
    step(0, bufs[0], None, None)

    def pair(j, carry):
        finalize(jnp.maximum(2 * j - 1, 0), *carry)
        finalize(2 * j, *step(2 * j + 1, bufs[1], 2 * j, bufs[0]))
        return step(2 * j + 2, bufs[0], 2 * j + 1, bufs[1])

    carry = lax.fori_loop(0, N_LAT_TILES // 2 - 1, pair,
                          (jnp.zeros((LANES, Q2), _F32), jnp.ones((1, Q2), _F32)))
    finalize(last - 2, *carry)
    finalize(last - 1, *step(last, bufs[1], last - 1, bufs[0]))
    finalize(last, *step(None, None, last, bufs[1]))


def _dense_body(*refs, n_q, diff, ctx_tile, o_off):
    q_refs, k_refs, vt_ref = refs[:n_q], refs[n_q:2 * n_q], refs[2 * n_q]
    rest = refs[2 * n_q + 1:]
    if diff:
        lam_ref, g_ref, o_ref, s0, s1, m0, m1 = rest
        lv = lam_ref[...]
        lam = (jnp.exp(jnp.sum(lv[0:1] * lv[1:2], axis=-1, keepdims=True))
               - jnp.exp(jnp.sum(lv[2:3] * lv[3:4], axis=-1, keepdims=True)) + LAMBDA_INIT)
    else:
        o_ref, s0, s1, m0, m1 = rest

    def q_blocks(col0):
        if n_q == 1:
            return [_split_q_cols(q_refs[0][0, 0, :, pl.ds(col0, ROW_TILE)])]
        return [r[0, 0, :, pl.ds(col0, ROW_TILE)] for r in q_refs]

    def scores(qb, row0, rows):
        parts = [_dot(k_refs[i][0, 0, pl.ds(row0, rows), :], qb[i]) for i in range(n_q)]
        return parts[0] if n_q == 1 else jnp.concatenate(parts, axis=1)

    def finalize_rows(row0, acc, ls):
        ot = acc / ls
        if diff:
            o = (ot[:, :ROW_TILE] - lam * ot[:, ROW_TILE:]).T
            o = _rms(o) * g_ref[...] * (1.0 - LAMBDA_INIT)
        else:
            o = jnp.concatenate([ot[:HALF, :ROW_TILE], ot[HALF:, ROW_TILE:]], axis=0).T
        o_ref[0, pl.ds(row0, ROW_TILE), :] = o.astype(_BF)

    def load_q(t):
        return q_blocks(pl.multiple_of(CTX_LEN + t * ROW_TILE, ROW_TILE))

    def score_chunk(t, qb, c, n):
        return scores(qb, c * KEY_CHUNK, n * KEY_CHUNK)

    def value_chunk(t, c, e):
        return _dot(vt_ref[0, 0, :, c * KEY_CHUNK:(c + 1) * KEY_CHUNK], e)

    def finalize(t, acc, ls):
        finalize_rows(pl.multiple_of(o_off + t * ROW_TILE, ROW_TILE), acc, ls)

    _pipelined_tiles(load_q, score_chunk, value_chunk, finalize, N_KEY_CHUNKS, ((s0, m0), (s1, m1)))

    if ctx_tile:
        s = scores(q_blocks(0), 0, CTX_LEN)
        e = jnp.exp2(s - jnp.max(s, axis=0, keepdims=True))
        finalize_rows(0, _dot(vt_ref[0, 0, :, 0:CTX_LEN], e.astype(_BF)),
                      jnp.sum(e, axis=0, keepdims=True))


def _na_body(qt_ref, k_ref, vt_ref, bias_ref, o_ref, s0, s1, m0, m1):
    def key_row0(t, c):
        first = jnp.clip(t - 1, 0, N_LAT_TILES - 3)
        return 0 if c == 0 else pl.multiple_of(CTX_LEN + (first + c - 1) * KEY_CHUNK, KEY_CHUNK)

    def load_q(t):
        return _split_q_cols(qt_ref[0, 0, :, pl.ds(pl.multiple_of(CTX_LEN + t * ROW_TILE, ROW_TILE), ROW_TILE)])

    def score_chunk(t, qb, c, n):
        s = _dot(k_ref[0, 0, pl.ds(key_row0(t, c), KEY_CHUNK), :], qb)
        if c > 0:
            kind = jnp.where(t == 0, 0, jnp.where(t == N_LAT_TILES - 1, 2, 1))
            s = s + bias_ref[0, kind, (c - 1) * KEY_CHUNK:c * KEY_CHUNK, :]
        return s

    def value_chunk(t, c, e):
        return _dot(vt_ref[0, 0, :, pl.ds(key_row0(t, c), KEY_CHUNK)], e)

    def finalize(t, acc, ls):
        ot = acc / ls
        o = jnp.concatenate([ot[:HALF, :ROW_TILE], ot[HALF:, ROW_TILE:]], axis=0).T
        o_ref[0, pl.ds(pl.multiple_of(t * ROW_TILE, ROW_TILE), ROW_TILE), :] = o.astype(_BF)

    _pipelined_tiles(load_q, score_chunk, value_chunk, finalize, 1 + NA_KEYS // KEY_CHUNK,
                     ((s0, m0), (s1, m1)))


def _attn_call(body, name, in_specs, arrays, out_rows, n_keys):
    return pl.pallas_call(
        body,
        grid=(BATCH, 4),
        in_specs=in_specs,
        out_specs=pl.BlockSpec((1, out_rows, LANES), lambda b, g: (b, 0, g)),
        out_shape=jax.ShapeDtypeStruct((BATCH, out_rows, 4 * LANES), _BF),
        scratch_shapes=[pltpu.VMEM((n_keys, Q2), _F32), pltpu.VMEM((n_keys, Q2), _F32),
                        pltpu.VMEM((8, Q2), _F32), pltpu.VMEM((8, Q2), _F32)],
        compiler_params=pltpu.CompilerParams(vmem_limit_bytes=VMEM_LIMIT),
        name=name,
    )(*arrays)


def _t_spec(f):
    return pl.BlockSpec((1, 1, LANES, S_ALL), f)


def _n_spec(f):
    return pl.BlockSpec((1, 1, S_ALL, LANES), f)


def _diff_attention(qdt, kd, vdt, lamv, g_subln):
    head = lambda b, g: (b, g, 0, 0)
    body = functools.partial(_dense_body, n_q=1, diff=True, ctx_tile=True, o_off=CTX_LEN)
    return _attn_call(body, "diff_attention",
                      [_t_spec(head), _n_spec(head), _t_spec(head),
                       pl.BlockSpec((4, A_HEAD_DIM), lambda b, g: (0, 0)),
                       pl.BlockSpec((1, LANES), lambda b, g: (0, 0))],
                      (qdt, kd, vdt, lamv, g_subln.reshape(1, -1)), S_ALL, S_ALL)


def _mla_attention(qmt, km, vmt):
    ha = lambda b, g: (b, 2 * g, 0, 0)
    hb = lambda b, g: (b, 2 * g + 1, 0, 0)
    body = functools.partial(_dense_body, n_q=2, diff=False, ctx_tile=True, o_off=CTX_LEN)
    return _attn_call(body, "mla_attention",
                      [_t_spec(ha), _t_spec(hb), _n_spec(ha), _n_spec(hb),
                       _t_spec(lambda b, g: (b, g, 0, 0))],
                      (qmt, qmt, km, km, vmt), S_ALL, S_ALL)


def _gqa_attention(qct, kc, vct):
    shared = lambda b, g: (b, 0, 0, 0)
    body = functools.partial(_dense_body, n_q=1, diff=False, ctx_tile=False, o_off=0)
    return _attn_call(body, "gqa_attention",
                      [_t_spec(lambda b, g: (b, g, 0, 0)), _n_spec(shared), _t_spec(shared)],
                      (qct, kc, vct), SEQ, S_ALL)


def _na_attention(qnt, kn, vnt, bias_t):
    head = lambda b, g: (b, g, 0, 0)
    return _attn_call(_na_body, "na_attention",
                      [_t_spec(head), _n_spec(head), _t_spec(head),
                       pl.BlockSpec((1, 3, NA_KEYS, Q2), lambda b, g: (g, 0, 0, 0))],
                      (qnt, kn, vnt, bias_t), SEQ, CTX_LEN + NA_KEYS)


def _merge_ffn_body(*refs, final):
    x = refs[0][...] if final else _stream_tile(refs[0], refs[1])
    (oa_ref, ob_ref, mod_ref, woa_ref, wob_ref, g_ref,
     w1_ref, w3_ref, w2_ref, gf_ref, o_ref) = refs[1 if final else 2:]
    m = mod_ref[...]
    stack = lambda r: r[...].reshape(NB * ROW_TILE, r.shape[-1])
    rows = lambda v, i: v[i * ROW_TILE:(i + 1) * ROW_TILE]
    attn = _dot(stack(oa_ref), woa_ref[...]) + _dot(stack(ob_ref), wob_ref[...])
    x1 = [x[i] + m[i, 2:3] * rows(attn, i) for i in range(NB)]
    h = _modulated_rows(x1, m, g_ref[...], 3, 4)
    a = _dot(h, w1_ref[...])
    b = _dot(h, w3_ref[...])
    u = (a * (1.0 / (1.0 + jnp.exp(-a))) * b).astype(_BF)
    f = _dot(u, w2_ref[...])
    for i in range(NB):
        x2 = x1[i] + m[i, 5:6] * rows(f, i)
        if final:
            x2 = _rms(x2) * gf_ref[...]
        o_ref[i] = x2


def _merge_ffn(streams, oa, ob, mods, woa, wob, g_ffn, w1, w3, w2, g_final, *, latent_only):
    off = 1 if latent_only else 0
    tiles = N_LAT_TILES if latent_only else N_TILES
    row = lambda w, o: pl.BlockSpec((NB, ROW_TILE, w), lambda b, t: (b, t + o, 0))
    mod = pl.BlockSpec((NB, 6, D_MODEL), lambda b, t: (_mod_row(b, t + off), 0, 0))
    x_specs = [row(D_MODEL, off)] if latent_only else _stream_specs()
    return pl.pallas_call(
        functools.partial(_merge_ffn_body, final=latent_only),
        grid=(BATCH // NB, tiles),
        in_specs=x_specs + [row(512, 0), row(512, 0), mod,
                  _const_spec(woa.shape), _const_spec(wob.shape), _const_spec((1, D_MODEL)),
                  _const_spec(w1.shape), _const_spec(w3.shape), _const_spec(w2.shape),
                  _const_spec((1, D_MODEL))],
        out_specs=row(D_MODEL, 0),
        out_shape=jax.ShapeDtypeStruct((BATCH, tiles * ROW_TILE, D_MODEL), _F32),
        compiler_params=pltpu.CompilerParams(vmem_limit_bytes=VMEM_LIMIT),
        name="merge_ffn_final" if latent_only else "merge_ffn",
    )(*streams, oa, ob, mods, woa, wob, g_ffn.reshape(1, -1), w1, w3, w2, g_final.reshape(1, -1))


def _rope_tables():
    t = np.arange(SEQ)
    rows = jnp.asarray(t // GRID_W, _F32)
    cols = jnp.asarray(t % GRID_W, _F32)

    def cs(rot_dim):
        axis_dim = rot_dim // 2
        inv = ROPE_THETA ** (-jnp.arange(0, axis_dim, 2, dtype=_F32) / axis_dim)
        ang = jnp.concatenate([rows[:, None] * inv, cols[:, None] * inv], axis=-1)
        return jnp.cos(ang), jnp.sin(ang)

    def with_ctx(tab, fill):
        return jnp.concatenate([jnp.full((CTX_LEN, LANES), fill, _F32), tab], axis=0)

    cos, sin = cs(A_HEAD_DIM)
    ca = with_ctx(jnp.tile(cos, (1, 4)), 1.0)
    sa = with_ctx(jnp.tile(jnp.concatenate([-sin, sin], axis=-1), (1, 2)), 0.0)
    cos, sin = cs(B_ROPE_DIM)
    ones = jnp.ones((SEQ, 64), _F32)
    zeros = jnp.zeros((SEQ, 64), _F32)
    cb = with_ctx(jnp.concatenate([ones, cos, cos, ones[:, :32]], axis=-1), 1.0)
    sb = with_ctx(jnp.concatenate([zeros, -sin, sin, zeros[:, :32]], axis=-1), 0.0)
    return ca, sa, cb, sb


def _na_bias_table(rpb):
    rows_n = SEQ // GRID_W
    span = 2 * GRID_W - 1
    left = GRID_W - NA_WIN_W
    p = jnp.pad(rpb * LOG2E, ((0, 0), (0, 0), (left, span - left - (2 * NA_WIN_W - 1))))
    flat = jnp.tile(p[:, :, ::-1], (1, 1, GRID_W + 1))[..., :GRID_W * (span + 1)]
    toe = flat.reshape(D_HEADS, 2 * NA_WIN_H - 1, GRID_W, span + 1)[:, :, ::-1, :GRID_W]
    kc = np.arange(GRID_W)[:, None]
    qc = np.arange(GRID_W)[None, :]
    cs = np.clip(qc - NA_WIN_W // 2, 0, GRID_W - NA_WIN_W)
    toe = jnp.where((kc >= cs) & (kc < cs + NA_WIN_W), toe, NEG_BIG)
    n_dr = 2 * NA_WIN_H - 1
    edge = ROWS_PER_TILE - 1
    n_quads = n_dr + edge

    def quads(t):
        fill = jnp.full((D_HEADS, edge, GRID_W, GRID_W), NEG_BIG, _F32)
        tp = jnp.concatenate([fill, t, fill], axis=1).reshape(4, 2, n_dr + 2 * edge, GRID_W, GRID_W)
        return jnp.concatenate([tp[:, w, edge - qr:edge - qr + n_quads]
                                for w in (0, 1) for qr in range(ROWS_PER_TILE)], axis=-1)

    dr = np.arange(n_dr)[None, :, None, None]
    lo = NA_WIN_H - 1 - NA_WIN_H // 2
    q_int = quads(jnp.where((dr >= lo) & (dr < lo + NA_WIN_H), toe, NEG_BIG))
    q_all = quads(toe)

    def tile(q, q_row0, k_row0):
        rs = [min(max(q_row0 + qr - NA_WIN_H // 2, 0), rows_n - NA_WIN_H) for qr in range(ROWS_PER_TILE)]
        parts = []
        for kr in range(NA_KEY_ROWS):
            inside = [r0 <= k_row0 + kr < r0 + NA_WIN_H for r0 in rs]
            d = k_row0 + kr - q_row0 + NA_WIN_H - 1
            assert q is q_int or all(inside) or not any(inside)
            if q is q_int or all(inside):
                parts.append(q[:, d])
            else:
                parts.append(jnp.full((4, GRID_W, Q2), NEG_BIG, _F32))
        return jnp.concatenate(parts, axis=1)

    return jnp.stack([tile(q_all, 0, 0),
                      tile(q_int, ROWS_PER_TILE, 0),
                      tile(q_all, rows_n - ROWS_PER_TILE, rows_n - NA_KEY_ROWS)], axis=1)


def _pad_cols(w, groups, width, total, offset=0):
    k = w.shape[0]
    w = w.reshape(k, groups, width)
    w = jnp.pad(w, ((0, 0), (0, 0), (offset, total - width - offset)))
    return w.reshape(k, groups * total)


def kernel(x, c, ctx, c_ctx, l0_w_mod, l0_b_mod, l0_g_attn, l0_w_in, l0_lam_q1, l0_lam_k1, l0_lam_q2, l0_lam_k2, l0_g_subln, l0_g_cq, l0_w_uq, l0_g_ckv, l0_w_ukv, l0_w_out, l0_g_ffn, l0_w1, l0_w3, l0_w2, l1_w_mod, l1_b_mod, l1_g_attn, l1_w_in, l1_g_qc, l1_g_kc, l1_rpb, l1_w_out, l1_g_ffn, l1_w1, l1_w3, l1_w2, g_final):
    bf = lambda w: w.astype(_BF)
    cond = jnp.concatenate([c, jnp.tile(c_ctx[None], (NB, 1)),
                            jnp.zeros((8 - BATCH - NB, D_MODEL), _F32)], axis=0)
    tabs = _rope_tables()
    tabs_t = [t.T for t in tabs]

    mods0 = _ada_params(cond, l0_w_mod, l0_b_mod)
    w_in0 = bf(jnp.concatenate([l0_w_in[:, :1920],
                                _pad_cols(l0_w_in[:, 1920:], 1, B_ROPE_DIM, LANES, B_NOPE_DIM)], axis=1))
    w_uq = bf(_pad_cols(l0_w_uq, B_HEADS, B_NOPE_DIM + B_ROPE_DIM, LANES))
    ukv = l0_w_ukv.reshape(B_KV_RANK, B_HEADS, B_NOPE_DIM + B_V_DIM)
    w_ukv = bf(jnp.concatenate([_pad_cols(ukv[:, :, :B_NOPE_DIM].reshape(B_KV_RANK, -1), B_HEADS, B_NOPE_DIM, LANES),
                                ukv[:, :, B_NOPE_DIM:].reshape(B_KV_RANK, -1)], axis=1))
    qdt, kd, vdt, qmt, km, vmt = _project0(x, ctx, mods0, l0_g_attn, w_in0, tabs, tabs_t,
                                           l0_g_cq, w_uq, l0_g_ckv, w_ukv)
    lamv = jnp.stack([l0_lam_q1, l0_lam_k1, l0_lam_q2, l0_lam_k2])
    o_a = _diff_attention(qdt, kd, vdt, lamv, l0_g_subln)
    o_b = _mla_attention(qmt, km, vmt)
    w_out0 = bf(l0_w_out)
    xs = _merge_ffn((x, ctx), o_a, o_b, mods0, w_out0[:512], w_out0[512:], l0_g_ffn,
                    bf(l0_w1), bf(l0_w3), bf(l0_w2), g_final, latent_only=False)

    mods1 = _ada_params(cond, l1_w_mod, l1_b_mod)
    wq = l1_w_in[:, :512].reshape(D_MODEL, 2, 4, C_HEAD_DIM).transpose(0, 2, 1, 3).reshape(D_MODEL, 512)
    w_in1 = bf(jnp.concatenate([wq, l1_w_in[:, 512:]], axis=1))
    grp = np.arange(LANES) // C_HEAD_DIM
    bd = jnp.asarray((grp[:, None] == grp[None, :]) / C_HEAD_DIM, _BF)
    qct, kc, vct, qnt, kn, vnt = _project1(xs, mods1, l1_g_attn, w_in1, tabs, tabs_t, bd, l1_g_qc, l1_g_kc)
    o_c = _gqa_attention(qct, kc, vct)
    o_d = _na_attention(qnt, kn, vnt, _na_bias_table(l1_rpb))
    w_out1 = bf(l1_w_out)
    woc = w_out1[:512].reshape(2, 4, C_HEAD_DIM, D_MODEL).transpose(1, 0, 2, 3).reshape(512, D_MODEL)
    return _merge_ffn((xs,), o_c, o_d, mods1, woc, w_out1[512:], l1_g_ffn,
                      bf(l1_w1), bf(l1_w3), bf(l1_w2), g_final, latent_only=True)
```

```python
import functools
import math

import numpy as np
import jax
import jax.numpy as jnp
from jax import lax
from jax.experimental import pallas as pl
from jax.experimental.pallas import tpu as pltpu

D_MODEL = 1024
BATCH = 4
SEQ = 4096
GRID_W = 64
CTX_LEN = 256
ROPE_THETA = 10000.0
EPS = 1e-6
A_HEADS = 4
A_HEAD_DIM = 64
B_HEADS = 8
B_Q_RANK = 256
B_KV_RANK = 128
B_NOPE_DIM = 64
B_ROPE_DIM = 32
B_V_DIM = 64
C_HEADS = 8
C_KV_HEADS = 2
C_HEAD_DIM = 64
D_HEADS = 8
D_HEAD_DIM = 64
NA_WIN_H = 8
NA_WIN_W = 16
FFN_HIDDEN = 2816
LAMBDA_INIT = 0.8 - 0.6 * math.exp(-0.3 * 0)

S_ALL = CTX_LEN + SEQ
LANES = 128
HALF = LANES // 2
ROW_TILE = 256
NB = 2
N_TILES = S_ALL // ROW_TILE
N_LAT_TILES = SEQ // ROW_TILE
Q2 = 2 * ROW_TILE
KEY_CHUNK = 256
N_KEY_CHUNKS = S_ALL // KEY_CHUNK
ROWS_PER_TILE = ROW_TILE // GRID_W
NA_KEY_ROWS = 12
NA_KEYS = NA_KEY_ROWS * GRID_W
LOG2E = math.log2(math.e)
NEG_BIG = -1e30
VMEM_LIMIT = 52 * 1024 * 1024

assert ROW_TILE == CTX_LEN and S_ALL % ROW_TILE == 0 and N_LAT_TILES % 2 == 0

_BF = jnp.bfloat16
_F32 = jnp.float32


def _dot(a, b):
    return jnp.dot(a, b, preferred_element_type=_F32)


def _rms(x):
    return x * lax.rsqrt(jnp.mean(x * x, axis=-1, keepdims=True) + EPS)


def _lane(shape):
    return lax.broadcasted_iota(jnp.int32, shape, len(shape) - 1)


def _swap_halves(x, half):
    n = x.shape[-1]
    up = pltpu.roll(x, n - half, axis=1)
    down = pltpu.roll(x, half, axis=1)
    return jnp.where((_lane(x.shape) % (2 * half)) < half, up, down)


def _rope(x, c, s, half):
    return x * c + _swap_halves(x, half) * s


def _rope_t(x, c, s, half):
    parts = []
    for r in range(0, x.shape[0], 2 * half):
        parts += [x[r + half:r + 2 * half], x[r:r + half]]
    return x * c + jnp.concatenate(parts, axis=0) * s


def _dot_nt(a, b):
    return lax.dot_general(a, b, (((1,), (1,)), ((), ())), preferred_element_type=_F32)


def _group_mean_sq(x, bd):
    x2 = x * x
    hi = x2.astype(_BF)
    lo = (x2 - hi.astype(_F32)).astype(_BF)
    return _dot(hi, bd) + _dot(lo, bd)


def _ada_body(cond_ref, w_ref, b_ref, o_ref):
    c = cond_ref[...]
    a = (c * (1.0 / (1.0 + jnp.exp(-c)))).astype(_BF)
    o_ref[...] = _dot(a, w_ref[...].astype(_BF)) + b_ref[...]


def _ada_params(cond, w_mod, b_mod):
    n = w_mod.shape[1]
    tn = n // 4
    out = pl.pallas_call(
        _ada_body,
        grid=(n // tn,),
        in_specs=[pl.BlockSpec((8, D_MODEL), lambda j: (0, 0)),
                  pl.BlockSpec((D_MODEL, tn), lambda j: (0, j)),
                  pl.BlockSpec((1, tn), lambda j: (0, j))],
        out_specs=pl.BlockSpec((8, tn), lambda j: (0, j)),
        out_shape=jax.ShapeDtypeStruct((8, n), _F32),
        compiler_params=pltpu.CompilerParams(vmem_limit_bytes=VMEM_LIMIT),
        name="ada_params",
    )(cond, w_mod, b_mod.reshape(1, n))
    return out.reshape(8, 6, D_MODEL)


def _mod_row(b, t):
    return jnp.where(t == 0, BATCH // NB, b)


def _const_spec(shape):
    nd = len(shape)
    return pl.BlockSpec(shape, lambda *_: (0,) * nd, pipeline_mode=pl.Buffered(1))


def _modulated(x, m, g, shift_row, scale_row):
    h = _rms(x) * g
    return h * (1.0 + m[scale_row:scale_row + 1]) + m[shift_row:shift_row + 1]


def _modulated_rows(xs, m, g, shift_row, scale_row):
    return jnp.concatenate([_modulated(xs[i], m[i], g, shift_row, scale_row) for i in range(NB)],
                           axis=0).astype(_BF)


def _stream_tile(x_ref, ctx_ref):
    return jnp.where(pl.program_id(1) == 0, ctx_ref[...], x_ref[...])


def _stream_specs():
    return [pl.BlockSpec((NB, ROW_TILE, D_MODEL), lambda b, t: (b, jnp.maximum(t - 1, 0), 0)),
            pl.BlockSpec((NB, CTX_LEN, D_MODEL), lambda b, t: (b, 0, 0))]


def _proj0_body(x_ref, ctx_ref, mod_ref, g_ref, wn_ref, wt_ref,
                ca_ref, sa_ref, cb_ref, sb_ref, cat_ref, sat_ref, cbt_ref, sbt_ref,
                g_cq_ref, w_uqt_ref, g_ckv_ref, w_uk_ref, w_uvt_ref,
                qdt_ref, kd_ref, vdt_ref, qmt_ref, km_ref, vmt_ref):
    h = _modulated_rows(_stream_tile(x_ref, ctx_ref), mod_ref[...], g_ref[...], 0, 1)
    yn = _dot(h, wn_ref[...])
    yt = _dot_nt(wt_ref[...], h)
    cq = (_rms(yn[:, 512:768]) * g_cq_ref[...]).astype(_BF)
    ckv = (_rms(yn[:, 768:896]) * g_ckv_ref[...]).astype(_BF)
    qmt = _dot_nt(w_uqt_ref[...], cq)
    kn = _dot(ckv, w_uk_ref[...])
    vmt = _dot_nt(w_uvt_ref[...], ckv)
    ca, sa, cb, sb = ca_ref[...], sa_ref[...], cb_ref[...], sb_ref[...]
    cat, sat, cbt, sbt = cat_ref[...], sat_ref[...], cbt_ref[...], sbt_ref[...]
    qscale = A_HEAD_DIM ** -0.5 * LOG2E
    mscale = (B_NOPE_DIM + B_ROPE_DIM) ** -0.5 * LOG2E
    for i in range(NB):
        rows = slice(i * ROW_TILE, (i + 1) * ROW_TILE)
        for hd in range(A_HEADS):
            lo = hd * LANES
            qdt_ref[i, hd] = (_rope_t(yt[lo:lo + LANES, rows], cat, sat, A_HEAD_DIM // 2) * qscale).astype(_BF)
            kd_ref[i, hd] = _rope(yn[rows, lo:lo + LANES], ca, sa, A_HEAD_DIM // 2).astype(_BF)
            vdt_ref[i, hd] = yt[512 + lo:512 + lo + LANES, rows].astype(_BF)
        kr = _rope(yn[rows, 896:1024], cb, sb, B_ROPE_DIM // 2)
        for hd in range(B_HEADS):
            lo = hd * LANES
            qmt_ref[i, hd] = (_rope_t(qmt[lo:lo + LANES, rows], cbt, sbt, B_ROPE_DIM // 2) * mscale).astype(_BF)
            km_ref[i, hd] = (kn[rows, lo:lo + LANES] + kr).astype(_BF)
        for pr in range(B_HEADS // 2):
            vmt_ref[i, pr] = vmt[pr * LANES:(pr + 1) * LANES, rows].astype(_BF)


def _head_out(n_heads, transposed):
    if transposed:
        return (jax.ShapeDtypeStruct((BATCH, n_heads, LANES, S_ALL), _BF),
                pl.BlockSpec((NB, n_heads, LANES, ROW_TILE), lambda b, t: (b, 0, 0, t)))
    return (jax.ShapeDtypeStruct((BATCH, n_heads, S_ALL, LANES), _BF),
            pl.BlockSpec((NB, n_heads, ROW_TILE, LANES), lambda b, t: (b, 0, t, 0)))


def _row_spec(width):
    return pl.BlockSpec((NB, ROW_TILE, width), lambda b, t: (b, t, 0))


def _mod_spec():
    return pl.BlockSpec((NB, 6, D_MODEL), lambda b, t: (_mod_row(b, t), 0, 0))


def _table_spec():
    return pl.BlockSpec((ROW_TILE, LANES), lambda b, t: (t, 0))


def _table_t_spec():
    return pl.BlockSpec((LANES, ROW_TILE), lambda b, t: (0, t))


def _project0(x, ctx, mods, g_attn, w_in, tabs, tabs_t, g_cq, w_uq, g_ckv, w_ukv):
    wn = jnp.concatenate([w_in[:, 512:1024], w_in[:, 1536:2048]], axis=1)
    wt = jnp.concatenate([w_in[:, 0:512], w_in[:, 1024:1536]], axis=1).T
    w_uqt = w_uq.T
    w_uk, w_uvt = w_ukv[:, :B_HEADS * LANES], w_ukv[:, B_HEADS * LANES:].T
    outs = [_head_out(4, True), _head_out(4, False), _head_out(4, True),
            _head_out(8, True), _head_out(8, False), _head_out(4, True)]
    return pl.pallas_call(
        _proj0_body,
        grid=(BATCH // NB, N_TILES),
        in_specs=_stream_specs() + [_mod_spec(), _const_spec((1, D_MODEL)),
                  _const_spec(wn.shape), _const_spec(wt.shape)]
                 + [_table_spec()] * 4 + [_table_t_spec()] * 4
                 + [_const_spec((1, B_Q_RANK)), _const_spec(w_uqt.shape),
                    _const_spec((1, B_KV_RANK)), _const_spec(w_uk.shape), _const_spec(w_uvt.shape)],
        out_specs=[o[1] for o in outs],
        out_shape=[o[0] for o in outs],
        compiler_params=pltpu.CompilerParams(vmem_limit_bytes=VMEM_LIMIT),
        name="project0",
    )(x, ctx, mods, g_attn.reshape(1, -1), wn, wt, *tabs, *tabs_t,
      g_cq.reshape(1, -1), w_uqt, g_ckv.reshape(1, -1), w_uk, w_uvt)


def _proj1_body(x_ref, mod_ref, g_ref, wn_ref, wt_ref, ca_ref, sa_ref, cat_ref, sat_ref, bd_ref,
                g_qc_ref, g_kc_ref,
                qct_ref, kc_ref, vct_ref, qnt_ref, kn_ref, vnt_ref):
    h = _modulated_rows(x_ref[...], mod_ref[...], g_ref[...], 0, 1)
    yn = _dot(h, wn_ref[...])
    yt = _dot_nt(wt_ref[...], h)
    ca, sa, cat, sat = ca_ref[...], sa_ref[...], cat_ref[...], sat_ref[...]
    qscale = C_HEAD_DIM ** -0.5 * LOG2E
    nscale = D_HEAD_DIM ** -0.5 * LOG2E

    kc_all = yn[:, 0:LANES]
    kc_all = kc_all * lax.rsqrt(_group_mean_sq(kc_all, bd_ref[...]) + EPS) * g_kc_ref[...]
    q3 = yt[0:512].reshape(C_HEADS, C_HEAD_DIM, NB * ROW_TILE)
    q3 = q3 * lax.rsqrt(jnp.mean(q3 * q3, axis=1, keepdims=True) + EPS)
    qc_all = q3.reshape(512, NB * ROW_TILE) * g_qc_ref[...]
    for i in range(NB):
        rows = slice(i * ROW_TILE, (i + 1) * ROW_TILE)
        kc_ref[i, 0] = _rope(kc_all[rows], ca, sa, C_HEAD_DIM // 2).astype(_BF)
        vct_ref[i, 0] = yt[512:640, rows].astype(_BF)
        for pr in range(4):
            lo = pr * LANES
            qct_ref[i, pr] = (_rope_t(qc_all[lo:lo + LANES, rows], cat, sat, C_HEAD_DIM // 2) * qscale).astype(_BF)
            qnt_ref[i, pr] = (yt[640 + lo:640 + lo + LANES, rows] * nscale).astype(_BF)
            kn_ref[i, pr] = yn[rows, LANES + lo:2 * LANES + lo].astype(_BF)
            vnt_ref[i, pr] = yt[1152 + lo:1152 + lo + LANES, rows].astype(_BF)


def _project1(xs, mods, g_attn, w_in, tabs, tabs_t, bd, g_qc, g_kc):
    wn = jnp.concatenate([w_in[:, 512:640], w_in[:, 1280:1792]], axis=1)
    wt = jnp.concatenate([w_in[:, 0:512], w_in[:, 640:1280], w_in[:, 1792:2304]], axis=1).T
    outs = [_head_out(4, True), _head_out(1, False), _head_out(1, True),
            _head_out(4, True), _head_out(4, False), _head_out(4, True)]
    return pl.pallas_call(
        _proj1_body,
        grid=(BATCH // NB, N_TILES),
        in_specs=[_row_spec(D_MODEL), _mod_spec(), _const_spec((1, D_MODEL)),
                  _const_spec(wn.shape), _const_spec(wt.shape),
                  _table_spec(), _table_spec(), _table_t_spec(), _table_t_spec(),
                  _const_spec(bd.shape),
                  _const_spec((512, 1)), _const_spec((1, LANES))],
        out_specs=[o[1] for o in outs],
        out_shape=[o[0] for o in outs],
        compiler_params=pltpu.CompilerParams(vmem_limit_bytes=VMEM_LIMIT),
        name="project1",
    )(xs, mods, g_attn.reshape(1, -1), wn, wt, tabs[0], tabs[1], tabs_t[0], tabs_t[1], bd,
      jnp.tile(g_qc, 8).reshape(-1, 1), jnp.tile(g_kc, 2).reshape(1, -1))


def _split_q_cols(qt):
    z = jnp.zeros((HALF, qt.shape[1]), qt.dtype)
    return jnp.concatenate([jnp.concatenate([qt[:HALF], z], axis=0),
                            jnp.concatenate([z, qt[HALF:]], axis=0)], axis=1)


def _pipelined_tiles(load_q, score_chunk, value_chunk, finalize, n_chunks, bufs, score_group=1):
    def step(t_next, nxt, t_cur, cur):
        mx = None
        if t_next is not None:
            qb = load_q(t_next)
        if t_cur is not None:
            m_cur = cur[1][0:1, :]
            ls = jnp.zeros((1, Q2), _F32)
            acc = jnp.zeros((LANES, Q2), _F32)
        for c in range(n_chunks):
            rows = slice(c * KEY_CHUNK, (c + 1) * KEY_CHUNK)
            if t_next is not None and c % score_group == 0:
                n = min(score_group, n_chunks - c)
                s = score_chunk(t_next, qb, c, n)
                nxt[0][c * KEY_CHUNK:(c + n) * KEY_CHUNK, :] = s
                cm = jnp.max(s, axis=0, keepdims=True)
                mx = cm if mx is None else jnp.maximum(mx, cm)
            if t_cur is not None:
                e = jnp.exp2(cur[0][rows, :] - m_cur)
                ls = ls + jnp.sum(e, axis=0, keepdims=True)
                acc = acc + value_chunk(t_cur, c, e.astype(_BF))
        if t_next is not None:
            nxt[1][...] = jnp.broadcast_to(mx, nxt[1].shape)
        return (acc, ls) if t_cur is not None else None

    last = N_LAT_TILES - 1
    step(0, bufs[0], None, None)

    def pair(j, carry):
        finalize(jnp.maximum(2 * j - 1, 0), *carry)
        finalize(2 * j, *step(2 * j + 1, bufs[1], 2 * j, bufs[0]))
        return step(2 * j + 2, bufs[0], 2 * j + 1, bufs[1])

    carry = lax.fori_loop(0, N_LAT_TILES // 2 - 1, pair,
                          (jnp.zeros((LANES, Q2), _F32), jnp.ones((1, Q2), _F32)))
    finalize(last - 2, *carry)
    finalize(last - 1, *step(last, bufs[1], last - 1, bufs[0]))
    finalize(last, *step(None, None, last, bufs[1]))


def _dense_body(*refs, n_q, diff, ctx_tile, o_off):
    q_refs, k_refs, vt_ref = refs[:n_q], refs[n_q:2 * n_q], refs[2 * n_q]
    rest = refs[2 * n_q + 1:]
    if diff:
        lam_ref, g_ref, o_ref, s0, s1, m0, m1 = rest
        lv = lam_ref[...]
        lam = (jnp.exp(jnp.sum(lv[0:1] * lv[1:2], axis=-1, keepdims=True))
               - jnp.exp(jnp.sum(lv[2:3] * lv[3:4], axis=-1, keepdims=True)) + LAMBDA_INIT)
    else:
        o_ref, s0, s1, m0, m1 = rest

    def q_blocks(col0):
        if n_q == 1:
            return [_split_q_cols(q_refs[0][0, 0, :, pl.ds(col0, ROW_TILE)])]
        return [r[0, 0, :, pl.ds(col0, ROW_TILE)] for r in q_refs]

    def scores(qb, row0, rows):
        parts = [_dot(k_refs[i][0, 0, pl.ds(row0, rows), :], qb[i]) for i in range(n_q)]
        return parts[0] if n_q == 1 else jnp.concatenate(parts, axis=1)

    def finalize_rows(row0, acc, ls):
        ot = acc / ls
        if diff:
            o = (ot[:, :ROW_TILE] - lam * ot[:, ROW_TILE:]).T
            o = _rms(o) * g_ref[...] * (1.0 - LAMBDA_INIT)
        else:
            o = jnp.concatenate([ot[:HALF, :ROW_TILE], ot[HALF:, ROW_TILE:]], axis=0).T
        o_ref[0, pl.ds(row0, ROW_TILE), :] = o.astype(_BF)

    def load_q(t):
        return q_blocks(pl.multiple_of(CTX_LEN + t * ROW_TILE, ROW_TILE))

    def score_chunk(t, qb, c, n):
        return scores(qb, c * KEY_CHUNK, n * KEY_CHUNK)

    def value_chunk(t, c, e):
        return _dot(vt_ref[0, 0, :, c * KEY_CHUNK:(c + 1) * KEY_CHUNK], e)

    def finalize(t, acc, ls):
        finalize_rows(pl.multiple_of(o_off + t * ROW_TILE, ROW_TILE), acc, ls)

    _pipelined_tiles(load_q, score_chunk, value_chunk, finalize, N_KEY_CHUNKS, ((s0, m0), (s1, m1)),
                     score_group=2)

    if ctx_tile:
        s = scores(q_blocks(0), 0, CTX_LEN)
        e = jnp.exp2(s - jnp.max(s, axis=0, keepdims=True))
        finalize_rows(0, _dot(vt_ref[0, 0, :, 0:CTX_LEN], e.astype(_BF)),
                      jnp.sum(e, axis=0, keepdims=True))


def _na_body(qt_ref, k_ref, vt_ref, bias_ref, o_ref, s0, s1, m0, m1):
    def key_row0(t, c):
        first = jnp.clip(t - 1, 0, N_LAT_TILES - 3)
        return 0 if c == 0 else pl.multiple_of(CTX_LEN + (first + c - 1) * KEY_CHUNK, KEY_CHUNK)

    def load_q(t):
        return _split_q_cols(qt_ref[0, 0, :, pl.ds(pl.multiple_of(CTX_LEN + t * ROW_TILE, ROW_TILE), ROW_TILE)])

    def score_chunk(t, qb, c, n):
        s = _dot(k_ref[0, 0, pl.ds(key_row0(t, c), KEY_CHUNK), :], qb)
        if c > 0:
            kind = jnp.where(t == 0, 0, jnp.where(t == N_LAT_TILES - 1, 2, 1))
            s = s + bias_ref[0, kind, (c - 1) * KEY_CHUNK:c * KEY_CHUNK, :]
        return s

    def value_chunk(t, c, e):
        return _dot(vt_ref[0, 0, :, pl.ds(key_row0(t, c), KEY_CHUNK)], e)

    def finalize(t, acc, ls):
        ot = acc / ls
        o = jnp.concatenate([ot[:HALF, :ROW_TILE], ot[HALF:, ROW_TILE:]], axis=0).T
        o_ref[0, pl.ds(pl.multiple_of(t * ROW_TILE, ROW_TILE), ROW_TILE), :] = o.astype(_BF)

    _pipelined_tiles(load_q, score_chunk, value_chunk, finalize, 1 + NA_KEYS // KEY_CHUNK,
                     ((s0, m0), (s1, m1)))


def _attn_call(body, name, in_specs, arrays, out_rows, n_keys):
    return pl.pallas_call(
        body,
        grid=(BATCH, 4),
        in_specs=in_specs,
        out_specs=pl.BlockSpec((1, out_rows, LANES), lambda b, g: (b, 0, g)),
        out_shape=jax.ShapeDtypeStruct((BATCH, out_rows, 4 * LANES), _BF),
        scratch_shapes=[pltpu.VMEM((n_keys, Q2), _F32), pltpu.VMEM((n_keys, Q2), _F32),
                        pltpu.VMEM((8, Q2), _F32), pltpu.VMEM((8, Q2), _F32)],
        compiler_params=pltpu.CompilerParams(vmem_limit_bytes=VMEM_LIMIT),
        name=name,
    )(*arrays)


def _t_spec(f):
    return pl.BlockSpec((1, 1, LANES, S_ALL), f)


def _n_spec(f):
    return pl.BlockSpec((1, 1, S_ALL, LANES), f)


def _diff_attention(qdt, kd, vdt, lamv, g_subln):
    head = lambda b, g: (b, g, 0, 0)
    body = functools.partial(_dense_body, n_q=1, diff=True, ctx_tile=True, o_off=CTX_LEN)
    return _attn_call(body, "diff_attention",
                      [_t_spec(head), _n_spec(head), _t_spec(head),
                       pl.BlockSpec((4, A_HEAD_DIM), lambda b, g: (0, 0)),
                       pl.BlockSpec((1, LANES), lambda b, g: (0, 0))],
                      (qdt, kd, vdt, lamv, g_subln.reshape(1, -1)), S_ALL, S_ALL)


def _mla_attention(qmt, km, vmt):
    ha = lambda b, g: (b, 2 * g, 0, 0)
    hb = lambda b, g: (b, 2 * g + 1, 0, 0)
    body = functools.partial(_dense_body, n_q=2, diff=False, ctx_tile=True, o_off=CTX_LEN)
    return _attn_call(body, "mla_attention",
                      [_t_spec(ha), _t_spec(hb), _n_spec(ha), _n_spec(hb),
                       _t_spec(lambda b, g: (b, g, 0, 0))],
                      (qmt, qmt, km, km, vmt), S_ALL, S_ALL)


def _gqa_attention(qct, kc, vct):
    shared = lambda b, g: (b, 0, 0, 0)
    body = functools.partial(_dense_body, n_q=1, diff=False, ctx_tile=False, o_off=0)
    return _attn_call(body, "gqa_attention",
                      [_t_spec(lambda b, g: (b, g, 0, 0)), _n_spec(shared), _t_spec(shared)],
                      (qct, kc, vct), SEQ, S_ALL)


def _na_attention(qnt, kn, vnt, bias_t):
    head = lambda b, g: (b, g, 0, 0)
    return _attn_call(_na_body, "na_attention",
                      [_t_spec(head), _n_spec(head), _t_spec(head),
                       pl.BlockSpec((1, 3, NA_KEYS, Q2), lambda b, g: (g, 0, 0, 0))],
                      (qnt, kn, vnt, bias_t), SEQ, CTX_LEN + NA_KEYS)


def _merge_ffn_body(*refs, final):
    x = refs[0][...] if final else _stream_tile(refs[0], refs[1])
    (oa_ref, ob_ref, mod_ref, woa_ref, wob_ref, g_ref,
     w1_ref, w3_ref, w2_ref, gf_ref, o_ref) = refs[1 if final else 2:]
    m = mod_ref[...]
    stack = lambda r: r[...].reshape(NB * ROW_TILE, r.shape[-1])
    rows = lambda v, i: v[i * ROW_TILE:(i + 1) * ROW_TILE]
    attn = _dot(stack(oa_ref), woa_ref[...]) + _dot(stack(ob_ref), wob_ref[...])
    x1 = [x[i] + m[i, 2:3] * rows(attn, i) for i in range(NB)]
    h = _modulated_rows(x1, m, g_ref[...], 3, 4)
    a = _dot(h, w1_ref[...])
    b = _dot(h, w3_ref[...])
    u = (a * (1.0 / (1.0 + jnp.exp(-a))) * b).astype(_BF)
    f = _dot(u, w2_ref[...])
    for i in range(NB):
        x2 = x1[i] + m[i, 5:6] * rows(f, i)
        if final:
            x2 = _rms(x2) * gf_ref[...]
        o_ref[i] = x2


def _merge_ffn(streams, oa, ob, mods, woa, wob, g_ffn, w1, w3, w2, g_final, *, latent_only):
    off = 1 if latent_only else 0
    tiles = N_LAT_TILES if latent_only else N_TILES
    row = lambda w, o: pl.BlockSpec((NB, ROW_TILE, w), lambda b, t: (b, t + o, 0))
    mod = pl.BlockSpec((NB, 6, D_MODEL), lambda b, t: (_mod_row(b, t + off), 0, 0))
    x_specs = [row(D_MODEL, off)] if latent_only else _stream_specs()
    return pl.pallas_call(
        functools.partial(_merge_ffn_body, final=latent_only),
        grid=(BATCH // NB, tiles),
        in_specs=x_specs + [row(512, 0), row(512, 0), mod,
                  _const_spec(woa.shape), _const_spec(wob.shape), _const_spec((1, D_MODEL)),
                  _const_spec(w1.shape), _const_spec(w3.shape), _const_spec(w2.shape),
                  _const_spec((1, D_MODEL))],
        out_specs=row(D_MODEL, 0),
        out_shape=jax.ShapeDtypeStruct((BATCH, tiles * ROW_TILE, D_MODEL), _F32),
        compiler_params=pltpu.CompilerParams(vmem_limit_bytes=VMEM_LIMIT),
        name="merge_ffn_final" if latent_only else "merge_ffn",
    )(*streams, oa, ob, mods, woa, wob, g_ffn.reshape(1, -1), w1, w3, w2, g_final.reshape(1, -1))


def _rope_tables():
    t = np.arange(SEQ)
    rows = jnp.asarray(t // GRID_W, _F32)
    cols = jnp.asarray(t % GRID_W, _F32)

    def cs(rot_dim):
        axis_dim = rot_dim // 2
        inv = ROPE_THETA ** (-jnp.arange(0, axis_dim, 2, dtype=_F32) / axis_dim)
        ang = jnp.concatenate([rows[:, None] * inv, cols[:, None] * inv], axis=-1)
        return jnp.cos(ang), jnp.sin(ang)

    def with_ctx(tab, fill):
        return jnp.concatenate([jnp.full((CTX_LEN, LANES), fill, _F32), tab], axis=0)

    cos, sin = cs(A_HEAD_DIM)
    ca = with_ctx(jnp.tile(cos, (1, 4)), 1.0)
    sa = with_ctx(jnp.tile(jnp.concatenate([-sin, sin], axis=-1), (1, 2)), 0.0)
    cos, sin = cs(B_ROPE_DIM)
    ones = jnp.ones((SEQ, 64), _F32)
    zeros = jnp.zeros((SEQ, 64), _F32)
    cb = with_ctx(jnp.concatenate([ones, cos, cos, ones[:, :32]], axis=-1), 1.0)
    sb = with_ctx(jnp.concatenate([zeros, -sin, sin, zeros[:, :32]], axis=-1), 0.0)
    return ca, sa, cb, sb


def _na_bias_table(rpb):
    rows_n = SEQ // GRID_W
    span = 2 * GRID_W - 1
    left = GRID_W - NA_WIN_W
    p = jnp.pad(rpb * LOG2E, ((0, 0), (0, 0), (left, span - left - (2 * NA_WIN_W - 1))))
    flat = jnp.tile(p[:, :, ::-1], (1, 1, GRID_W + 1))[..., :GRID_W * (span + 1)]
    toe = flat.reshape(D_HEADS, 2 * NA_WIN_H - 1, GRID_W, span + 1)[:, :, ::-1, :GRID_W]
    kc = np.arange(GRID_W)[:, None]
    qc = np.arange(GRID_W)[None, :]
    cs = np.clip(qc - NA_WIN_W // 2, 0, GRID_W - NA_WIN_W)
    toe = jnp.where((kc >= cs) & (kc < cs + NA_WIN_W), toe, NEG_BIG)
    n_dr = 2 * NA_WIN_H - 1
    edge = ROWS_PER_TILE - 1
    n_quads = n_dr + edge

    def quads(t):
        fill = jnp.full((D_HEADS, edge, GRID_W, GRID_W), NEG_BIG, _F32)
        tp = jnp.concatenate([fill, t, fill], axis=1).reshape(4, 2, n_dr + 2 * edge, GRID_W, GRID_W)
        return jnp.concatenate([tp[:, w, edge - qr:edge - qr + n_quads]
                                for w in (0, 1) for qr in range(ROWS_PER_TILE)], axis=-1)

    dr = np.arange(n_dr)[None, :, None, None]
    lo = NA_WIN_H - 1 - NA_WIN_H // 2
    q_int = quads(jnp.where((dr >= lo) & (dr < lo + NA_WIN_H), toe, NEG_BIG))
    q_all = quads(toe)

    def tile(q, q_row0, k_row0):
        rs = [min(max(q_row0 + qr - NA_WIN_H // 2, 0), rows_n - NA_WIN_H) for qr in range(ROWS_PER_TILE)]
        parts = []
        for kr in range(NA_KEY_ROWS):
            inside = [r0 <= k_row0 + kr < r0 + NA_WIN_H for r0 in rs]
            d = k_row0 + kr - q_row0 + NA_WIN_H - 1
            assert q is q_int or all(inside) or not any(inside)
            if q is q_int or all(inside):
                parts.append(q[:, d])
            else:
                parts.append(jnp.full((4, GRID_W, Q2), NEG_BIG, _F32))
        return jnp.concatenate(parts, axis=1)

    return jnp.stack([tile(q_all, 0, 0),
                      tile(q_int, ROWS_PER_TILE, 0),
                      tile(q_all, rows_n - ROWS_PER_TILE, rows_n - NA_KEY_ROWS)], axis=1)


def _pad_cols(w, groups, width, total, offset=0):
    k = w.shape[0]
    w = w.reshape(k, groups, width)
    w = jnp.pad(w, ((0, 0), (0, 0), (offset, total - width - offset)))
    return w.reshape(k, groups * total)


def kernel(x, c, ctx, c_ctx, l0_w_mod, l0_b_mod, l0_g_attn, l0_w_in, l0_lam_q1, l0_lam_k1, l0_lam_q2, l0_lam_k2, l0_g_subln, l0_g_cq, l0_w_uq, l0_g_ckv, l0_w_ukv, l0_w_out, l0_g_ffn, l0_w1, l0_w3, l0_w2, l1_w_mod, l1_b_mod, l1_g_attn, l1_w_in, l1_g_qc, l1_g_kc, l1_rpb, l1_w_out, l1_g_ffn, l1_w1, l1_w3, l1_w2, g_final):
    bf = lambda w: w.astype(_BF)
    cond = jnp.concatenate([c, jnp.tile(c_ctx[None], (NB, 1)),
                            jnp.zeros((8 - BATCH - NB, D_MODEL), _F32)], axis=0)
    tabs = _rope_tables()
    tabs_t = [t.T for t in tabs]

    mods0 = _ada_params(cond, l0_w_mod, l0_b_mod)
    w_in0 = bf(jnp.concatenate([l0_w_in[:, :1920],
                                _pad_cols(l0_w_in[:, 1920:], 1, B_ROPE_DIM, LANES, B_NOPE_DIM)], axis=1))
    w_uq = bf(_pad_cols(l0_w_uq, B_HEADS, B_NOPE_DIM + B_ROPE_DIM, LANES))
    ukv = l0_w_ukv.reshape(B_KV_RANK, B_HEADS, B_NOPE_DIM + B_V_DIM)
    w_ukv = bf(jnp.concatenate([_pad_cols(ukv[:, :, :B_NOPE_DIM].reshape(B_KV_RANK, -1), B_HEADS, B_NOPE_DIM, LANES),
                                ukv[:, :, B_NOPE_DIM:].reshape(B_KV_RANK, -1)], axis=1))
    qdt, kd, vdt, qmt, km, vmt = _project0(x, ctx, mods0, l0_g_attn, w_in0, tabs, tabs_t,
                                           l0_g_cq, w_uq, l0_g_ckv, w_ukv)
    lamv = jnp.stack([l0_lam_q1, l0_lam_k1, l0_lam_q2, l0_lam_k2])
    o_a = _diff_attention(qdt, kd, vdt, lamv, l0_g_subln)
    o_b = _mla_attention(qmt, km, vmt)
    w_out0 = bf(l0_w_out)
    xs = _merge_ffn((x, ctx), o_a, o_b, mods0, w_out0[:512], w_out0[512:], l0_g_ffn,
                    bf(l0_w1), bf(l0_w3), bf(l0_w2), g_final, latent_only=False)

    mods1 = _ada_params(cond, l1_w_mod, l1_b_mod)
    wq = l1_w_in[:, :512].reshape(D_MODEL, 2, 4, C_HEAD_DIM).transpose(0, 2, 1, 3).reshape(D_MODEL, 512)
    w_in1 = bf(jnp.concatenate([wq, l1_w_in[:, 512:]], axis=1))
    grp = np.arange(LANES) // C_HEAD_DIM
    bd = jnp.asarray((grp[:, None] == grp[None, :]) / C_HEAD_DIM, _BF)
    qct, kc, vct, qnt, kn, vnt = _project1(xs, mods1, l1_g_attn, w_in1, tabs, tabs_t, bd, l1_g_qc, l1_g_kc)
    o_c = _gqa_attention(qct, kc, vct)
    o_d = _na_attention(qnt, kn, vnt, _na_bias_table(l1_rpb))
    w_out1 = bf(l1_w_out)
    woc = w_out1[:512].reshape(2, 4, C_HEAD_DIM, D_MODEL).transpose(1, 0, 2, 3).reshape(512, D_MODEL)
    return _merge_ffn((xs,), o_c, o_d, mods1, woc, w_out1[512:], l1_g_ffn,
                      bf(l1_w1), bf(l1_w3), bf(l1_w2), g_final, latent_only=True)
```

```python
import functools
import math

import numpy as np
import jax
import jax.numpy as jnp
from jax import lax
from jax.experimental import pallas as pl
from jax.experimental.pallas import tpu as pltpu

D_MODEL = 1024
BATCH = 4
SEQ = 4096
GRID_W = 64
CTX_LEN = 256
ROPE_THETA = 10000.0
EPS = 1e-6
A_HEADS = 4
A_HEAD_DIM = 64
B_HEADS = 8
B_Q_RANK = 256
B_KV_RANK = 128
B_NOPE_DIM = 64
B_ROPE_DIM = 32
B_V_DIM = 64
C_HEADS = 8
C_KV_HEADS = 2
C_HEAD_DIM = 64
D_HEADS = 8
D_HEAD_DIM = 64
NA_WIN_H = 8
NA_WIN_W = 16
FFN_HIDDEN = 2816
LAMBDA_INIT = 0.8 - 0.6 * math.exp(-0.3 * 0)

S_ALL = CTX_LEN + SEQ
LANES = 128
HALF = LANES // 2
ROW_TILE = 256
NB = 2
N_TILES = S_ALL // ROW_TILE
N_LAT_TILES = SEQ // ROW_TILE
Q2 = 2 * ROW_TILE
KEY_CHUNK = 256
N_KEY_CHUNKS = S_ALL // KEY_CHUNK
ROWS_PER_TILE = ROW_TILE // GRID_W
NA_KEY_ROWS = 12
NA_KEYS = NA_KEY_ROWS * GRID_W
LOG2E = math.log2(math.e)
NEG_BIG = -1e30
VMEM_LIMIT = 52 * 1024 * 1024

assert ROW_TILE == CTX_LEN and S_ALL % ROW_TILE == 0 and N_LAT_TILES % 2 == 0

_BF = jnp.bfloat16
_F32 = jnp.float32


def _dot(a, b):
    return jnp.dot(a, b, preferred_element_type=_F32)


def _rms(x):
    return x * lax.rsqrt(jnp.mean(x * x, axis=-1, keepdims=True) + EPS)


def _lane(shape):
    return lax.broadcasted_iota(jnp.int32, shape, len(shape) - 1)


def _swap_halves(x, half):
    n = x.shape[-1]
    up = pltpu.roll(x, n - half, axis=1)
    down = pltpu.roll(x, half, axis=1)
    return jnp.where((_lane(x.shape) % (2 * half)) < half, up, down)


def _rope(x, c, s, half):
    return x * c + _swap_halves(x, half) * s


def _rope_t(x, c, s, half):
    parts = []
    for r in range(0, x.shape[0], 2 * half):
        parts += [x[r + half:r + 2 * half], x[r:r + half]]
    return x * c + jnp.concatenate(parts, axis=0) * s


def _dot_nt(a, b):
    return lax.dot_general(a, b, (((1,), (1,)), ((), ())), preferred_element_type=_F32)


def _group_mean_sq(x, bd):
    x2 = x * x
    hi = x2.astype(_BF)
    lo = (x2 - hi.astype(_F32)).astype(_BF)
    return _dot(hi, bd) + _dot(lo, bd)


def _ada_body(cond_ref, w_ref, b_ref, o_ref):
    c = cond_ref[...]
    a = (c * (1.0 / (1.0 + jnp.exp(-c)))).astype(_BF)
    o_ref[...] = _dot(a, w_ref[...].astype(_BF)) + b_ref[...]


def _ada_params(cond, w_mod, b_mod):
    n = w_mod.shape[1]
    tn = n // 4
    out = pl.pallas_call(
        _ada_body,
        grid=(n // tn,),
        in_specs=[pl.BlockSpec((8, D_MODEL), lambda j: (0, 0)),
                  pl.BlockSpec((D_MODEL, tn), lambda j: (0, j)),
                  pl.BlockSpec((1, tn), lambda j: (0, j))],
        out_specs=pl.BlockSpec((8, tn), lambda j: (0, j)),
        out_shape=jax.ShapeDtypeStruct((8, n), _F32),
        compiler_params=pltpu.CompilerParams(vmem_limit_bytes=VMEM_LIMIT),
        name="ada_params",
    )(cond, w_mod, b_mod.reshape(1, n))
    return out.reshape(8, 6, D_MODEL)


def _mod_row(b, t):
    return jnp.where(t == 0, BATCH // NB, b)


def _const_spec(shape):
    nd = len(shape)
    return pl.BlockSpec(shape, lambda *_: (0,) * nd, pipeline_mode=pl.Buffered(1))


def _modulated(x, m, g, shift_row, scale_row):
    h = _rms(x) * g
    return h * (1.0 + m[scale_row:scale_row + 1]) + m[shift_row:shift_row + 1]


def _modulated_rows(xs, m, g, shift_row, scale_row):
    return jnp.concatenate([_modulated(xs[i], m[i], g, shift_row, scale_row) for i in range(NB)],
                           axis=0).astype(_BF)


def _stream_tile(x_ref, ctx_ref):
    return jnp.where(pl.program_id(1) == 0, ctx_ref[...], x_ref[...])


def _stream_specs():
    return [pl.BlockSpec((NB, ROW_TILE, D_MODEL), lambda b, t: (b, jnp.maximum(t - 1, 0), 0)),
            pl.BlockSpec((NB, CTX_LEN, D_MODEL), lambda b, t: (b, 0, 0))]


def _proj0_body(x_ref, ctx_ref, mod_ref, g_ref, wn_ref, wt_ref,
                ca_ref, sa_ref, cb_ref, sb_ref, cat_ref, sat_ref, cbt_ref, sbt_ref,
                g_cq_ref, w_uqt_ref, g_ckv_ref, w_uk_ref, w_uvt_ref,
                qdt_ref, kd_ref, vdt_ref, qmt_ref, km_ref, vmt_ref):
    h = _modulated_rows(_stream_tile(x_ref, ctx_ref), mod_ref[...], g_ref[...], 0, 1)
    yn = _dot(h, wn_ref[...])
    yt = _dot_nt(wt_ref[...], h)
    cq = (_rms(yn[:, 512:768]) * g_cq_ref[...]).astype(_BF)
    ckv = (_rms(yn[:, 768:896]) * g_ckv_ref[...]).astype(_BF)
    qmt = _dot_nt(w_uqt_ref[...], cq)
    kn = _dot(ckv, w_uk_ref[...])
    vmt = _dot_nt(w_uvt_ref[...], ckv)
    ca, sa, cb, sb = ca_ref[...], sa_ref[...], cb_ref[...], sb_ref[...]
    cat, sat, cbt, sbt = cat_ref[...], sat_ref[...], cbt_ref[...], sbt_ref[...]
    qscale = A_HEAD_DIM ** -0.5 * LOG2E
    mscale = (B_NOPE_DIM + B_ROPE_DIM) ** -0.5 * LOG2E
    for i in range(NB):
        rows = slice(i * ROW_TILE, (i + 1) * ROW_TILE)
        for hd in range(A_HEADS):
            lo = hd * LANES
            qdt_ref[i, hd] = (_rope_t(yt[lo:lo + LANES, rows], cat, sat, A_HEAD_DIM // 2) * qscale).astype(_BF)
            kd_ref[i, hd] = _rope(yn[rows, lo:lo + LANES], ca, sa, A_HEAD_DIM // 2).astype(_BF)
            vdt_ref[i, hd] = yt[512 + lo:512 + lo + LANES, rows].astype(_BF)
        kr = _rope(yn[rows, 896:1024], cb, sb, B_ROPE_DIM // 2)
        for hd in range(B_HEADS):
            lo = hd * LANES
            qmt_ref[i, hd] = (_rope_t(qmt[lo:lo + LANES, rows], cbt, sbt, B_ROPE_DIM // 2) * mscale).astype(_BF)
            km_ref[i, hd] = (kn[rows, lo:lo + LANES] + kr).astype(_BF)
        for pr in range(B_HEADS // 2):
            vmt_ref[i, pr] = vmt[pr * LANES:(pr + 1) * LANES, rows].astype(_BF)


def _head_out(n_heads, transposed):
    if transposed:
        return (jax.ShapeDtypeStruct((BATCH, n_heads, LANES, S_ALL), _BF),
                pl.BlockSpec((NB, n_heads, LANES, ROW_TILE), lambda b, t: (b, 0, 0, t)))
    return (jax.ShapeDtypeStruct((BATCH, n_heads, S_ALL, LANES), _BF),
            pl.BlockSpec((NB, n_heads, ROW_TILE, LANES), lambda b, t: (b, 0, t, 0)))


def _row_spec(width):
    return pl.BlockSpec((NB, ROW_TILE, width), lambda b, t: (b, t, 0))


def _mod_spec():
    return pl.BlockSpec((NB, 6, D_MODEL), lambda b, t: (_mod_row(b, t), 0, 0))


def _table_spec():
    return pl.BlockSpec((ROW_TILE, LANES), lambda b, t: (t, 0))


def _table_t_spec():
    return pl.BlockSpec((LANES, ROW_TILE), lambda b, t: (0, t))


def _project0(x, ctx, mods, g_attn, w_in, tabs, tabs_t, g_cq, w_uq, g_ckv, w_ukv):
    wn = jnp.concatenate([w_in[:, 512:1024], w_in[:, 1536:2048]], axis=1)
    wt = jnp.concatenate([w_in[:, 0:512], w_in[:, 1024:1536]], axis=1).T
    w_uqt = w_uq.T
    w_uk, w_uvt = w_ukv[:, :B_HEADS * LANES], w_ukv[:, B_HEADS * LANES:].T
    outs = [_head_out(4, True), _head_out(4, False), _head_out(4, True),
            _head_out(8, True), _head_out(8, False), _head_out(4, True)]
    return pl.pallas_call(
        _proj0_body,
        grid=(BATCH // NB, N_TILES),
        in_specs=_stream_specs() + [_mod_spec(), _const_spec((1, D_MODEL)),
                  _const_spec(wn.shape), _const_spec(wt.shape)]
                 + [_table_spec()] * 4 + [_table_t_spec()] * 4
                 + [_const_spec((1, B_Q_RANK)), _const_spec(w_uqt.shape),
                    _const_spec((1, B_KV_RANK)), _const_spec(w_uk.shape), _const_spec(w_uvt.shape)],
        out_specs=[o[1] for o in outs],
        out_shape=[o[0] for o in outs],
        compiler_params=pltpu.CompilerParams(vmem_limit_bytes=VMEM_LIMIT),
        name="project0",
    )(x, ctx, mods, g_attn.reshape(1, -1), wn, wt, *tabs, *tabs_t,
      g_cq.reshape(1, -1), w_uqt, g_ckv.reshape(1, -1), w_uk, w_uvt)


def _proj1_body(x_ref, mod_ref, g_ref, wn_ref, wt_ref, ca_ref, sa_ref, cat_ref, sat_ref, bd_ref,
                g_qc_ref, g_kc_ref,
                qct_ref, kc_ref, vct_ref, qnt_ref, kn_ref, vnt_ref):
    h = _modulated_rows(x_ref[...], mod_ref[...], g_ref[...], 0, 1)
    yn = _dot(h, wn_ref[...])
    yt = _dot_nt(wt_ref[...], h)
    ca, sa, cat, sat = ca_ref[...], sa_ref[...], cat_ref[...], sat_ref[...]
    qscale = C_HEAD_DIM ** -0.5 * LOG2E
    nscale = D_HEAD_DIM ** -0.5 * LOG2E

    kc_all = yn[:, 0:LANES]
    kc_all = kc_all * lax.rsqrt(_group_mean_sq(kc_all, bd_ref[...]) + EPS) * g_kc_ref[...]
    q3 = yt[0:512].reshape(C_HEADS, C_HEAD_DIM, NB * ROW_TILE)
    q3 = q3 * lax.rsqrt(jnp.mean(q3 * q3, axis=1, keepdims=True) + EPS)
    qc_all = q3.reshape(512, NB * ROW_TILE) * g_qc_ref[...]
    for i in range(NB):
        rows = slice(i * ROW_TILE, (i + 1) * ROW_TILE)
        kc_ref[i, 0] = _rope(kc_all[rows], ca, sa, C_HEAD_DIM // 2).astype(_BF)
        vct_ref[i, 0] = yt[512:640, rows].astype(_BF)
        for pr in range(4):
            lo = pr * LANES
            qct_ref[i, pr] = (_rope_t(qc_all[lo:lo + LANES, rows], cat, sat, C_HEAD_DIM // 2) * qscale).astype(_BF)
            qnt_ref[i, pr] = (yt[640 + lo:640 + lo + LANES, rows] * nscale).astype(_BF)
            kn_ref[i, pr] = yn[rows, LANES + lo:2 * LANES + lo].astype(_BF)
            vnt_ref[i, pr] = yt[1152 + lo:1152 + lo + LANES, rows].astype(_BF)


def _project1(xs, mods, g_attn, w_in, tabs, tabs_t, bd, g_qc, g_kc):
    wn = jnp.concatenate([w_in[:, 512:640], w_in[:, 1280:1792]], axis=1)
    wt = jnp.concatenate([w_in[:, 0:512], w_in[:, 640:1280], w_in[:, 1792:2304]], axis=1).T
    outs = [_head_out(4, True), _head_out(1, False), _head_out(1, True),
            _head_out(4, True), _head_out(4, False), _head_out(4, True)]
    return pl.pallas_call(
        _proj1_body,
        grid=(BATCH // NB, N_TILES),
        in_specs=[_row_spec(D_MODEL), _mod_spec(), _const_spec((1, D_MODEL)),
                  _const_spec(wn.shape), _const_spec(wt.shape),
                  _table_spec(), _table_spec(), _table_t_spec(), _table_t_spec(),
                  _const_spec(bd.shape),
                  _const_spec((512, 1)), _const_spec((1, LANES))],
        out_specs=[o[1] for o in outs],
        out_shape=[o[0] for o in outs],
        compiler_params=pltpu.CompilerParams(vmem_limit_bytes=VMEM_LIMIT),
        name="project1",
    )(xs, mods, g_attn.reshape(1, -1), wn, wt, tabs[0], tabs[1], tabs_t[0], tabs_t[1], bd,
      jnp.tile(g_qc, 8).reshape(-1, 1), jnp.tile(g_kc, 2).reshape(1, -1))


def _split_q_cols(qt):
    z = jnp.zeros((HALF, qt.shape[1]), qt.dtype)
    return jnp.concatenate([jnp.concatenate([qt[:HALF], z], axis=0),
                            jnp.concatenate([z, qt[HALF:]], axis=0)], axis=1)


def _pv_t(vt, e, split):
    if not split:
        return _dot(vt, e)
    return jnp.concatenate([_dot(vt[:HALF], e[:, :ROW_TILE]), _dot(vt[HALF:], e[:, ROW_TILE:])], axis=0)


def _normalise_t(acc, ls, split):
    if not split:
        return acc / ls
    return jnp.concatenate([acc[:HALF] / ls[:, :ROW_TILE], acc[HALF:] / ls[:, ROW_TILE:]], axis=0)


def _pipelined_tiles(load_q, score_chunk, value_chunk, finalize, n_chunks, bufs, acc_cols, score_group=1):
    def step(t_next, nxt, t_cur, cur):
        mx = None
        if t_next is not None:
            qb = load_q(t_next)
        if t_cur is not None:
            m_cur = cur[1][0:1, :]
            ls = jnp.zeros((1, Q2), _F32)
            acc = jnp.zeros((LANES, acc_cols), _F32)
        for c in range(n_chunks):
            rows = slice(c * KEY_CHUNK, (c + 1) * KEY_CHUNK)
            if t_next is not None and c % score_group == 0:
                n = min(score_group, n_chunks - c)
                s = score_chunk(t_next, qb, c, n)
                nxt[0][c * KEY_CHUNK:(c + n) * KEY_CHUNK, :] = s
                cm = jnp.max(s, axis=0, keepdims=True)
                mx = cm if mx is None else jnp.maximum(mx, cm)
            if t_cur is not None:
                e = jnp.exp2(cur[0][rows, :] - m_cur)
                ls = ls + jnp.sum(e, axis=0, keepdims=True)
                acc = acc + value_chunk(t_cur, c, e.astype(_BF))
        if t_next is not None:
            nxt[1][...] = jnp.broadcast_to(mx, nxt[1].shape)
        return (acc, ls) if t_cur is not None else None

    last = N_LAT_TILES - 1
    step(0, bufs[0], None, None)

    def pair(j, carry):
        finalize(jnp.maximum(2 * j - 1, 0), *carry)
        finalize(2 * j, *step(2 * j + 1, bufs[1], 2 * j, bufs[0]))
        return step(2 * j + 2, bufs[0], 2 * j + 1, bufs[1])

    carry = lax.fori_loop(0, N_LAT_TILES // 2 - 1, pair,
                          (jnp.zeros((LANES, acc_cols), _F32), jnp.ones((1, Q2), _F32)))
    finalize(last - 2, *carry)
    finalize(last - 1, *step(last, bufs[1], last - 1, bufs[0]))
    finalize(last, *step(None, None, last, bufs[1]))


def _dense_body(*refs, n_q, diff, ctx_tile, o_off):
    q_refs, k_refs, vt_ref = refs[:n_q], refs[n_q:2 * n_q], refs[2 * n_q]
    rest = refs[2 * n_q + 1:]
    if diff:
        lam_ref, g_ref, o_ref, s0, s1, m0, m1 = rest
        lv = lam_ref[...]
        lam = (jnp.exp(jnp.sum(lv[0:1] * lv[1:2], axis=-1, keepdims=True))
               - jnp.exp(jnp.sum(lv[2:3] * lv[3:4], axis=-1, keepdims=True)) + LAMBDA_INIT)
    else:
        o_ref, s0, s1, m0, m1 = rest

    def q_blocks(col0):
        if n_q == 1:
            return [_split_q_cols(q_refs[0][0, 0, :, pl.ds(col0, ROW_TILE)])]
        return [r[0, 0, :, pl.ds(col0, ROW_TILE)] for r in q_refs]

    def scores(qb, row0, rows):
        parts = [_dot(k_refs[i][0, 0, pl.ds(row0, rows), :], qb[i]) for i in range(n_q)]
        return parts[0] if n_q == 1 else jnp.concatenate(parts, axis=1)

    def finalize_rows(row0, acc, ls):
        ot = _normalise_t(acc, ls, not diff)
        if diff:
            o = (ot[:, :ROW_TILE] - lam * ot[:, ROW_TILE:]).T
            o = _rms(o) * g_ref[...] * (1.0 - LAMBDA_INIT)
        else:
            o = ot.T
        o_ref[0, pl.ds(row0, ROW_TILE), :] = o.astype(_BF)

    def load_q(t):
        return q_blocks(pl.multiple_of(CTX_LEN + t * ROW_TILE, ROW_TILE))

    def score_chunk(t, qb, c, n):
        return scores(qb, c * KEY_CHUNK, n * KEY_CHUNK)

    def value_chunk(t, c, e):
        return _pv_t(vt_ref[0, 0, :, c * KEY_CHUNK:(c + 1) * KEY_CHUNK], e, not diff)

    def finalize(t, acc, ls):
        finalize_rows(pl.multiple_of(o_off + t * ROW_TILE, ROW_TILE), acc, ls)

    _pipelined_tiles(load_q, score_chunk, value_chunk, finalize, N_KEY_CHUNKS, ((s0, m0), (s1, m1)),
                     Q2 if diff else ROW_TILE)

    if ctx_tile:
        s = scores(q_blocks(0), 0, CTX_LEN)
        e = jnp.exp2(s - jnp.max(s, axis=0, keepdims=True))
        finalize_rows(0, _pv_t(vt_ref[0, 0, :, 0:CTX_LEN], e.astype(_BF), not diff),
                      jnp.sum(e, axis=0, keepdims=True))


def _na_body(qt_ref, k_ref, vt_ref, bias_ref, o_ref, s0, s1, m0, m1):
    def key_row0(t, c):
        first = jnp.clip(t - 1, 0, N_LAT_TILES - 3)
        return 0 if c == 0 else pl.multiple_of(CTX_LEN + (first + c - 1) * KEY_CHUNK, KEY_CHUNK)

    def load_q(t):
        return _split_q_cols(qt_ref[0, 0, :, pl.ds(pl.multiple_of(CTX_LEN + t * ROW_TILE, ROW_TILE), ROW_TILE)])

    def score_chunk(t, qb, c, n):
        s = _dot(k_ref[0, 0, pl.ds(key_row0(t, c), KEY_CHUNK), :], qb)
        if c > 0:
            kind = jnp.where(t == 0, 0, jnp.where(t == N_LAT_TILES - 1, 2, 1))
            s = s + bias_ref[0, kind, (c - 1) * KEY_CHUNK:c * KEY_CHUNK, :]
        return s

    def value_chunk(t, c, e):
        return _pv_t(vt_ref[0, 0, :, pl.ds(key_row0(t, c), KEY_CHUNK)], e, True)

    def finalize(t, acc, ls):
        o = _normalise_t(acc, ls, True).T
        o_ref[0, pl.ds(pl.multiple_of(t * ROW_TILE, ROW_TILE), ROW_TILE), :] = o.astype(_BF)

    _pipelined_tiles(load_q, score_chunk, value_chunk, finalize, 1 + NA_KEYS // KEY_CHUNK,
                     ((s0, m0), (s1, m1)), ROW_TILE)


def _attn_call(body, name, in_specs, arrays, out_rows, n_keys):
    return pl.pallas_call(
        body,
        grid=(BATCH, 4),
        in_specs=in_specs,
        out_specs=pl.BlockSpec((1, out_rows, LANES), lambda b, g: (b, 0, g)),
        out_shape=jax.ShapeDtypeStruct((BATCH, out_rows, 4 * LANES), _BF),
        scratch_shapes=[pltpu.VMEM((n_keys, Q2), _F32), pltpu.VMEM((n_keys, Q2), _F32),
                        pltpu.VMEM((8, Q2), _F32), pltpu.VMEM((8, Q2), _F32)],
        compiler_params=pltpu.CompilerParams(vmem_limit_bytes=VMEM_LIMIT),
        name=name,
    )(*arrays)


def _t_spec(f):
    return pl.BlockSpec((1, 1, LANES, S_ALL), f)


def _n_spec(f):
    return pl.BlockSpec((1, 1, S_ALL, LANES), f)


def _diff_attention(qdt, kd, vdt, lamv, g_subln):
    head = lambda b, g: (b, g, 0, 0)
    body = functools.partial(_dense_body, n_q=1, diff=True, ctx_tile=True, o_off=CTX_LEN)
    return _attn_call(body, "diff_attention",
                      [_t_spec(head), _n_spec(head), _t_spec(head),
                       pl.BlockSpec((4, A_HEAD_DIM), lambda b, g: (0, 0)),
                       pl.BlockSpec((1, LANES), lambda b, g: (0, 0))],
                      (qdt, kd, vdt, lamv, g_subln.reshape(1, -1)), S_ALL, S_ALL)


def _mla_attention(qmt, km, vmt):
    ha = lambda b, g: (b, 2 * g, 0, 0)
    hb = lambda b, g: (b, 2 * g + 1, 0, 0)
    body = functools.partial(_dense_body, n_q=2, diff=False, ctx_tile=True, o_off=CTX_LEN)
    return _attn_call(body, "mla_attention",
                      [_t_spec(ha), _t_spec(hb), _n_spec(ha), _n_spec(hb),
                       _t_spec(lambda b, g: (b, g, 0, 0))],
                      (qmt, qmt, km, km, vmt), S_ALL, S_ALL)


def _gqa_attention(qct, kc, vct):
    shared = lambda b, g: (b, 0, 0, 0)
    body = functools.partial(_dense_body, n_q=1, diff=False, ctx_tile=False, o_off=0)
    return _attn_call(body, "gqa_attention",
                      [_t_spec(lambda b, g: (b, g, 0, 0)), _n_spec(shared), _t_spec(shared)],
                      (qct, kc, vct), SEQ, S_ALL)


def _na_attention(qnt, kn, vnt, bias_t):
    head = lambda b, g: (b, g, 0, 0)
    return _attn_call(_na_body, "na_attention",
                      [_t_spec(head), _n_spec(head), _t_spec(head),
                       pl.BlockSpec((1, 3, NA_KEYS, Q2), lambda b, g: (g, 0, 0, 0))],
                      (qnt, kn, vnt, bias_t), SEQ, CTX_LEN + NA_KEYS)


def _merge_ffn_body(*refs, final):
    x = refs[0][...] if final else _stream_tile(refs[0], refs[1])
    (oa_ref, ob_ref, mod_ref, woa_ref, wob_ref, g_ref,
     w1_ref, w3_ref, w2_ref, gf_ref, o_ref) = refs[1 if final else 2:]
    m = mod_ref[...]
    stack = lambda r: r[...].reshape(NB * ROW_TILE, r.shape[-1])
    rows = lambda v, i: v[i * ROW_TILE:(i + 1) * ROW_TILE]
    attn = _dot(stack(oa_ref), woa_ref[...]) + _dot(stack(ob_ref), wob_ref[...])
    x1 = [x[i] + m[i, 2:3] * rows(attn, i) for i in range(NB)]
    h = _modulated_rows(x1, m, g_ref[...], 3, 4)
    a = _dot(h, w1_ref[...])
    b = _dot(h, w3_ref[...])
    u = (a * (1.0 / (1.0 + jnp.exp(-a))) * b).astype(_BF)
    f = _dot(u, w2_ref[...])
    for i in range(NB):
        x2 = x1[i] + m[i, 5:6] * rows(f, i)
        if final:
            x2 = _rms(x2) * gf_ref[...]
        o_ref[i] = x2


def _merge_ffn(streams, oa, ob, mods, woa, wob, g_ffn, w1, w3, w2, g_final, *, latent_only):
    off = 1 if latent_only else 0
    tiles = N_LAT_TILES if latent_only else N_TILES
    row = lambda w, o: pl.BlockSpec((NB, ROW_TILE, w), lambda b, t: (b, t + o, 0))
    mod = pl.BlockSpec((NB, 6, D_MODEL), lambda b, t: (_mod_row(b, t + off), 0, 0))
    x_specs = [row(D_MODEL, off)] if latent_only else _stream_specs()
    return pl.pallas_call(
        functools.partial(_merge_ffn_body, final=latent_only),
        grid=(BATCH // NB, tiles),
        in_specs=x_specs + [row(512, 0), row(512, 0), mod,
                  _const_spec(woa.shape), _const_spec(wob.shape), _const_spec((1, D_MODEL)),
                  _const_spec(w1.shape), _const_spec(w3.shape), _const_spec(w2.shape),
                  _const_spec((1, D_MODEL))],
        out_specs=row(D_MODEL, 0),
        out_shape=jax.ShapeDtypeStruct((BATCH, tiles * ROW_TILE, D_MODEL), _F32),
        compiler_params=pltpu.CompilerParams(vmem_limit_bytes=VMEM_LIMIT),
        name="merge_ffn_final" if latent_only else "merge_ffn",
    )(*streams, oa, ob, mods, woa, wob, g_ffn.reshape(1, -1), w1, w3, w2, g_final.reshape(1, -1))


def _rope_tables():
    t = np.arange(SEQ)
    rows = jnp.asarray(t // GRID_W, _F32)
    cols = jnp.asarray(t % GRID_W, _F32)

    def cs(rot_dim):
        axis_dim = rot_dim // 2
        inv = ROPE_THETA ** (-jnp.arange(0, axis_dim, 2, dtype=_F32) / axis_dim)
        ang = jnp.concatenate([rows[:, None] * inv, cols[:, None] * inv], axis=-1)
        return jnp.cos(ang), jnp.sin(ang)

    def with_ctx(tab, fill):
        return jnp.concatenate([jnp.full((CTX_LEN, LANES), fill, _F32), tab], axis=0)

    cos, sin = cs(A_HEAD_DIM)
    ca = with_ctx(jnp.tile(cos, (1, 4)), 1.0)
    sa = with_ctx(jnp.tile(jnp.concatenate([-sin, sin], axis=-1), (1, 2)), 0.0)
    cos, sin = cs(B_ROPE_DIM)
    ones = jnp.ones((SEQ, 64), _F32)
    zeros = jnp.zeros((SEQ, 64), _F32)
    cb = with_ctx(jnp.concatenate([ones, cos, cos, ones[:, :32]], axis=-1), 1.0)
    sb = with_ctx(jnp.concatenate([zeros, -sin, sin, zeros[:, :32]], axis=-1), 0.0)
    return ca, sa, cb, sb


def _na_bias_table(rpb):
    rows_n = SEQ // GRID_W
    span = 2 * GRID_W - 1
    left = GRID_W - NA_WIN_W
    p = jnp.pad(rpb * LOG2E, ((0, 0), (0, 0), (left, span - left - (2 * NA_WIN_W - 1))))
    flat = jnp.tile(p[:, :, ::-1], (1, 1, GRID_W + 1))[..., :GRID_W * (span + 1)]
    toe = flat.reshape(D_HEADS, 2 * NA_WIN_H - 1, GRID_W, span + 1)[:, :, ::-1, :GRID_W]
    kc = np.arange(GRID_W)[:, None]
    qc = np.arange(GRID_W)[None, :]
    cs = np.clip(qc - NA_WIN_W // 2, 0, GRID_W - NA_WIN_W)
    toe = jnp.where((kc >= cs) & (kc < cs + NA_WIN_W), toe, NEG_BIG)
    n_dr = 2 * NA_WIN_H - 1
    edge = ROWS_PER_TILE - 1
    n_quads = n_dr + edge

    def quads(t):
        fill = jnp.full((D_HEADS, edge, GRID_W, GRID_W), NEG_BIG, _F32)
        tp = jnp.concatenate([fill, t, fill], axis=1).reshape(4, 2, n_dr + 2 * edge, GRID_W, GRID_W)
        return jnp.concatenate([tp[:, w, edge - qr:edge - qr + n_quads]
                                for w in (0, 1) for qr in range(ROWS_PER_TILE)], axis=-1)

    dr = np.arange(n_dr)[None, :, None, None]
    lo = NA_WIN_H - 1 - NA_WIN_H // 2
    q_int = quads(jnp.where((dr >= lo) & (dr < lo + NA_WIN_H), toe, NEG_BIG))
    q_all = quads(toe)

    def tile(q, q_row0, k_row0):
        rs = [min(max(q_row0 + qr - NA_WIN_H // 2, 0), rows_n - NA_WIN_H) for qr in range(ROWS_PER_TILE)]
        parts = []
        for kr in range(NA_KEY_ROWS):
            inside = [r0 <= k_row0 + kr < r0 + NA_WIN_H for r0 in rs]
            d = k_row0 + kr - q_row0 + NA_WIN_H - 1
            assert q is q_int or all(inside) or not any(inside)
            if q is q_int or all(inside):
                parts.append(q[:, d])
            else:
                parts.append(jnp.full((4, GRID_W, Q2), NEG_BIG, _F32))
        return jnp.concatenate(parts, axis=1)

    return jnp.stack([tile(q_all, 0, 0),
                      tile(q_int, ROWS_PER_TILE, 0),
                      tile(q_all, rows_n - ROWS_PER_TILE, rows_n - NA_KEY_ROWS)], axis=1)


def _pad_cols(w, groups, width, total, offset=0):
    k = w.shape[0]
    w = w.reshape(k, groups, width)
    w = jnp.pad(w, ((0, 0), (0, 0), (offset, total - width - offset)))
    return w.reshape(k, groups * total)


def kernel(x, c, ctx, c_ctx, l0_w_mod, l0_b_mod, l0_g_attn, l0_w_in, l0_lam_q1, l0_lam_k1, l0_lam_q2, l0_lam_k2, l0_g_subln, l0_g_cq, l0_w_uq, l0_g_ckv, l0_w_ukv, l0_w_out, l0_g_ffn, l0_w1, l0_w3, l0_w2, l1_w_mod, l1_b_mod, l1_g_attn, l1_w_in, l1_g_qc, l1_g_kc, l1_rpb, l1_w_out, l1_g_ffn, l1_w1, l1_w3, l1_w2, g_final):
    bf = lambda w: w.astype(_BF)
    cond = jnp.concatenate([c, jnp.tile(c_ctx[None], (NB, 1)),
                            jnp.zeros((8 - BATCH - NB, D_MODEL), _F32)], axis=0)
    tabs = _rope_tables()
    tabs_t = [t.T for t in tabs]

    mods0 = _ada_params(cond, l0_w_mod, l0_b_mod)
    w_in0 = bf(jnp.concatenate([l0_w_in[:, :1920],
                                _pad_cols(l0_w_in[:, 1920:], 1, B_ROPE_DIM, LANES, B_NOPE_DIM)], axis=1))
    w_uq = bf(_pad_cols(l0_w_uq, B_HEADS, B_NOPE_DIM + B_ROPE_DIM, LANES))
    ukv = l0_w_ukv.reshape(B_KV_RANK, B_HEADS, B_NOPE_DIM + B_V_DIM)
    w_ukv = bf(jnp.concatenate([_pad_cols(ukv[:, :, :B_NOPE_DIM].reshape(B_KV_RANK, -1), B_HEADS, B_NOPE_DIM, LANES),
                                ukv[:, :, B_NOPE_DIM:].reshape(B_KV_RANK, -1)], axis=1))
    qdt, kd, vdt, qmt, km, vmt = _project0(x, ctx, mods0, l0_g_attn, w_in0, tabs, tabs_t,
                                           l0_g_cq, w_uq, l0_g_ckv, w_ukv)
    lamv = jnp.stack([l0_lam_q1, l0_lam_k1, l0_lam_q2, l0_lam_k2])
    o_a = _diff_attention(qdt, kd, vdt, lamv, l0_g_subln)
    o_b = _mla_attention(qmt, km, vmt)
    w_out0 = bf(l0_w_out)
    xs = _merge_ffn((x, ctx), o_a, o_b, mods0, w_out0[:512], w_out0[512:], l0_g_ffn,
                    bf(l0_w1), bf(l0_w3), bf(l0_w2), g_final, latent_only=False)

    mods1 = _ada_params(cond, l1_w_mod, l1_b_mod)
    wq = l1_w_in[:, :512].reshape(D_MODEL, 2, 4, C_HEAD_DIM).transpose(0, 2, 1, 3).reshape(D_MODEL, 512)
    w_in1 = bf(jnp.concatenate([wq, l1_w_in[:, 512:]], axis=1))
    grp = np.arange(LANES) // C_HEAD_DIM
    bd = jnp.asarray((grp[:, None] == grp[None, :]) / C_HEAD_DIM, _BF)
    qct, kc, vct, qnt, kn, vnt = _project1(xs, mods1, l1_g_attn, w_in1, tabs, tabs_t, bd, l1_g_qc, l1_g_kc)
    o_c = _gqa_attention(qct, kc, vct)
    o_d = _na_attention(qnt, kn, vnt, _na_bias_table(l1_rpb))
    w_out1 = bf(l1_w_out)
    woc = w_out1[:512].reshape(2, 4, C_HEAD_DIM, D_MODEL).transpose(1, 0, 2, 3).reshape(512, D_MODEL)
    return _merge_ffn((xs,), o_c, o_d, mods1, woc, w_out1[512:], l1_g_ffn,
                      bf(l1_w1), bf(l1_w3), bf(l1_w2), g_final, latent_only=True)
```

```python
import functools
import math

import numpy as np
import jax
import jax.numpy as jnp
from jax import lax
from jax.experimental import pallas as pl
from jax.experimental.pallas import tpu as pltpu

D_MODEL = 1024
BATCH = 4
SEQ = 4096
GRID_W = 64
CTX_LEN = 256
ROPE_THETA = 10000.0
EPS = 1e-6
A_HEADS = 4
A_HEAD_DIM = 64
B_HEADS = 8
B_Q_RANK = 256
B_KV_RANK = 128
B_NOPE_DIM = 64
B_ROPE_DIM = 32
B_V_DIM = 64
C_HEADS = 8
C_KV_HEADS = 2
C_HEAD_DIM = 64
D_HEADS = 8
D_HEAD_DIM = 64
NA_WIN_H = 8
NA_WIN_W = 16
FFN_HIDDEN = 2816
LAMBDA_INIT = 0.8 - 0.6 * math.exp(-0.3 * 0)

S_ALL = CTX_LEN + SEQ
LANES = 128
HALF = LANES // 2
ROW_TILE = 256
NB = 2
N_TILES = S_ALL // ROW_TILE
N_LAT_TILES = SEQ // ROW_TILE
Q2 = 2 * ROW_TILE
KEY_CHUNK = 256
N_KEY_CHUNKS = S_ALL // KEY_CHUNK
ROWS_PER_TILE = ROW_TILE // GRID_W
NA_KEY_ROWS = 12
NA_KEYS = NA_KEY_ROWS * GRID_W
LOG2E = math.log2(math.e)
NEG_BIG = -1e30
VMEM_LIMIT = 52 * 1024 * 1024

assert ROW_TILE == CTX_LEN and S_ALL % ROW_TILE == 0 and N_LAT_TILES % 2 == 0

_BF = jnp.bfloat16
_F32 = jnp.float32


def _dot(a, b):
    return jnp.dot(a, b, preferred_element_type=_F32)


def _rms(x):
    return x * lax.rsqrt(jnp.mean(x * x, axis=-1, keepdims=True) + EPS)


def _lane(shape):
    return lax.broadcasted_iota(jnp.int32, shape, len(shape) - 1)


def _swap_halves(x, half):
    n = x.shape[-1]
    up = pltpu.roll(x, n - half, axis=1)
    down = pltpu.roll(x, half, axis=1)
    return jnp.where((_lane(x.shape) % (2 * half)) < half, up, down)


def _rope(x, c, s, half):
    return x * c + _swap_halves(x, half) * s


def _rope_t(x, c, s, half):
    parts = []
    for r in range(0, x.shape[0], 2 * half):
        parts += [x[r + half:r + 2 * half], x[r:r + half]]
    return x * c + jnp.concatenate(parts, axis=0) * s


def _dot_nt(a, b):
    return lax.dot_general(a, b, (((1,), (1,)), ((), ())), preferred_element_type=_F32)


def _group_mean_sq(x, bd):
    x2 = x * x
    hi = x2.astype(_BF)
    lo = (x2 - hi.astype(_F32)).astype(_BF)
    return _dot(hi, bd) + _dot(lo, bd)


def _ada_body(cond_ref, w_ref, b_ref, o_ref):
    c = cond_ref[...]
    a = (c * (1.0 / (1.0 + jnp.exp(-c)))).astype(_BF)
    o_ref[...] = _dot(a, w_ref[...].astype(_BF)) + b_ref[...]


def _ada_params(cond, w_mod, b_mod):
    n = w_mod.shape[1]
    tn = n // 4
    out = pl.pallas_call(
        _ada_body,
        grid=(n // tn,),
        in_specs=[pl.BlockSpec((8, D_MODEL), lambda j: (0, 0)),
                  pl.BlockSpec((D_MODEL, tn), lambda j: (0, j)),
                  pl.BlockSpec((1, tn), lambda j: (0, j))],
        out_specs=pl.BlockSpec((8, tn), lambda j: (0, j)),
        out_shape=jax.ShapeDtypeStruct((8, n), _F32),
        compiler_params=pltpu.CompilerParams(vmem_limit_bytes=VMEM_LIMIT),
        name="ada_params",
    )(cond, w_mod, b_mod.reshape(1, n))
    return out.reshape(8, 6, D_MODEL)


def _mod_row(b, t):
    return jnp.where(t == 0, BATCH // NB, b)


def _const_spec(shape):
    nd = len(shape)
    return pl.BlockSpec(shape, lambda *_: (0,) * nd, pipeline_mode=pl.Buffered(1))


def _modulated(x, m, g, shift_row, scale_row):
    h = _rms(x) * g
    return h * (1.0 + m[scale_row:scale_row + 1]) + m[shift_row:shift_row + 1]


def _modulated_rows(xs, m, g, shift_row, scale_row):
    return jnp.concatenate([_modulated(xs[i], m[i], g, shift_row, scale_row) for i in range(NB)],
                           axis=0).astype(_BF)


def _stream_tile(x_ref, ctx_ref):
    return jnp.where(pl.program_id(1) == 0, ctx_ref[...], x_ref[...])


def _stream_specs():
    return [pl.BlockSpec((NB, ROW_TILE, D_MODEL), lambda b, t: (b, jnp.maximum(t - 1, 0), 0)),
            pl.BlockSpec((NB, CTX_LEN, D_MODEL), lambda b, t: (b, 0, 0))]


def _proj0_body(x_ref, ctx_ref, mod_ref, g_ref, wn_ref, wt_ref,
                ca_ref, sa_ref, cb_ref, sb_ref, cat_ref, sat_ref, cbt_ref, sbt_ref,
                g_cq_ref, w_uqt_ref, g_ckv_ref, w_uk_ref, w_uvt_ref,
                qdt_ref, kd_ref, vdt_ref, qmt_ref, km_ref, vmt_ref):
    h = _modulated_rows(_stream_tile(x_ref, ctx_ref), mod_ref[...], g_ref[...], 0, 1)
    yn = _dot(h, wn_ref[...])
    yt = _dot_nt(wt_ref[...], h)
    cq = (_rms(yn[:, 512:768]) * g_cq_ref[...]).astype(_BF)
    ckv = (_rms(yn[:, 768:896]) * g_ckv_ref[...]).astype(_BF)
    qmt = _dot_nt(w_uqt_ref[...], cq)
    kn = _dot(ckv, w_uk_ref[...])
    vmt = _dot_nt(w_uvt_ref[...], ckv)
    ca, sa, cb, sb = ca_ref[...], sa_ref[...], cb_ref[...], sb_ref[...]
    cat, sat, cbt, sbt = cat_ref[...], sat_ref[...], cbt_ref[...], sbt_ref[...]
    qscale = A_HEAD_DIM ** -0.5 * LOG2E
    mscale = (B_NOPE_DIM + B_ROPE_DIM) ** -0.5 * LOG2E
    for i in range(NB):
        rows = slice(i * ROW_TILE, (i + 1) * ROW_TILE)
        for hd in range(A_HEADS):
            lo = hd * LANES
            qdt_ref[i, hd] = (_rope_t(yt[lo:lo + LANES, rows], cat, sat, A_HEAD_DIM // 2) * qscale).astype(_BF)
            kd_ref[i, hd] = _rope(yn[rows, lo:lo + LANES], ca, sa, A_HEAD_DIM // 2).astype(_BF)
            vdt_ref[i, hd] = yt[512 + lo:512 + lo + LANES, rows].astype(_BF)
        kr = _rope(yn[rows, 896:1024], cb, sb, B_ROPE_DIM // 2)
        for hd in range(B_HEADS):
            lo = hd * LANES
            qmt_ref[i, hd] = (_rope_t(qmt[lo:lo + LANES, rows], cbt, sbt, B_ROPE_DIM // 2) * mscale).astype(_BF)
            km_ref[i, hd] = (kn[rows, lo:lo + LANES] + kr).astype(_BF)
        for pr in range(B_HEADS // 2):
            vmt_ref[i, pr] = vmt[pr * LANES:(pr + 1) * LANES, rows].astype(_BF)


def _head_out(n_heads, transposed):
    if transposed:
        return (jax.ShapeDtypeStruct((BATCH, n_heads, LANES, S_ALL), _BF),
                pl.BlockSpec((NB, n_heads, LANES, ROW_TILE), lambda b, t: (b, 0, 0, t)))
    return (jax.ShapeDtypeStruct((BATCH, n_heads, S_ALL, LANES), _BF),
            pl.BlockSpec((NB, n_heads, ROW_TILE, LANES), lambda b, t: (b, 0, t, 0)))


def _row_spec(width):
    return pl.BlockSpec((NB, ROW_TILE, width), lambda b, t: (b, t, 0))


def _mod_spec():
    return pl.BlockSpec((NB, 6, D_MODEL), lambda b, t: (_mod_row(b, t), 0, 0))


def _table_spec():
    return pl.BlockSpec((ROW_TILE, LANES), lambda b, t: (t, 0))


def _table_t_spec():
    return pl.BlockSpec((LANES, ROW_TILE), lambda b, t: (0, t))


def _project0(x, ctx, mods, g_attn, w_in, tabs, tabs_t, g_cq, w_uq, g_ckv, w_ukv):
    wn = jnp.concatenate([w_in[:, 512:1024], w_in[:, 1536:2048]], axis=1)
    wt = jnp.concatenate([w_in[:, 0:512], w_in[:, 1024:1536]], axis=1).T
    w_uqt = w_uq.T
    w_uk, w_uvt = w_ukv[:, :B_HEADS * LANES], w_ukv[:, B_HEADS * LANES:].T
    outs = [_head_out(4, True), _head_out(4, False), _head_out(4, True),
            _head_out(8, True), _head_out(8, False), _head_out(4, True)]
    return pl.pallas_call(
        _proj0_body,
        grid=(BATCH // NB, N_TILES),
        in_specs=_stream_specs() + [_mod_spec(), _const_spec((1, D_MODEL)),
                  _const_spec(wn.shape), _const_spec(wt.shape)]
                 + [_table_spec()] * 4 + [_table_t_spec()] * 4
                 + [_const_spec((1, B_Q_RANK)), _const_spec(w_uqt.shape),
                    _const_spec((1, B_KV_RANK)), _const_spec(w_uk.shape), _const_spec(w_uvt.shape)],
        out_specs=[o[1] for o in outs],
        out_shape=[o[0] for o in outs],
        compiler_params=pltpu.CompilerParams(vmem_limit_bytes=VMEM_LIMIT),
        name="project0",
    )(x, ctx, mods, g_attn.reshape(1, -1), wn, wt, *tabs, *tabs_t,
      g_cq.reshape(1, -1), w_uqt, g_ckv.reshape(1, -1), w_uk, w_uvt)


def _proj1_body(x_ref, mod_ref, g_ref, wn_ref, wt_ref, ca_ref, sa_ref, cat_ref, sat_ref, bd_ref,
                g_qc_ref, g_kc_ref,
                qct_ref, kc_ref, vct_ref, qnt_ref, kn_ref, vnt_ref):
    h = _modulated_rows(x_ref[...], mod_ref[...], g_ref[...], 0, 1)
    yn = _dot(h, wn_ref[...])
    yt = _dot_nt(wt_ref[...], h)
    ca, sa, cat, sat = ca_ref[...], sa_ref[...], cat_ref[...], sat_ref[...]
    qscale = C_HEAD_DIM ** -0.5 * LOG2E
    nscale = D_HEAD_DIM ** -0.5 * LOG2E

    kc_all = yn[:, 0:LANES]
    kc_all = kc_all * lax.rsqrt(_group_mean_sq(kc_all, bd_ref[...]) + EPS) * g_kc_ref[...]
    q3 = yt[0:512].reshape(C_HEADS, C_HEAD_DIM, NB * ROW_TILE)
    q3 = q3 * lax.rsqrt(jnp.mean(q3 * q3, axis=1, keepdims=True) + EPS)
    qc_all = q3.reshape(512, NB * ROW_TILE) * g_qc_ref[...]
    for i in range(NB):
        rows = slice(i * ROW_TILE, (i + 1) * ROW_TILE)
        kc_ref[i, 0] = _rope(kc_all[rows], ca, sa, C_HEAD_DIM // 2).astype(_BF)
        vct_ref[i, 0] = yt[512:640, rows].astype(_BF)
        for pr in range(4):
            lo = pr * LANES
            qct_ref[i, pr] = (_rope_t(qc_all[lo:lo + LANES, rows], cat, sat, C_HEAD_DIM // 2) * qscale).astype(_BF)
            qnt_ref[i, pr] = (yt[640 + lo:640 + lo + LANES, rows] * nscale).astype(_BF)
            kn_ref[i, pr] = yn[rows, LANES + lo:2 * LANES + lo].astype(_BF)
            vnt_ref[i, pr] = yt[1152 + lo:1152 + lo + LANES, rows].astype(_BF)


def _project1(xs, mods, g_attn, w_in, tabs, tabs_t, bd, g_qc, g_kc):
    wn = jnp.concatenate([w_in[:, 512:640], w_in[:, 1280:1792]], axis=1)
    wt = jnp.concatenate([w_in[:, 0:512], w_in[:, 640:1280], w_in[:, 1792:2304]], axis=1).T
    outs = [_head_out(4, True), _head_out(1, False), _head_out(1, True),
            _head_out(4, True), _head_out(4, False), _head_out(4, True)]
    return pl.pallas_call(
        _proj1_body,
        grid=(BATCH // NB, N_TILES),
        in_specs=[_row_spec(D_MODEL), _mod_spec(), _const_spec((1, D_MODEL)),
                  _const_spec(wn.shape), _const_spec(wt.shape),
                  _table_spec(), _table_spec(), _table_t_spec(), _table_t_spec(),
                  _const_spec(bd.shape),
                  _const_spec((512, 1)), _const_spec((1, LANES))],
        out_specs=[o[1] for o in outs],
        out_shape=[o[0] for o in outs],
        compiler_params=pltpu.CompilerParams(vmem_limit_bytes=VMEM_LIMIT),
        name="project1",
    )(xs, mods, g_attn.reshape(1, -1), wn, wt, tabs[0], tabs[1], tabs_t[0], tabs_t[1], bd,
      jnp.tile(g_qc, 8).reshape(-1, 1), jnp.tile(g_kc, 2).reshape(1, -1))


def _split_q_cols(qt):
    z = jnp.zeros((HALF, qt.shape[1]), qt.dtype)
    return jnp.concatenate([jnp.concatenate([qt[:HALF], z], axis=0),
                            jnp.concatenate([z, qt[HALF:]], axis=0)], axis=1)


def _pv_t(vt, e, split):
    if not split:
        return _dot(vt, e)
    return jnp.concatenate([_dot(vt[:HALF], e[:, :ROW_TILE]), _dot(vt[HALF:], e[:, ROW_TILE:])], axis=0)


def _normalise_t(acc, ls, split):
    if not split:
        return acc / ls
    return jnp.concatenate([acc[:HALF] / ls[:, :ROW_TILE], acc[HALF:] / ls[:, ROW_TILE:]], axis=0)


def _pipelined_tiles(load_q, score_chunk, value_chunk, finalize, n_chunks, bufs, acc_cols, score_group=1):
    def step(t_next, nxt, t_cur, cur):
        mx = None
        if t_next is not None:
            qb = load_q(t_next)
        if t_cur is not None:
            m_cur = cur[1][0:1, :]
            ls = jnp.zeros((1, Q2), _F32)
            acc = jnp.zeros((LANES, acc_cols), _F32)
        for c in range(n_chunks):
            rows = slice(c * KEY_CHUNK, (c + 1) * KEY_CHUNK)
            if t_next is not None and c % score_group == 0:
                n = min(score_group, n_chunks - c)
                s = score_chunk(t_next, qb, c, n)
                nxt[0][c * KEY_CHUNK:(c + n) * KEY_CHUNK, :] = s
                cm = jnp.max(s, axis=0, keepdims=True)
                mx = cm if mx is None else jnp.maximum(mx, cm)
            if t_cur is not None:
                e = jnp.exp2(cur[0][rows, :] - m_cur)
                ls = ls + jnp.sum(e, axis=0, keepdims=True)
                acc = acc + value_chunk(t_cur, c, e.astype(_BF))
        if t_next is not None:
            nxt[1][...] = jnp.broadcast_to(mx, nxt[1].shape)
        return (acc, ls) if t_cur is not None else None

    last = N_LAT_TILES - 1
    step(0, bufs[0], None, None)

    def pair(j, carry):
        finalize(jnp.maximum(2 * j - 1, 0), *carry)
        finalize(2 * j, *step(2 * j + 1, bufs[1], 2 * j, bufs[0]))
        return step(2 * j + 2, bufs[0], 2 * j + 1, bufs[1])

    carry = lax.fori_loop(0, N_LAT_TILES // 2 - 1, pair,
                          (jnp.zeros((LANES, acc_cols), _F32), jnp.ones((1, Q2), _F32)))
    finalize(last - 2, *carry)
    finalize(last - 1, *step(last, bufs[1], last - 1, bufs[0]))
    finalize(last, *step(None, None, last, bufs[1]))


def _dense_body(*refs, n_q, diff, ctx_tile, o_off):
    q_refs, k_refs, vt_ref = refs[:n_q], refs[n_q:2 * n_q], refs[2 * n_q]
    rest = refs[2 * n_q + 1:]
    if diff:
        lam_ref, g_ref, o_ref, s0, s1, m0, m1 = rest
        lv = lam_ref[...]
        lam = (jnp.exp(jnp.sum(lv[0:1] * lv[1:2], axis=-1, keepdims=True))
               - jnp.exp(jnp.sum(lv[2:3] * lv[3:4], axis=-1, keepdims=True)) + LAMBDA_INIT)
    else:
        o_ref, s0, s1, m0, m1 = rest

    def q_blocks(col0):
        if n_q == 1:
            return [_split_q_cols(q_refs[0][0, 0, :, pl.ds(col0, ROW_TILE)])]
        return [r[0, 0, :, pl.ds(col0, ROW_TILE)] for r in q_refs]

    def scores(qb, row0, rows):
        parts = [_dot(k_refs[i][0, 0, pl.ds(row0, rows), :], qb[i]) for i in range(n_q)]
        return parts[0] if n_q == 1 else jnp.concatenate(parts, axis=1)

    def finalize_rows(row0, acc, ls):
        ot = _normalise_t(acc, ls, not diff)
        if diff:
            o = (ot[:, :ROW_TILE] - lam * ot[:, ROW_TILE:]).T
            o = _rms(o) * g_ref[...] * (1.0 - LAMBDA_INIT)
        else:
            o = ot.T
        o_ref[0, pl.ds(row0, ROW_TILE), :] = o.astype(_BF)

    def load_q(t):
        return q_blocks(pl.multiple_of(CTX_LEN + t * ROW_TILE, ROW_TILE))

    def score_chunk(t, qb, c, n):
        return scores(qb, c * KEY_CHUNK, n * KEY_CHUNK)

    def value_chunk(t, c, e):
        return _pv_t(vt_ref[0, 0, :, c * KEY_CHUNK:(c + 1) * KEY_CHUNK], e, not diff)

    def finalize(t, acc, ls):
        finalize_rows(pl.multiple_of(o_off + t * ROW_TILE, ROW_TILE), acc, ls)

    _pipelined_tiles(load_q, score_chunk, value_chunk, finalize, N_KEY_CHUNKS, ((s0, m0), (s1, m1)),
                     Q2 if diff else ROW_TILE)

    if ctx_tile:
        s = scores(q_blocks(0), 0, CTX_LEN)
        e = jnp.exp2(s - jnp.max(s, axis=0, keepdims=True))
        finalize_rows(0, _pv_t(vt_ref[0, 0, :, 0:CTX_LEN], e.astype(_BF), not diff),
                      jnp.sum(e, axis=0, keepdims=True))


def _na_body(qt_ref, k_ref, vt_ref, bias_ref, o_ref, s0, s1, m0, m1):
    def key_row0(t, c):
        first = jnp.clip(t - 1, 0, N_LAT_TILES - 3)
        return 0 if c == 0 else pl.multiple_of(CTX_LEN + (first + c - 1) * KEY_CHUNK, KEY_CHUNK)

    def load_q(t):
        return _split_q_cols(qt_ref[0, 0, :, pl.ds(pl.multiple_of(CTX_LEN + t * ROW_TILE, ROW_TILE), ROW_TILE)])

    def score_chunk(t, qb, c, n):
        s = _dot(k_ref[0, 0, pl.ds(key_row0(t, c), KEY_CHUNK), :], qb)
        if c > 0:
            kind = jnp.where(t == 0, 0, jnp.where(t == N_LAT_TILES - 1, 2, 1))
            s = s + bias_ref[0, kind, (c - 1) * KEY_CHUNK:c * KEY_CHUNK, :]
        return s

    def value_chunk(t, c, e):
        return _pv_t(vt_ref[0, 0, :, pl.ds(key_row0(t, c), KEY_CHUNK)], e, True)

    def finalize(t, acc, ls):
        o = _normalise_t(acc, ls, True).T
        o_ref[0, pl.ds(pl.multiple_of(t * ROW_TILE, ROW_TILE), ROW_TILE), :] = o.astype(_BF)

    _pipelined_tiles(load_q, score_chunk, value_chunk, finalize, 1 + NA_KEYS // KEY_CHUNK,
                     ((s0, m0), (s1, m1)), ROW_TILE)


def _attn_call(body, name, in_specs, arrays, out_rows, n_keys):
    return pl.pallas_call(
        body,
        grid=(BATCH, 4),
        in_specs=in_specs,
        out_specs=pl.BlockSpec((1, out_rows, LANES), lambda b, g: (b, 0, g)),
        out_shape=jax.ShapeDtypeStruct((BATCH, out_rows, 4 * LANES), _BF),
        scratch_shapes=[pltpu.VMEM((n_keys, Q2), _F32), pltpu.VMEM((n_keys, Q2), _F32),
                        pltpu.VMEM((8, Q2), _F32), pltpu.VMEM((8, Q2), _F32)],
        compiler_params=pltpu.CompilerParams(vmem_limit_bytes=VMEM_LIMIT),
        name=name,
    )(*arrays)


def _t_spec(f):
    return pl.BlockSpec((1, 1, LANES, S_ALL), f)


def _n_spec(f):
    return pl.BlockSpec((1, 1, S_ALL, LANES), f)


def _diff_attention(qdt, kd, vdt, lamv, g_subln):
    head = lambda b, g: (b, g, 0, 0)
    body = functools.partial(_dense_body, n_q=1, diff=True, ctx_tile=True, o_off=CTX_LEN)
    return _attn_call(body, "diff_attention",
                      [_t_spec(head), _n_spec(head), _t_spec(head),
                       pl.BlockSpec((4, A_HEAD_DIM), lambda b, g: (0, 0)),
                       pl.BlockSpec((1, LANES), lambda b, g: (0, 0))],
                      (qdt, kd, vdt, lamv, g_subln.reshape(1, -1)), S_ALL, S_ALL)


def _mla_attention(qmt, km, vmt):
    ha = lambda b, g: (b, 2 * g, 0, 0)
    hb = lambda b, g: (b, 2 * g + 1, 0, 0)
    body = functools.partial(_dense_body, n_q=2, diff=False, ctx_tile=True, o_off=CTX_LEN)
    return _attn_call(body, "mla_attention",
                      [_t_spec(ha), _t_spec(hb), _n_spec(ha), _n_spec(hb),
                       _t_spec(lambda b, g: (b, g, 0, 0))],
                      (qmt, qmt, km, km, vmt), S_ALL, S_ALL)


def _gqa_attention(qct, kc, vct):
    shared = lambda b, g: (b, 0, 0, 0)
    body = functools.partial(_dense_body, n_q=1, diff=False, ctx_tile=False, o_off=0)
    return _attn_call(body, "gqa_attention",
                      [_t_spec(lambda b, g: (b, g, 0, 0)), _n_spec(shared), _t_spec(shared)],
                      (qct, kc, vct), SEQ, S_ALL)


def _na_attention(qnt, kn, vnt, bias_t):
    head = lambda b, g: (b, g, 0, 0)
    return _attn_call(_na_body, "na_attention",
                      [_t_spec(head), _n_spec(head), _t_spec(head),
                       pl.BlockSpec((1, 3, NA_KEYS, Q2), lambda b, g: (g, 0, 0, 0))],
                      (qnt, kn, vnt, bias_t), SEQ, CTX_LEN + NA_KEYS)


def _merge_ffn_body(*refs, final):
    x = refs[0][...] if final else _stream_tile(refs[0], refs[1])
    (oa_ref, ob_ref, mod_ref, woa_ref, wob_ref, g_ref,
     w1_ref, w3_ref, w2_ref, gf_ref, o_ref) = refs[1 if final else 2:]
    m = mod_ref[...]
    stack = lambda r: r[...].reshape(NB * ROW_TILE, r.shape[-1])
    rows = lambda v, i: v[i * ROW_TILE:(i + 1) * ROW_TILE]
    attn = _dot(stack(oa_ref), woa_ref[...]) + _dot(stack(ob_ref), wob_ref[...])
    x1 = [x[i] + m[i, 2:3] * rows(attn, i) for i in range(NB)]
    h = _modulated_rows(x1, m, g_ref[...], 3, 4)
    a = _dot(h, w1_ref[...])
    b = _dot(h, w3_ref[...])
    u = (a * (1.0 / (1.0 + jnp.exp(-a))) * b).astype(_BF)
    f = _dot(u, w2_ref[...])
    for i in range(NB):
        x2 = x1[i] + m[i, 5:6] * rows(f, i)
        if final:
            x2 = _rms(x2) * gf_ref[...]
        o_ref[i] = x2


def _merge_ffn(streams, oa, ob, mods, woa, wob, g_ffn, w1, w3, w2, g_final, *, latent_only):
    off = 1 if latent_only else 0
    tiles = N_LAT_TILES if latent_only else N_TILES
    row = lambda w, o: pl.BlockSpec((NB, ROW_TILE, w), lambda b, t: (b, t + o, 0))
    mod = pl.BlockSpec((NB, 6, D_MODEL), lambda b, t: (_mod_row(b, t + off), 0, 0))
    x_specs = [row(D_MODEL, off)] if latent_only else _stream_specs()
    return pl.pallas_call(
        functools.partial(_merge_ffn_body, final=latent_only),
        grid=(BATCH // NB, tiles),
        in_specs=x_specs + [row(512, 0), row(512, 0), mod,
                  _const_spec(woa.shape), _const_spec(wob.shape), _const_spec((1, D_MODEL)),
                  _const_spec(w1.shape), _const_spec(w3.shape), _const_spec(w2.shape),
                  _const_spec((1, D_MODEL))],
        out_specs=row(D_MODEL, 0),
        out_shape=jax.ShapeDtypeStruct((BATCH, tiles * ROW_TILE, D_MODEL), _F32),
        compiler_params=pltpu.CompilerParams(vmem_limit_bytes=VMEM_LIMIT),
        name="merge_ffn_final" if latent_only else "merge_ffn",
    )(*streams, oa, ob, mods, woa, wob, g_ffn.reshape(1, -1), w1, w3, w2, g_final.reshape(1, -1))


def _rope_tables():
    t = np.arange(SEQ)
    rows = jnp.asarray(t // GRID_W, _F32)
    cols = jnp.asarray(t % GRID_W, _F32)

    def cs(rot_dim):
        axis_dim = rot_dim // 2
        inv = ROPE_THETA ** (-jnp.arange(0, axis_dim, 2, dtype=_F32) / axis_dim)
        ang = jnp.concatenate([rows[:, None] * inv, cols[:, None] * inv], axis=-1)
        return jnp.cos(ang), jnp.sin(ang)

    def with_ctx(tab, fill):
        return jnp.concatenate([jnp.full((CTX_LEN, LANES), fill, _F32), tab], axis=0)

    cos, sin = cs(A_HEAD_DIM)
    ca = with_ctx(jnp.tile(cos, (1, 4)), 1.0)
    sa = with_ctx(jnp.tile(jnp.concatenate([-sin, sin], axis=-1), (1, 2)), 0.0)
    cos, sin = cs(B_ROPE_DIM)
    ones = jnp.ones((SEQ, 64), _F32)
    zeros = jnp.zeros((SEQ, 64), _F32)
    cb = with_ctx(jnp.concatenate([ones, cos, cos, ones[:, :32]], axis=-1), 1.0)
    sb = with_ctx(jnp.concatenate([zeros, -sin, sin, zeros[:, :32]], axis=-1), 0.0)
    return ca, sa, cb, sb


def _na_bias_table(rpb):
    rows_n = SEQ // GRID_W
    span = 2 * GRID_W - 1
    left = GRID_W - NA_WIN_W
    p = jnp.pad(rpb * LOG2E, ((0, 0), (0, 0), (left, span - left - (2 * NA_WIN_W - 1))))
    flat = jnp.tile(p[:, :, ::-1], (1, 1, GRID_W + 1))[..., :GRID_W * (span + 1)]
    toe = flat.reshape(D_HEADS, 2 * NA_WIN_H - 1, GRID_W, span + 1)[:, :, ::-1, :GRID_W]
    kc = np.arange(GRID_W)[:, None]
    qc = np.arange(GRID_W)[None, :]
    cs = np.clip(qc - NA_WIN_W // 2, 0, GRID_W - NA_WIN_W)
    toe = jnp.where((kc >= cs) & (kc < cs + NA_WIN_W), toe, NEG_BIG)
    kinds = [(0, 0), (ROWS_PER_TILE, 0), (rows_n - ROWS_PER_TILE, rows_n - NA_KEY_ROWS)]
    d_first = [k0 - q0 + NA_WIN_H - 1 for q0, k0 in kinds]
    d_lo, d_hi = min(d_first), max(d_first) + NA_KEY_ROWS
    n_dr = 2 * NA_WIN_H - 1
    before, after = ROWS_PER_TILE - 1 - d_lo, d_hi - n_dr
    fill = lambda n: jnp.full((D_HEADS, n, GRID_W, GRID_W), NEG_BIG, _F32)
    tp = jnp.concatenate([fill(before), toe, fill(after)], axis=1).reshape(4, 2, before + n_dr + after, GRID_W, GRID_W)
    quads = jnp.concatenate([tp[:, w, ROWS_PER_TILE - 1 - qr:ROWS_PER_TILE - 1 - qr + d_hi - d_lo]
                             for w in (0, 1) for qr in range(ROWS_PER_TILE)], axis=-1)

    tiles = []
    for (q0, k0), d0 in zip(kinds, d_first):
        rs = np.clip(q0 + np.arange(ROWS_PER_TILE) - NA_WIN_H // 2, 0, rows_n - NA_WIN_H)[None, :]
        krow = (k0 + np.arange(NA_KEY_ROWS))[:, None]
        inside = (krow >= rs) & (krow < rs + NA_WIN_H)
        inside = np.tile(np.repeat(np.repeat(inside, GRID_W, axis=0), GRID_W, axis=1), (1, 2))
        block = quads[:, d0 - d_lo:d0 - d_lo + NA_KEY_ROWS].reshape(4, NA_KEYS, Q2)
        tiles.append(jnp.where(inside, block, NEG_BIG))
    return jnp.stack(tiles, axis=1)


def _pad_cols(w, groups, width, total, offset=0):
    k = w.shape[0]
    w = w.reshape(k, groups, width)
    w = jnp.pad(w, ((0, 0), (0, 0), (offset, total - width - offset)))
    return w.reshape(k, groups * total)


def kernel(x, c, ctx, c_ctx, l0_w_mod, l0_b_mod, l0_g_attn, l0_w_in, l0_lam_q1, l0_lam_k1, l0_lam_q2, l0_lam_k2, l0_g_subln, l0_g_cq, l0_w_uq, l0_g_ckv, l0_w_ukv, l0_w_out, l0_g_ffn, l0_w1, l0_w3, l0_w2, l1_w_mod, l1_b_mod, l1_g_attn, l1_w_in, l1_g_qc, l1_g_kc, l1_rpb, l1_w_out, l1_g_ffn, l1_w1, l1_w3, l1_w2, g_final):
    bf = lambda w: w.astype(_BF)
    cond = jnp.concatenate([c, jnp.tile(c_ctx[None], (NB, 1)),
                            jnp.zeros((8 - BATCH - NB, D_MODEL), _F32)], axis=0)
    tabs = _rope_tables()
    tabs_t = [t.T for t in tabs]

    mods0 = _ada_params(cond, l0_w_mod, l0_b_mod)
    w_in0 = bf(jnp.concatenate([l0_w_in[:, :1920],
                                _pad_cols(l0_w_in[:, 1920:], 1, B_ROPE_DIM, LANES, B_NOPE_DIM)], axis=1))
    w_uq = bf(_pad_cols(l0_w_uq, B_HEADS, B_NOPE_DIM + B_ROPE_DIM, LANES))
    ukv = l0_w_ukv.reshape(B_KV_RANK, B_HEADS, B_NOPE_DIM + B_V_DIM)
    w_ukv = bf(jnp.concatenate([_pad_cols(ukv[:, :, :B_NOPE_DIM].reshape(B_KV_RANK, -1), B_HEADS, B_NOPE_DIM, LANES),
                                ukv[:, :, B_NOPE_DIM:].reshape(B_KV_RANK, -1)], axis=1))
    qdt, kd, vdt, qmt, km, vmt = _project0(x, ctx, mods0, l0_g_attn, w_in0, tabs, tabs_t,
                                           l0_g_cq, w_uq, l0_g_ckv, w_ukv)
    lamv = jnp.stack([l0_lam_q1, l0_lam_k1, l0_lam_q2, l0_lam_k2])
    o_a = _diff_attention(qdt, kd, vdt, lamv, l0_g_subln)
    o_b = _mla_attention(qmt, km, vmt)
    w_out0 = bf(l0_w_out)
    xs = _merge_ffn((x, ctx), o_a, o_b, mods0, w_out0[:512], w_out0[512:], l0_g_ffn,
                    bf(l0_w1), bf(l0_w3), bf(l0_w2), g_final, latent_only=False)

    mods1 = _ada_params(cond, l1_w_mod, l1_b_mod)
    wq = l1_w_in[:, :512].reshape(D_MODEL, 2, 4, C_HEAD_DIM).transpose(0, 2, 1, 3).reshape(D_MODEL, 512)
    w_in1 = bf(jnp.concatenate([wq, l1_w_in[:, 512:]], axis=1))
    grp = np.arange(LANES) // C_HEAD_DIM
    bd = jnp.asarray((grp[:, None] == grp[None, :]) / C_HEAD_DIM, _BF)
    qct, kc, vct, qnt, kn, vnt = _project1(xs, mods1, l1_g_attn, w_in1, tabs, tabs_t, bd, l1_g_qc, l1_g_kc)
    o_c = _gqa_attention(qct, kc, vct)
    o_d = _na_attention(qnt, kn, vnt, _na_bias_table(l1_rpb))
    w_out1 = bf(l1_w_out)
    woc = w_out1[:512].reshape(2, 4, C_HEAD_DIM, D_MODEL).transpose(1, 0, 2, 3).reshape(512, D_MODEL)
    return _merge_ffn((xs,), o_c, o_d, mods1, woc, w_out1[512:], l1_g_ffn,
                      bf(l1_w1), bf(l1_w3), bf(l1_w2), g_final, latent_only=True)
```

```python
import functools
import math

import numpy as np
import jax
import jax.numpy as jnp
from jax import lax
from jax.experimental import pallas as pl
from jax.experimental.pallas import tpu as pltpu

D_MODEL = 1024
BATCH = 4
SEQ = 4096
GRID_W = 64
CTX_LEN = 256
ROPE_THETA = 10000.0
EPS = 1e-6
A_HEADS = 4
A_HEAD_DIM = 64
B_HEADS = 8
B_Q_RANK = 256
B_KV_RANK = 128
B_NOPE_DIM = 64
B_ROPE_DIM = 32
B_V_DIM = 64
C_HEADS = 8
C_KV_HEADS = 2
C_HEAD_DIM = 64
D_HEADS = 8
D_HEAD_DIM = 64
NA_WIN_H = 8
NA_WIN_W = 16
FFN_HIDDEN = 2816
LAMBDA_INIT = 0.8 - 0.6 * math.exp(-0.3 * 0)

S_ALL = CTX_LEN + SEQ
LANES = 128
HALF = LANES // 2
ROW_TILE = 256
NB = 2
N_TILES = S_ALL // ROW_TILE
N_LAT_TILES = SEQ // ROW_TILE
Q2 = 2 * ROW_TILE
KEY_CHUNK = 256
N_KEY_CHUNKS = S_ALL // KEY_CHUNK
ROWS_PER_TILE = ROW_TILE // GRID_W
NA_KEY_ROWS = 12
NA_KEYS = NA_KEY_ROWS * GRID_W
LOG2E = math.log2(math.e)
NEG_BIG = -1e30
VMEM_LIMIT = 52 * 1024 * 1024

assert ROW_TILE == CTX_LEN and S_ALL % ROW_TILE == 0 and N_LAT_TILES % 2 == 0

_BF = jnp.bfloat16
_F32 = jnp.float32


def _dot(a, b):
    return jnp.dot(a, b, preferred_element_type=_F32)


def _rms(x):
    return x * lax.rsqrt(jnp.mean(x * x, axis=-1, keepdims=True) + EPS)


def _lane(shape):
    return lax.broadcasted_iota(jnp.int32, shape, len(shape) - 1)


def _swap_halves(x, half):
    n = x.shape[-1]
    up = pltpu.roll(x, n - half, axis=1)
    down = pltpu.roll(x, half, axis=1)
    return jnp.where((_lane(x.shape) % (2 * half)) < half, up, down)


def _rope(x, c, s, half):
    return x * c + _swap_halves(x, half) * s


def _rope_t(x, c, s, half):
    parts = []
    for r in range(0, x.shape[0], 2 * half):
        parts += [x[r + half:r + 2 * half], x[r:r + half]]
    return x * c + jnp.concatenate(parts, axis=0) * s


def _dot_nt(a, b):
    return lax.dot_general(a, b, (((1,), (1,)), ((), ())), preferred_element_type=_F32)


def _group_mean_sq(x, bd):
    x2 = x * x
    hi = x2.astype(_BF)
    lo = (x2 - hi.astype(_F32)).astype(_BF)
    return _dot(hi, bd) + _dot(lo, bd)


def _ada_body(cond_ref, w_ref, b_ref, o_ref):
    c = cond_ref[...]
    a = (c * (1.0 / (1.0 + jnp.exp(-c)))).astype(_BF)
    o_ref[...] = _dot(a, w_ref[...].astype(_BF)) + b_ref[...]


def _ada_params(cond, w_mod, b_mod):
    n = w_mod.shape[1]
    tn = n // 4
    out = pl.pallas_call(
        _ada_body,
        grid=(n // tn,),
        in_specs=[pl.BlockSpec((8, D_MODEL), lambda j: (0, 0)),
                  pl.BlockSpec((D_MODEL, tn), lambda j: (0, j)),
                  pl.BlockSpec((1, tn), lambda j: (0, j))],
        out_specs=pl.BlockSpec((8, tn), lambda j: (0, j)),
        out_shape=jax.ShapeDtypeStruct((8, n), _F32),
        compiler_params=pltpu.CompilerParams(vmem_limit_bytes=VMEM_LIMIT),
        name="ada_params",
    )(cond, w_mod, b_mod.reshape(1, n))
    return out.reshape(8, 6, D_MODEL)


def _mod_row(b, t):
    return jnp.where(t == 0, BATCH // NB, b)


def _const_spec(shape):
    nd = len(shape)
    return pl.BlockSpec(shape, lambda *_: (0,) * nd, pipeline_mode=pl.Buffered(1))


def _modulated(x, m, g, shift_row, scale_row):
    h = _rms(x) * g
    return h * (1.0 + m[scale_row:scale_row + 1]) + m[shift_row:shift_row + 1]


def _modulated_rows(xs, m, g, shift_row, scale_row):
    return jnp.concatenate([_modulated(xs[i], m[i], g, shift_row, scale_row) for i in range(NB)],
                           axis=0).astype(_BF)


def _stream_tile(x_ref, ctx_ref):
    return jnp.where(pl.program_id(1) == 0, ctx_ref[...], x_ref[...])


def _stream_specs():
    return [pl.BlockSpec((NB, ROW_TILE, D_MODEL), lambda b, t: (b, jnp.maximum(t - 1, 0), 0)),
            pl.BlockSpec((NB, CTX_LEN, D_MODEL), lambda b, t: (b, 0, 0))]


def _proj0_body(x_ref, ctx_ref, mod_ref, g_ref, wn_ref, wt_ref,
                ca_ref, sa_ref, cb_ref, sb_ref, cat_ref, sat_ref, cbt_ref, sbt_ref,
                g_cq_ref, w_uqt_ref, g_ckv_ref, w_uk_ref, w_uvt_ref,
                qdt_ref, kd_ref, vdt_ref, qmt_ref, km_ref, vmt_ref):
    h = _modulated_rows(_stream_tile(x_ref, ctx_ref), mod_ref[...], g_ref[...], 0, 1)
    yn = _dot(h, wn_ref[...])
    yt = _dot_nt(wt_ref[...], h)
    cq = (_rms(yn[:, 512:768]) * g_cq_ref[...]).astype(_BF)
    ckv = (_rms(yn[:, 768:896]) * g_ckv_ref[...]).astype(_BF)
    qmt = _dot_nt(w_uqt_ref[...], cq)
    kn = _dot(ckv, w_uk_ref[...])
    vmt = _dot_nt(w_uvt_ref[...], ckv)
    ca, sa, cb, sb = ca_ref[...], sa_ref[...], cb_ref[...], sb_ref[...]
    cat, sat, cbt, sbt = cat_ref[...], sat_ref[...], cbt_ref[...], sbt_ref[...]
    qscale = A_HEAD_DIM ** -0.5 * LOG2E
    mscale = (B_NOPE_DIM + B_ROPE_DIM) ** -0.5 * LOG2E
    for i in range(NB):
        rows = slice(i * ROW_TILE, (i + 1) * ROW_TILE)
        for hd in range(A_HEADS):
            lo = hd * LANES
            qdt_ref[i, hd] = (_rope_t(yt[lo:lo + LANES, rows], cat, sat, A_HEAD_DIM // 2) * qscale).astype(_BF)
            kd_ref[i, hd] = _rope(yn[rows, lo:lo + LANES], ca, sa, A_HEAD_DIM // 2).astype(_BF)
            vdt_ref[i, hd] = yt[512 + lo:512 + lo + LANES, rows].astype(_BF)
        kr = _rope(yn[rows, 896:1024], cb, sb, B_ROPE_DIM // 2)
        for hd in range(B_HEADS):
            lo = hd * LANES
            qmt_ref[i, hd] = (_rope_t(qmt[lo:lo + LANES, rows], cbt, sbt, B_ROPE_DIM // 2) * mscale).astype(_BF)
            km_ref[i, hd] = (kn[rows, lo:lo + LANES] + kr).astype(_BF)
        for pr in range(B_HEADS // 2):
            vmt_ref[i, pr] = vmt[pr * LANES:(pr + 1) * LANES, rows].astype(_BF)


def _head_out(n_heads, transposed):
    if transposed:
        return (jax.ShapeDtypeStruct((BATCH, n_heads, LANES, S_ALL), _BF),
                pl.BlockSpec((NB, n_heads, LANES, ROW_TILE), lambda b, t: (b, 0, 0, t)))
    return (jax.ShapeDtypeStruct((BATCH, n_heads, S_ALL, LANES), _BF),
            pl.BlockSpec((NB, n_heads, ROW_TILE, LANES), lambda b, t: (b, 0, t, 0)))


def _row_spec(width):
    return pl.BlockSpec((NB, ROW_TILE, width), lambda b, t: (b, t, 0))


def _mod_spec():
    return pl.BlockSpec((NB, 6, D_MODEL), lambda b, t: (_mod_row(b, t), 0, 0))


def _table_spec():
    return pl.BlockSpec((ROW_TILE, LANES), lambda b, t: (t, 0))


def _table_t_spec():
    return pl.BlockSpec((LANES, ROW_TILE), lambda b, t: (0, t))


def _project0(x, ctx, mods, g_attn, w_in, tabs, tabs_t, g_cq, w_uq, g_ckv, w_ukv):
    wn = jnp.concatenate([w_in[:, 512:1024], w_in[:, 1536:2048]], axis=1)
    wt = jnp.concatenate([w_in[:, 0:512], w_in[:, 1024:1536]], axis=1).T
    w_uqt = w_uq.T
    w_uk, w_uvt = w_ukv[:, :B_HEADS * LANES], w_ukv[:, B_HEADS * LANES:].T
    outs = [_head_out(4, True), _head_out(4, False), _head_out(4, True),
            _head_out(8, True), _head_out(8, False), _head_out(4, True)]
    return pl.pallas_call(
        _proj0_body,
        grid=(BATCH // NB, N_TILES),
        in_specs=_stream_specs() + [_mod_spec(), _const_spec((1, D_MODEL)),
                  _const_spec(wn.shape), _const_spec(wt.shape)]
                 + [_table_spec()] * 4 + [_table_t_spec()] * 4
                 + [_const_spec((1, B_Q_RANK)), _const_spec(w_uqt.shape),
                    _const_spec((1, B_KV_RANK)), _const_spec(w_uk.shape), _const_spec(w_uvt.shape)],
        out_specs=[o[1] for o in outs],
        out_shape=[o[0] for o in outs],
        compiler_params=pltpu.CompilerParams(vmem_limit_bytes=VMEM_LIMIT),
        name="project0",
    )(x, ctx, mods, g_attn.reshape(1, -1), wn, wt, *tabs, *tabs_t,
      g_cq.reshape(1, -1), w_uqt, g_ckv.reshape(1, -1), w_uk, w_uvt)


def _proj1_rows(xs, m, g_ref, wn_ref, wt_ref, ca_ref, sa_ref, cat_ref, sat_ref, bd_ref,
                g_qc_ref, g_kc_ref,
                qct_ref, kc_ref, vct_ref, qnt_ref, kn_ref, vnt_ref):
    h = _modulated_rows(xs, m, g_ref[...], 0, 1)
    yn = _dot(h, wn_ref[...])
    yt = _dot_nt(wt_ref[...], h)
    ca, sa, cat, sat = ca_ref[...], sa_ref[...], cat_ref[...], sat_ref[...]
    qscale = C_HEAD_DIM ** -0.5 * LOG2E
    nscale = D_HEAD_DIM ** -0.5 * LOG2E

    kc_all = yn[:, 0:LANES]
    kc_all = kc_all * lax.rsqrt(_group_mean_sq(kc_all, bd_ref[...]) + EPS) * g_kc_ref[...]
    q3 = yt[0:512].reshape(C_HEADS, C_HEAD_DIM, NB * ROW_TILE)
    q3 = q3 * lax.rsqrt(jnp.mean(q3 * q3, axis=1, keepdims=True) + EPS)
    qc_all = q3.reshape(512, NB * ROW_TILE) * g_qc_ref[...]
    for i in range(NB):
        rows = slice(i * ROW_TILE, (i + 1) * ROW_TILE)
        kc_ref[i, 0] = _rope(kc_all[rows], ca, sa, C_HEAD_DIM // 2).astype(_BF)
        vct_ref[i, 0] = yt[512:640, rows].astype(_BF)
        for pr in range(4):
            lo = pr * LANES
            qct_ref[i, pr] = (_rope_t(qc_all[lo:lo + LANES, rows], cat, sat, C_HEAD_DIM // 2) * qscale).astype(_BF)
            qnt_ref[i, pr] = (yt[640 + lo:640 + lo + LANES, rows] * nscale).astype(_BF)
            kn_ref[i, pr] = yn[rows, LANES + lo:2 * LANES + lo].astype(_BF)
            vnt_ref[i, pr] = yt[1152 + lo:1152 + lo + LANES, rows].astype(_BF)


def _project1_operands(mods, g_attn, w_in, tabs, tabs_t, bd, g_qc, g_kc):
    wn = jnp.concatenate([w_in[:, 512:640], w_in[:, 1280:1792]], axis=1)
    wt = jnp.concatenate([w_in[:, 0:512], w_in[:, 640:1280], w_in[:, 1792:2304]], axis=1).T
    specs = [_mod_spec(), _const_spec((1, D_MODEL)), _const_spec(wn.shape), _const_spec(wt.shape),
             _table_spec(), _table_spec(), _table_t_spec(), _table_t_spec(), _const_spec(bd.shape),
             _const_spec((512, 1)), _const_spec((1, LANES))]
    arrays = (mods, g_attn.reshape(1, -1), wn, wt, tabs[0], tabs[1], tabs_t[0], tabs_t[1], bd,
              jnp.tile(g_qc, 8).reshape(-1, 1), jnp.tile(g_kc, 2).reshape(1, -1))
    outs = [_head_out(4, True), _head_out(1, False), _head_out(1, True),
            _head_out(4, True), _head_out(4, False), _head_out(4, True)]
    return specs, arrays, outs


def _split_q_cols(qt):
    z = jnp.zeros((HALF, qt.shape[1]), qt.dtype)
    return jnp.concatenate([jnp.concatenate([qt[:HALF], z], axis=0),
                            jnp.concatenate([z, qt[HALF:]], axis=0)], axis=1)


def _pv_t(vt, e, split):
    if not split:
        return _dot(vt, e)
    return jnp.concatenate([_dot(vt[:HALF], e[:, :ROW_TILE]), _dot(vt[HALF:], e[:, ROW_TILE:])], axis=0)


def _normalise_t(acc, ls, split):
    if not split:
        return acc / ls
    return jnp.concatenate([acc[:HALF] / ls[:, :ROW_TILE], acc[HALF:] / ls[:, ROW_TILE:]], axis=0)


def _pipelined_tiles(load_q, score_chunk, value_chunk, finalize, n_chunks, bufs, acc_cols, score_group=1):
    def step(t_next, nxt, t_cur, cur):
        mx = None
        if t_next is not None:
            qb = load_q(t_next)
        if t_cur is not None:
            m_cur = cur[1][0:1, :]
            ls = jnp.zeros((1, Q2), _F32)
            acc = jnp.zeros((LANES, acc_cols), _F32)
        for c in range(n_chunks):
            rows = slice(c * KEY_CHUNK, (c + 1) * KEY_CHUNK)
            if t_next is not None and c % score_group == 0:
                n = min(score_group, n_chunks - c)
                s = score_chunk(t_next, qb, c, n)
                nxt[0][c * KEY_CHUNK:(c + n) * KEY_CHUNK, :] = s
                cm = jnp.max(s, axis=0, keepdims=True)
                mx = cm if mx is None else jnp.maximum(mx, cm)
            if t_cur is not None:
                e = jnp.exp2(cur[0][rows, :] - m_cur)
                ls = ls + jnp.sum(e, axis=0, keepdims=True)
                acc = acc + value_chunk(t_cur, c, e.astype(_BF))
        if t_next is not None:
            nxt[1][...] = jnp.broadcast_to(mx, nxt[1].shape)
        return (acc, ls) if t_cur is not None else None

    last = N_LAT_TILES - 1
    step(0, bufs[0], None, None)

    def pair(j, carry):
        finalize(jnp.maximum(2 * j - 1, 0), *carry)
        finalize(2 * j, *step(2 * j + 1, bufs[1], 2 * j, bufs[0]))
        return step(2 * j + 2, bufs[0], 2 * j + 1, bufs[1])

    carry = lax.fori_loop(0, N_LAT_TILES // 2 - 1, pair,
                          (jnp.zeros((LANES, acc_cols), _F32), jnp.ones((1, Q2), _F32)))
    finalize(last - 2, *carry)
    finalize(last - 1, *step(last, bufs[1], last - 1, bufs[0]))
    finalize(last, *step(None, None, last, bufs[1]))


def _dense_body(*refs, n_q, diff, ctx_tile, o_off):
    q_refs, k_refs, vt_ref = refs[:n_q], refs[n_q:2 * n_q], refs[2 * n_q]
    rest = refs[2 * n_q + 1:]
    if diff:
        lam_ref, g_ref, o_ref, s0, s1, m0, m1 = rest
        lv = lam_ref[...]
        lam = (jnp.exp(jnp.sum(lv[0:1] * lv[1:2], axis=-1, keepdims=True))
               - jnp.exp(jnp.sum(lv[2:3] * lv[3:4], axis=-1, keepdims=True)) + LAMBDA_INIT)
    else:
        o_ref, s0, s1, m0, m1 = rest

    def q_blocks(col0):
        if n_q == 1:
            return [_split_q_cols(q_refs[0][0, 0, :, pl.ds(col0, ROW_TILE)])]
        return [r[0, 0, :, pl.ds(col0, ROW_TILE)] for r in q_refs]

    def scores(qb, row0, rows):
        parts = [_dot(k_refs[i][0, 0, pl.ds(row0, rows), :], qb[i]) for i in range(n_q)]
        return parts[0] if n_q == 1 else jnp.concatenate(parts, axis=1)

    def finalize_rows(row0, acc, ls):
        ot = _normalise_t(acc, ls, not diff)
        if diff:
            o = (ot[:, :ROW_TILE] - lam * ot[:, ROW_TILE:]).T
            o = _rms(o) * g_ref[...] * (1.0 - LAMBDA_INIT)
        else:
            o = ot.T
        o_ref[0, pl.ds(row0, ROW_TILE), :] = o.astype(_BF)

    def load_q(t):
        return q_blocks(pl.multiple_of(CTX_LEN + t * ROW_TILE, ROW_TILE))

    def score_chunk(t, qb, c, n):
        return scores(qb, c * KEY_CHUNK, n * KEY_CHUNK)

    def value_chunk(t, c, e):
        return _pv_t(vt_ref[0, 0, :, c * KEY_CHUNK:(c + 1) * KEY_CHUNK], e, not diff)

    def finalize(t, acc, ls):
        finalize_rows(pl.multiple_of(o_off + t * ROW_TILE, ROW_TILE), acc, ls)

    _pipelined_tiles(load_q, score_chunk, value_chunk, finalize, N_KEY_CHUNKS, ((s0, m0), (s1, m1)),
                     Q2 if diff else ROW_TILE)

    if ctx_tile:
        s = scores(q_blocks(0), 0, CTX_LEN)
        e = jnp.exp2(s - jnp.max(s, axis=0, keepdims=True))
        finalize_rows(0, _pv_t(vt_ref[0, 0, :, 0:CTX_LEN], e.astype(_BF), not diff),
                      jnp.sum(e, axis=0, keepdims=True))


def _na_body(qt_ref, k_ref, vt_ref, bias_ref, o_ref, s0, s1, m0, m1):
    def key_row0(t, c):
        first = jnp.clip(t - 1, 0, N_LAT_TILES - 3)
        return 0 if c == 0 else pl.multiple_of(CTX_LEN + (first + c - 1) * KEY_CHUNK, KEY_CHUNK)

    def load_q(t):
        return _split_q_cols(qt_ref[0, 0, :, pl.ds(pl.multiple_of(CTX_LEN + t * ROW_TILE, ROW_TILE), ROW_TILE)])

    def score_chunk(t, qb, c, n):
        s = _dot(k_ref[0, 0, pl.ds(key_row0(t, c), KEY_CHUNK), :], qb)
        if c > 0:
            kind = jnp.where(t == 0, 0, jnp.where(t == N_LAT_TILES - 1, 2, 1))
            s = s + bias_ref[0, kind, (c - 1) * KEY_CHUNK:c * KEY_CHUNK, :]
        return s

    def value_chunk(t, c, e):
        return _pv_t(vt_ref[0, 0, :, pl.ds(key_row0(t, c), KEY_CHUNK)], e, True)

    def finalize(t, acc, ls):
        o = _normalise_t(acc, ls, True).T
        o_ref[0, pl.ds(pl.multiple_of(t * ROW_TILE, ROW_TILE), ROW_TILE), :] = o.astype(_BF)

    _pipelined_tiles(load_q, score_chunk, value_chunk, finalize, 1 + NA_KEYS // KEY_CHUNK,
                     ((s0, m0), (s1, m1)), ROW_TILE)


def _attn_call(body, name, in_specs, arrays, out_rows, n_keys):
    return pl.pallas_call(
        body,
        grid=(BATCH, 4),
        in_specs=in_specs,
        out_specs=pl.BlockSpec((1, out_rows, LANES), lambda b, g: (b, 0, g)),
        out_shape=jax.ShapeDtypeStruct((BATCH, out_rows, 4 * LANES), _BF),
        scratch_shapes=[pltpu.VMEM((n_keys, Q2), _F32), pltpu.VMEM((n_keys, Q2), _F32),
                        pltpu.VMEM((8, Q2), _F32), pltpu.VMEM((8, Q2), _F32)],
        compiler_params=pltpu.CompilerParams(vmem_limit_bytes=VMEM_LIMIT),
        name=name,
    )(*arrays)


def _t_spec(f):
    return pl.BlockSpec((1, 1, LANES, S_ALL), f)


def _n_spec(f):
    return pl.BlockSpec((1, 1, S_ALL, LANES), f)


def _diff_attention(qdt, kd, vdt, lamv, g_subln):
    head = lambda b, g: (b, g, 0, 0)
    body = functools.partial(_dense_body, n_q=1, diff=True, ctx_tile=True, o_off=CTX_LEN)
    return _attn_call(body, "diff_attention",
                      [_t_spec(head), _n_spec(head), _t_spec(head),
                       pl.BlockSpec((4, A_HEAD_DIM), lambda b, g: (0, 0)),
                       pl.BlockSpec((1, LANES), lambda b, g: (0, 0))],
                      (qdt, kd, vdt, lamv, g_subln.reshape(1, -1)), S_ALL, S_ALL)


def _mla_attention(qmt, km, vmt):
    ha = lambda b, g: (b, 2 * g, 0, 0)
    hb = lambda b, g: (b, 2 * g + 1, 0, 0)
    body = functools.partial(_dense_body, n_q=2, diff=False, ctx_tile=True, o_off=CTX_LEN)
    return _attn_call(body, "mla_attention",
                      [_t_spec(ha), _t_spec(hb), _n_spec(ha), _n_spec(hb),
                       _t_spec(lambda b, g: (b, g, 0, 0))],
                      (qmt, qmt, km, km, vmt), S_ALL, S_ALL)


def _gqa_attention(qct, kc, vct):
    shared = lambda b, g: (b, 0, 0, 0)
    body = functools.partial(_dense_body, n_q=1, diff=False, ctx_tile=False, o_off=0)
    return _attn_call(body, "gqa_attention",
                      [_t_spec(lambda b, g: (b, g, 0, 0)), _n_spec(shared), _t_spec(shared)],
                      (qct, kc, vct), SEQ, S_ALL)


def _na_attention(qnt, kn, vnt, bias_t):
    head = lambda b, g: (b, g, 0, 0)
    return _attn_call(_na_body, "na_attention",
                      [_t_spec(head), _n_spec(head), _t_spec(head),
                       pl.BlockSpec((1, 3, NA_KEYS, Q2), lambda b, g: (g, 0, 0, 0))],
                      (qnt, kn, vnt, bias_t), SEQ, CTX_LEN + NA_KEYS)


def _merge_ffn_rows(x, oa_ref, ob_ref, m, woa_ref, wob_ref, g_ref, w1_ref, w3_ref, w2_ref):
    stack = lambda r: r[...].reshape(NB * ROW_TILE, r.shape[-1])
    rows = lambda v, i: v[i * ROW_TILE:(i + 1) * ROW_TILE]
    attn = _dot(stack(oa_ref), woa_ref[...]) + _dot(stack(ob_ref), wob_ref[...])
    x1 = [x[i] + m[i, 2:3] * rows(attn, i) for i in range(NB)]
    h = _modulated_rows(x1, m, g_ref[...], 3, 4)
    a = _dot(h, w1_ref[...])
    b = _dot(h, w3_ref[...])
    u = (a * (1.0 / (1.0 + jnp.exp(-a))) * b).astype(_BF)
    f = _dot(u, w2_ref[...])
    return [x1[i] + m[i, 5:6] * rows(f, i) for i in range(NB)]


def _merge_ffn_proj_body(x_ref, ctx_ref, oa_ref, ob_ref, mod_ref, woa_ref, wob_ref, g_ref,
                         w1_ref, w3_ref, w2_ref, mod1_ref, *rest):
    proj_refs, o_ref, head_refs = rest[:10], rest[10], rest[11:]
    x2 = _merge_ffn_rows(_stream_tile(x_ref, ctx_ref), oa_ref, ob_ref, mod_ref[...],
                         woa_ref, wob_ref, g_ref, w1_ref, w3_ref, w2_ref)
    for i in range(NB):
        o_ref[i] = x2[i]
    _proj1_rows(x2, mod1_ref[...], *proj_refs, *head_refs)


def _merge_ffn_final_body(x_ref, oa_ref, ob_ref, mod_ref, woa_ref, wob_ref, g_ref,
                          w1_ref, w3_ref, w2_ref, gf_ref, o_ref):
    x2 = _merge_ffn_rows(x_ref[...], oa_ref, ob_ref, mod_ref[...],
                         woa_ref, wob_ref, g_ref, w1_ref, w3_ref, w2_ref)
    for i in range(NB):
        o_ref[i] = _rms(x2[i]) * gf_ref[...]


def _ffn_specs(woa, wob, w1, w3, w2, off):
    row = lambda w, o: pl.BlockSpec((NB, ROW_TILE, w), lambda b, t: (b, t + o, 0))
    mod = pl.BlockSpec((NB, 6, D_MODEL), lambda b, t: (_mod_row(b, t + off), 0, 0))
    return [row(512, 0), row(512, 0), mod,
            _const_spec(woa.shape), _const_spec(wob.shape), _const_spec((1, D_MODEL)),
            _const_spec(w1.shape), _const_spec(w3.shape), _const_spec(w2.shape)]


def _merge_ffn_project1(x, ctx, oa, ob, mods, woa, wob, g_ffn, w1, w3, w2, proj_operands):
    p_specs, p_arrays, p_outs = proj_operands
    return pl.pallas_call(
        _merge_ffn_proj_body,
        grid=(BATCH // NB, N_TILES),
        in_specs=_stream_specs() + _ffn_specs(woa, wob, w1, w3, w2, 0) + p_specs,
        out_specs=[_row_spec(D_MODEL)] + [o[1] for o in p_outs],
        out_shape=[jax.ShapeDtypeStruct((BATCH, S_ALL, D_MODEL), _F32)] + [o[0] for o in p_outs],
        compiler_params=pltpu.CompilerParams(vmem_limit_bytes=VMEM_LIMIT),
        name="merge_ffn_project1",
    )(x, ctx, oa, ob, mods, woa, wob, g_ffn.reshape(1, -1), w1, w3, w2, *p_arrays)


def _merge_ffn_final(xs, oa, ob, mods, woa, wob, g_ffn, w1, w3, w2, g_final):
    row = lambda w, o: pl.BlockSpec((NB, ROW_TILE, w), lambda b, t: (b, t + o, 0))
    return pl.pallas_call(
        _merge_ffn_final_body,
        grid=(BATCH // NB, N_LAT_TILES),
        in_specs=[row(D_MODEL, 1)] + _ffn_specs(woa, wob, w1, w3, w2, 1) + [_const_spec((1, D_MODEL))],
        out_specs=row(D_MODEL, 0),
        out_shape=jax.ShapeDtypeStruct((BATCH, SEQ, D_MODEL), _F32),
        compiler_params=pltpu.CompilerParams(vmem_limit_bytes=VMEM_LIMIT),
        name="merge_ffn_final",
    )(xs, oa, ob, mods, woa, wob, g_ffn.reshape(1, -1), w1, w3, w2, g_final.reshape(1, -1))


def _rope_tables():
    t = np.arange(SEQ)
    rows = jnp.asarray(t // GRID_W, _F32)
    cols = jnp.asarray(t % GRID_W, _F32)

    def cs(rot_dim):
        axis_dim = rot_dim // 2
        inv = ROPE_THETA ** (-jnp.arange(0, axis_dim, 2, dtype=_F32) / axis_dim)
        ang = jnp.concatenate([rows[:, None] * inv, cols[:, None] * inv], axis=-1)
        return jnp.cos(ang), jnp.sin(ang)

    def with_ctx(tab, fill):
        return jnp.concatenate([jnp.full((CTX_LEN, LANES), fill, _F32), tab], axis=0)

    cos, sin = cs(A_HEAD_DIM)
    ca = with_ctx(jnp.tile(cos, (1, 4)), 1.0)
    sa = with_ctx(jnp.tile(jnp.concatenate([-sin, sin], axis=-1), (1, 2)), 0.0)
    cos, sin = cs(B_ROPE_DIM)
    ones = jnp.ones((SEQ, 64), _F32)
    zeros = jnp.zeros((SEQ, 64), _F32)
    cb = with_ctx(jnp.concatenate([ones, cos, cos, ones[:, :32]], axis=-1), 1.0)
    sb = with_ctx(jnp.concatenate([zeros, -sin, sin, zeros[:, :32]], axis=-1), 0.0)
    return ca, sa, cb, sb


def _na_bias_table(rpb):
    rows_n = SEQ // GRID_W
    span = 2 * GRID_W - 1
    left = GRID_W - NA_WIN_W
    p = jnp.pad(rpb * LOG2E, ((0, 0), (0, 0), (left, span - left - (2 * NA_WIN_W - 1))))
    flat = jnp.tile(p[:, :, ::-1], (1, 1, GRID_W + 1))[..., :GRID_W * (span + 1)]
    toe = flat.reshape(D_HEADS, 2 * NA_WIN_H - 1, GRID_W, span + 1)[:, :, ::-1, :GRID_W]
    kc = np.arange(GRID_W)[:, None]
    qc = np.arange(GRID_W)[None, :]
    cs = np.clip(qc - NA_WIN_W // 2, 0, GRID_W - NA_WIN_W)
    toe = jnp.where((kc >= cs) & (kc < cs + NA_WIN_W), toe, NEG_BIG)
    kinds = [(0, 0), (ROWS_PER_TILE, 0), (rows_n - ROWS_PER_TILE, rows_n - NA_KEY_ROWS)]
    d_first = [k0 - q0 + NA_WIN_H - 1 for q0, k0 in kinds]
    d_lo, d_hi = min(d_first), max(d_first) + NA_KEY_ROWS
    n_dr = 2 * NA_WIN_H - 1
    before, after = ROWS_PER_TILE - 1 - d_lo, d_hi - n_dr
    fill = lambda n: jnp.full((D_HEADS, n, GRID_W, GRID_W), NEG_BIG, _F32)
    tp = jnp.concatenate([fill(before), toe, fill(after)], axis=1).reshape(4, 2, before + n_dr + after, GRID_W, GRID_W)
    quads = jnp.concatenate([tp[:, w, ROWS_PER_TILE - 1 - qr:ROWS_PER_TILE - 1 - qr + d_hi - d_lo]
                             for w in (0, 1) for qr in range(ROWS_PER_TILE)], axis=-1)

    tiles = []
    for (q0, k0), d0 in zip(kinds, d_first):
        rs = np.clip(q0 + np.arange(ROWS_PER_TILE) - NA_WIN_H // 2, 0, rows_n - NA_WIN_H)[None, :]
        krow = (k0 + np.arange(NA_KEY_ROWS))[:, None]
        inside = (krow >= rs) & (krow < rs + NA_WIN_H)
        inside = np.tile(np.repeat(np.repeat(inside, GRID_W, axis=0), GRID_W, axis=1), (1, 2))
        block = quads[:, d0 - d_lo:d0 - d_lo + NA_KEY_ROWS].reshape(4, NA_KEYS, Q2)
        tiles.append(jnp.where(inside, block, NEG_BIG))
    return jnp.stack(tiles, axis=1)


def _pad_cols(w, groups, width, total, offset=0):
    k = w.shape[0]
    w = w.reshape(k, groups, width)
    w = jnp.pad(w, ((0, 0), (0, 0), (offset, total - width - offset)))
    return w.reshape(k, groups * total)


def kernel(x, c, ctx, c_ctx, l0_w_mod, l0_b_mod, l0_g_attn, l0_w_in, l0_lam_q1, l0_lam_k1, l0_lam_q2, l0_lam_k2, l0_g_subln, l0_g_cq, l0_w_uq, l0_g_ckv, l0_w_ukv, l0_w_out, l0_g_ffn, l0_w1, l0_w3, l0_w2, l1_w_mod, l1_b_mod, l1_g_attn, l1_w_in, l1_g_qc, l1_g_kc, l1_rpb, l1_w_out, l1_g_ffn, l1_w1, l1_w3, l1_w2, g_final):
    bf = lambda w: w.astype(_BF)
    cond = jnp.concatenate([c, jnp.tile(c_ctx[None], (NB, 1)),
                            jnp.zeros((8 - BATCH - NB, D_MODEL), _F32)], axis=0)
    tabs = _rope_tables()
    tabs_t = [t.T for t in tabs]

    mods0 = _ada_params(cond, l0_w_mod, l0_b_mod)
    w_in0 = bf(jnp.concatenate([l0_w_in[:, :1920],
                                _pad_cols(l0_w_in[:, 1920:], 1, B_ROPE_DIM, LANES, B_NOPE_DIM)], axis=1))
    w_uq = bf(_pad_cols(l0_w_uq, B_HEADS, B_NOPE_DIM + B_ROPE_DIM, LANES))
    ukv = l0_w_ukv.reshape(B_KV_RANK, B_HEADS, B_NOPE_DIM + B_V_DIM)
    w_ukv = bf(jnp.concatenate([_pad_cols(ukv[:, :, :B_NOPE_DIM].reshape(B_KV_RANK, -1), B_HEADS, B_NOPE_DIM, LANES),
                                ukv[:, :, B_NOPE_DIM:].reshape(B_KV_RANK, -1)], axis=1))
    qdt, kd, vdt, qmt, km, vmt = _project0(x, ctx, mods0, l0_g_attn, w_in0, tabs, tabs_t,
                                           l0_g_cq, w_uq, l0_g_ckv, w_ukv)
    lamv = jnp.stack([l0_lam_q1, l0_lam_k1, l0_lam_q2, l0_lam_k2])
    o_a = _diff_attention(qdt, kd, vdt, lamv, l0_g_subln)
    o_b = _mla_attention(qmt, km, vmt)
    w_out0 = bf(l0_w_out)

    mods1 = _ada_params(cond, l1_w_mod, l1_b_mod)
    wq = l1_w_in[:, :512].reshape(D_MODEL, 2, 4, C_HEAD_DIM).transpose(0, 2, 1, 3).reshape(D_MODEL, 512)
    w_in1 = bf(jnp.concatenate([wq, l1_w_in[:, 512:]], axis=1))
    grp = np.arange(LANES) // C_HEAD_DIM
    bd = jnp.asarray((grp[:, None] == grp[None, :]) / C_HEAD_DIM, _BF)
    xs, qct, kc, vct, qnt, kn, vnt = _merge_ffn_project1(
        x, ctx, o_a, o_b, mods0, w_out0[:512], w_out0[512:], l0_g_ffn, bf(l0_w1), bf(l0_w3), bf(l0_w2),
        _project1_operands(mods1, l1_g_attn, w_in1, tabs, tabs_t, bd, l1_g_qc, l1_g_kc))
    o_c = _gqa_attention(qct, kc, vct)
    o_d = _na_attention(qnt, kn, vnt, _na_bias_table(l1_rpb))
    w_out1 = bf(l1_w_out)
    woc = w_out1[:512].reshape(2, 4, C_HEAD_DIM, D_MODEL).transpose(1, 0, 2, 3).reshape(512, D_MODEL)
    return _merge_ffn_final(xs, o_c, o_d, mods1, woc, w_out1[512:], l1_g_ffn,
                            bf(l1_w1), bf(l1_w3), bf(l1_w2), g_final)
```

```python
import functools
import math

import numpy as np
import jax
import jax.numpy as jnp
from jax import lax
from jax.experimental import pallas as pl
from jax.experimental.pallas import tpu as pltpu

D_MODEL = 1024
BATCH = 4
SEQ = 4096
GRID_W = 64
CTX_LEN = 256
ROPE_THETA = 10000.0
EPS = 1e-6
A_HEADS = 4
A_HEAD_DIM = 64
B_HEADS = 8
B_Q_RANK = 256
B_KV_RANK = 128
B_NOPE_DIM = 64
B_ROPE_DIM = 32
B_V_DIM = 64
C_HEADS = 8
C_KV_HEADS = 2
C_HEAD_DIM = 64
D_HEADS = 8
D_HEAD_DIM = 64
NA_WIN_H = 8
NA_WIN_W = 16
FFN_HIDDEN = 2816
LAMBDA_INIT = 0.8 - 0.6 * math.exp(-0.3 * 0)

S_ALL = CTX_LEN + SEQ
LANES = 128
HALF = LANES // 2
ROW_TILE = 256
NB = 2
N_TILES = S_ALL // ROW_TILE
N_LAT_TILES = SEQ // ROW_TILE
Q2 = 2 * ROW_TILE
KEY_CHUNK = 256
N_KEY_CHUNKS = S_ALL // KEY_CHUNK
SUM_ROWS = 16
ROWS_PER_TILE = ROW_TILE // GRID_W
NA_KEY_ROWS = 12
NA_KEYS = NA_KEY_ROWS * GRID_W
LOG2E = math.log2(math.e)
NEG_BIG = -1e30
VMEM_LIMIT = 52 * 1024 * 1024

assert ROW_TILE == CTX_LEN and S_ALL % ROW_TILE == 0 and N_LAT_TILES % 2 == 0

_BF = jnp.bfloat16
_F32 = jnp.float32


def _dot(a, b):
    return jnp.dot(a, b, preferred_element_type=_F32)


def _rms(x):
    return x * lax.rsqrt(jnp.mean(x * x, axis=-1, keepdims=True) + EPS)


def _lane(shape):
    return lax.broadcasted_iota(jnp.int32, shape, len(shape) - 1)


def _swap_halves(x, half):
    n = x.shape[-1]
    up = pltpu.roll(x, n - half, axis=1)
    down = pltpu.roll(x, half, axis=1)
    return jnp.where((_lane(x.shape) % (2 * half)) < half, up, down)


def _rope(x, c, s, half):
    return x * c + _swap_halves(x, half) * s


def _rope_t(x, c, s, half):
    parts = []
    for r in range(0, x.shape[0], 2 * half):
        parts += [x[r + half:r + 2 * half], x[r:r + half]]
    return x * c + jnp.concatenate(parts, axis=0) * s


def _dot_nt(a, b):
    return lax.dot_general(a, b, (((1,), (1,)), ((), ())), preferred_element_type=_F32)


def _group_mean_sq(x, bd):
    x2 = x * x
    hi = x2.astype(_BF)
    lo = (x2 - hi.astype(_F32)).astype(_BF)
    return _dot(hi, bd) + _dot(lo, bd)


def _ada_body(cond_ref, w_ref, b_ref, o_ref):
    c = cond_ref[...]
    a = (c * (1.0 / (1.0 + jnp.exp(-c)))).astype(_BF)
    o_ref[...] = _dot(a, w_ref[...].astype(_BF)) + b_ref[...]


def _ada_params(cond, w_mod, b_mod):
    n = w_mod.shape[1]
    tn = n // 4
    out = pl.pallas_call(
        _ada_body,
        grid=(n // tn,),
        in_specs=[pl.BlockSpec((8, D_MODEL), lambda j: (0, 0)),
                  pl.BlockSpec((D_MODEL, tn), lambda j: (0, j)),
                  pl.BlockSpec((1, tn), lambda j: (0, j))],
        out_specs=pl.BlockSpec((8, tn), lambda j: (0, j)),
        out_shape=jax.ShapeDtypeStruct((8, n), _F32),
        compiler_params=pltpu.CompilerParams(vmem_limit_bytes=VMEM_LIMIT),
        name="ada_params",
    )(cond, w_mod, b_mod.reshape(1, n))
    return out.reshape(8, 6, D_MODEL)


def _mod_row(b, t):
    return jnp.where(t == 0, BATCH // NB, b)


def _const_spec(shape):
    nd = len(shape)
    return pl.BlockSpec(shape, lambda *_: (0,) * nd, pipeline_mode=pl.Buffered(1))


def _modulated(x, m, g, shift_row, scale_row):
    h = _rms(x) * g
    return h * (1.0 + m[scale_row:scale_row + 1]) + m[shift_row:shift_row + 1]


def _modulated_rows(xs, m, g, shift_row, scale_row):
    return jnp.concatenate([_modulated(xs[i], m[i], g, shift_row, scale_row) for i in range(NB)],
                           axis=0).astype(_BF)


def _stream_tile(x_ref, ctx_ref):
    return jnp.where(pl.program_id(1) == 0, ctx_ref[...], x_ref[...])


def _stream_specs():
    return [pl.BlockSpec((NB, ROW_TILE, D_MODEL), lambda b, t: (b, jnp.maximum(t - 1, 0), 0)),
            pl.BlockSpec((NB, CTX_LEN, D_MODEL), lambda b, t: (b, 0, 0))]


def _proj0_body(x_ref, ctx_ref, mod_ref, g_ref, wn_ref, wt_ref,
                ca_ref, sa_ref, cb_ref, sb_ref, cat_ref, sat_ref, cbt_ref, sbt_ref,
                g_cq_ref, w_uqt_ref, g_ckv_ref, w_uk_ref, w_uvt_ref,
                qdt_ref, kd_ref, vdt_ref, qmt_ref, km_ref, vmt_ref):
    h = _modulated_rows(_stream_tile(x_ref, ctx_ref), mod_ref[...], g_ref[...], 0, 1)
    yn = _dot(h, wn_ref[...])
    yt = _dot_nt(wt_ref[...], h)
    cq = (_rms(yn[:, 512:768]) * g_cq_ref[...]).astype(_BF)
    ckv = (_rms(yn[:, 768:896]) * g_ckv_ref[...]).astype(_BF)
    qmt = _dot_nt(w_uqt_ref[...], cq)
    kn = _dot(ckv, w_uk_ref[...])
    vmt = _dot_nt(w_uvt_ref[...], ckv)
    ca, sa, cb, sb = ca_ref[...], sa_ref[...], cb_ref[...], sb_ref[...]
    cat, sat, cbt, sbt = cat_ref[...], sat_ref[...], cbt_ref[...], sbt_ref[...]
    qscale = A_HEAD_DIM ** -0.5 * LOG2E
    mscale = (B_NOPE_DIM + B_ROPE_DIM) ** -0.5 * LOG2E
    for i in range(NB):
        rows = slice(i * ROW_TILE, (i + 1) * ROW_TILE)
        for hd in range(A_HEADS):
            lo = hd * LANES
            qdt_ref[i, hd] = (_rope_t(yt[lo:lo + LANES, rows], cat, sat, A_HEAD_DIM // 2) * qscale).astype(_BF)
            kd_ref[i, hd] = _rope(yn[rows, lo:lo + LANES], ca, sa, A_HEAD_DIM // 2).astype(_BF)
            vdt_ref[i, hd] = yt[512 + lo:512 + lo + LANES, rows].astype(_BF)
        kr = _rope(yn[rows, 896:1024], cb, sb, B_ROPE_DIM // 2)
        for hd in range(B_HEADS):
            lo = hd * LANES
            qmt_ref[i, hd] = (_rope_t(qmt[lo:lo + LANES, rows], cbt, sbt, B_ROPE_DIM // 2) * mscale).astype(_BF)
            km_ref[i, hd] = (kn[rows, lo:lo + LANES] + kr).astype(_BF)
        for pr in range(B_HEADS // 2):
            vmt_ref[i, pr] = vmt[pr * LANES:(pr + 1) * LANES, rows].astype(_BF)


def _head_out(n_heads, transposed):
    if transposed:
        return (jax.ShapeDtypeStruct((BATCH, n_heads, LANES, S_ALL), _BF),
                pl.BlockSpec((NB, n_heads, LANES, ROW_TILE), lambda b, t: (b, 0, 0, t)))
    return (jax.ShapeDtypeStruct((BATCH, n_heads, S_ALL, LANES), _BF),
            pl.BlockSpec((NB, n_heads, ROW_TILE, LANES), lambda b, t: (b, 0, t, 0)))


def _row_spec(width):
    return pl.BlockSpec((NB, ROW_TILE, width), lambda b, t: (b, t, 0))


def _mod_spec():
    return pl.BlockSpec((NB, 6, D_MODEL), lambda b, t: (_mod_row(b, t), 0, 0))


def _table_spec():
    return pl.BlockSpec((ROW_TILE, LANES), lambda b, t: (t, 0))


def _table_t_spec():
    return pl.BlockSpec((LANES, ROW_TILE), lambda b, t: (0, t))


def _project0(x, ctx, mods, g_attn, w_in, tabs, tabs_t, g_cq, w_uq, g_ckv, w_ukv):
    wn = jnp.concatenate([w_in[:, 512:1024], w_in[:, 1536:2048]], axis=1)
    wt = jnp.concatenate([w_in[:, 0:512], w_in[:, 1024:1536]], axis=1).T
    w_uqt = w_uq.T
    w_uk, w_uvt = w_ukv[:, :B_HEADS * LANES], w_ukv[:, B_HEADS * LANES:].T
    outs = [_head_out(4, True), _head_out(4, False), _head_out(4, True),
            _head_out(8, True), _head_out(8, False), _head_out(4, True)]
    return pl.pallas_call(
        _proj0_body,
        grid=(BATCH // NB, N_TILES),
        in_specs=_stream_specs() + [_mod_spec(), _const_spec((1, D_MODEL)),
                  _const_spec(wn.shape), _const_spec(wt.shape)]
                 + [_table_spec()] * 4 + [_table_t_spec()] * 4
                 + [_const_spec((1, B_Q_RANK)), _const_spec(w_uqt.shape),
                    _const_spec((1, B_KV_RANK)), _const_spec(w_uk.shape), _const_spec(w_uvt.shape)],
        out_specs=[o[1] for o in outs],
        out_shape=[o[0] for o in outs],
        compiler_params=pltpu.CompilerParams(vmem_limit_bytes=VMEM_LIMIT),
        name="project0",
    )(x, ctx, mods, g_attn.reshape(1, -1), wn, wt, *tabs, *tabs_t,
      g_cq.reshape(1, -1), w_uqt, g_ckv.reshape(1, -1), w_uk, w_uvt)


def _proj1_body(x_ref, mod_ref, g_ref, wn_ref, wt_ref, ca_ref, sa_ref, cat_ref, sat_ref, bd_ref,
                g_qc_ref, g_kc_ref,
                qct_ref, kc_ref, vct_ref, qnt_ref, kn_ref, vnt_ref):
    h = _modulated_rows(x_ref[...], mod_ref[...], g_ref[...], 0, 1)
    yn = _dot(h, wn_ref[...])
    yt = _dot_nt(wt_ref[...], h)
    ca, sa, cat, sat = ca_ref[...], sa_ref[...], cat_ref[...], sat_ref[...]
    qscale = C_HEAD_DIM ** -0.5 * LOG2E
    nscale = D_HEAD_DIM ** -0.5 * LOG2E

    kc_all = yn[:, 0:LANES]
    kc_all = kc_all * lax.rsqrt(_group_mean_sq(kc_all, bd_ref[...]) + EPS) * g_kc_ref[...]
    q3 = yt[0:512].reshape(C_HEADS, C_HEAD_DIM, NB * ROW_TILE)
    q3 = q3 * lax.rsqrt(jnp.mean(q3 * q3, axis=1, keepdims=True) + EPS)
    qc_all = q3.reshape(512, NB * ROW_TILE) * g_qc_ref[...]
    for i in range(NB):
        rows = slice(i * ROW_TILE, (i + 1) * ROW_TILE)
        kc_ref[i, 0] = _rope(kc_all[rows], ca, sa, C_HEAD_DIM // 2).astype(_BF)
        vct_ref[i, 0] = yt[512:640, rows].astype(_BF)
        for pr in range(4):
            lo = pr * LANES
            qct_ref[i, pr] = (_rope_t(qc_all[lo:lo + LANES, rows], cat, sat, C_HEAD_DIM // 2) * qscale).astype(_BF)
            qnt_ref[i, pr] = (yt[640 + lo:640 + lo + LANES, rows] * nscale).astype(_BF)
            kn_ref[i, pr] = yn[rows, LANES + lo:2 * LANES + lo].astype(_BF)
            vnt_ref[i, pr] = yt[1152 + lo:1152 + lo + LANES, rows].astype(_BF)


def _project1(xs, mods, g_attn, w_in, tabs, tabs_t, bd, g_qc, g_kc):
    wn = jnp.concatenate([w_in[:, 512:640], w_in[:, 1280:1792]], axis=1)
    wt = jnp.concatenate([w_in[:, 0:512], w_in[:, 640:1280], w_in[:, 1792:2304]], axis=1).T
    outs = [_head_out(4, True), _head_out(1, False), _head_out(1, True),
            _head_out(4, True), _head_out(4, False), _head_out(4, True)]
    return pl.pallas_call(
        _proj1_body,
        grid=(BATCH // NB, N_TILES),
        in_specs=[_row_spec(D_MODEL), _mod_spec(), _const_spec((1, D_MODEL)),
                  _const_spec(wn.shape), _const_spec(wt.shape),
                  _table_spec(), _table_spec(), _table_t_spec(), _table_t_spec(),
                  _const_spec(bd.shape),
                  _const_spec((512, 1)), _const_spec((1, LANES))],
        out_specs=[o[1] for o in outs],
        out_shape=[o[0] for o in outs],
        compiler_params=pltpu.CompilerParams(vmem_limit_bytes=VMEM_LIMIT),
        name="project1",
    )(xs, mods, g_attn.reshape(1, -1), wn, wt, tabs[0], tabs[1], tabs_t[0], tabs_t[1], bd,
      jnp.tile(g_qc, 8).reshape(-1, 1), jnp.tile(g_kc, 2).reshape(1, -1))


def _split_q_cols(qt):
    z = jnp.zeros((HALF, qt.shape[1]), qt.dtype)
    return jnp.concatenate([jnp.concatenate([qt[:HALF], z], axis=0),
                            jnp.concatenate([z, qt[HALF:]], axis=0)], axis=1)


def _with_sum_rows(vt):
    n = vt.shape[1]
    return jnp.concatenate([vt, jnp.ones((1, n), vt.dtype), jnp.zeros((SUM_ROWS - 1, n), vt.dtype)], axis=0)


def _pv_t(vt, e, split):
    if not split:
        return _dot(_with_sum_rows(vt), e)
    return jnp.concatenate([_dot(_with_sum_rows(vt[:HALF]), e[:, :ROW_TILE]),
                            _dot(_with_sum_rows(vt[HALF:]), e[:, ROW_TILE:])], axis=0)


def _normalise_t(acc, split):
    if not split:
        return acc[:LANES] / acc[LANES:LANES + 1]
    hb = HALF + SUM_ROWS
    return jnp.concatenate([acc[:HALF] / acc[HALF:HALF + 1],
                            acc[hb:hb + HALF] / acc[hb + HALF:hb + HALF + 1]], axis=0)


def _pipelined_tiles(load_q, score_chunk, value_chunk, finalize, n_chunks, bufs, split, score_group=1):
    acc_shape = (LANES + 2 * SUM_ROWS, ROW_TILE) if split else (LANES + SUM_ROWS, Q2)

    def step(t_next, nxt, t_cur, cur):
        mx = acc = None
        if t_next is not None:
            qb = load_q(t_next)
        if t_cur is not None:
            m_cur = cur[1][0:1, :]
            acc = jnp.zeros(acc_shape, _F32)
        for c in range(n_chunks):
            rows = slice(c * KEY_CHUNK, (c + 1) * KEY_CHUNK)
            if t_next is not None and c % score_group == 0:
                n = min(score_group, n_chunks - c)
                s = score_chunk(t_next, qb, c, n)
                nxt[0][c * KEY_CHUNK:(c + n) * KEY_CHUNK, :] = s
                cm = jnp.max(s, axis=0, keepdims=True)
                mx = cm if mx is None else jnp.maximum(mx, cm)
            if t_cur is not None:
                e = jnp.exp2(cur[0][rows, :] - m_cur)
                acc = acc + value_chunk(t_cur, c, e.astype(_BF))
        if t_next is not None:
            nxt[1][...] = jnp.broadcast_to(mx, nxt[1].shape)
        return acc

    last = N_LAT_TILES - 1
    step(0, bufs[0], None, None)

    def pair(j, carry):
        finalize(jnp.maximum(2 * j - 1, 0), carry)
        finalize(2 * j, step(2 * j + 1, bufs[1], 2 * j, bufs[0]))
        return step(2 * j + 2, bufs[0], 2 * j + 1, bufs[1])

    carry = lax.fori_loop(0, N_LAT_TILES // 2 - 1, pair, jnp.ones(acc_shape, _F32))
    finalize(last - 2, carry)
    finalize(last - 1, step(last, bufs[1], last - 1, bufs[0]))
    finalize(last, step(None, None, last, bufs[1]))


def _dense_body(*refs, n_q, diff, ctx_tile, o_off):
    q_refs, k_refs, vt_ref = refs[:n_q], refs[n_q:2 * n_q], refs[2 * n_q]
    rest = refs[2 * n_q + 1:]
    if diff:
        lam_ref, g_ref, o_ref, s0, s1, m0, m1 = rest
        lv = lam_ref[...]
        lam = (jnp.exp(jnp.sum(lv[0:1] * lv[1:2], axis=-1, keepdims=True))
               - jnp.exp(jnp.sum(lv[2:3] * lv[3:4], axis=-1, keepdims=True)) + LAMBDA_INIT)
    else:
        o_ref, s0, s1, m0, m1 = rest

    def q_blocks(col0):
        if n_q == 1:
            return [_split_q_cols(q_refs[0][0, 0, :, pl.ds(col0, ROW_TILE)])]
        return [r[0, 0, :, pl.ds(col0, ROW_TILE)] for r in q_refs]

    def scores(qb, row0, rows):
        parts = [_dot(k_refs[i][0, 0, pl.ds(row0, rows), :], qb[i]) for i in range(n_q)]
        return parts[0] if n_q == 1 else jnp.concatenate(parts, axis=1)

    def finalize_rows(row0, acc):
        ot = _normalise_t(acc, not diff)
        if diff:
            o = (ot[:, :ROW_TILE] - lam * ot[:, ROW_TILE:]).T
            o = _rms(o) * g_ref[...] * (1.0 - LAMBDA_INIT)
        else:
            o = ot.T
        o_ref[0, pl.ds(row0, ROW_TILE), :] = o.astype(_BF)

    def load_q(t):
        return q_blocks(pl.multiple_of(CTX_LEN + t * ROW_TILE, ROW_TILE))

    def score_chunk(t, qb, c, n):
        return scores(qb, c * KEY_CHUNK, n * KEY_CHUNK)

    def value_chunk(t, c, e):
        return _pv_t(vt_ref[0, 0, :, c * KEY_CHUNK:(c + 1) * KEY_CHUNK], e, not diff)

    def finalize(t, acc):
        finalize_rows(pl.multiple_of(o_off + t * ROW_TILE, ROW_TILE), acc)

    _pipelined_tiles(load_q, score_chunk, value_chunk, finalize, N_KEY_CHUNKS, ((s0, m0), (s1, m1)), not diff,
                     score_group=2)

    if ctx_tile:
        s = scores(q_blocks(0), 0, CTX_LEN)
        e = jnp.exp2(s - jnp.max(s, axis=0, keepdims=True))
        finalize_rows(0, _pv_t(vt_ref[0, 0, :, 0:CTX_LEN], e.astype(_BF), not diff))


def _na_body(qt_ref, k_ref, vt_ref, bias_ref, o_ref, s0, s1, m0, m1):
    def key_row0(t, c):
        first = jnp.clip(t - 1, 0, N_LAT_TILES - 3)
        return 0 if c == 0 else pl.multiple_of(CTX_LEN + (first + c - 1) * KEY_CHUNK, KEY_CHUNK)

    def load_q(t):
        return _split_q_cols(qt_ref[0, 0, :, pl.ds(pl.multiple_of(CTX_LEN + t * ROW_TILE, ROW_TILE), ROW_TILE)])

    def score_chunk(t, qb, c, n):
        s = _dot(k_ref[0, 0, pl.ds(key_row0(t, c), KEY_CHUNK), :], qb)
        if c > 0:
            kind = jnp.where(t == 0, 0, jnp.where(t == N_LAT_TILES - 1, 2, 1))
            s = s + bias_ref[0, kind, (c - 1) * KEY_CHUNK:c * KEY_CHUNK, :]
        return s

    def value_chunk(t, c, e):
        return _pv_t(vt_ref[0, 0, :, pl.ds(key_row0(t, c), KEY_CHUNK)], e, True)

    def finalize(t, acc):
        o = _normalise_t(acc, True).T
        o_ref[0, pl.ds(pl.multiple_of(t * ROW_TILE, ROW_TILE), ROW_TILE), :] = o.astype(_BF)

    _pipelined_tiles(load_q, score_chunk, value_chunk, finalize, 1 + NA_KEYS // KEY_CHUNK,
                     ((s0, m0), (s1, m1)), True)


def _attn_call(body, name, in_specs, arrays, out_rows, n_keys):
    return pl.pallas_call(
        body,
        grid=(BATCH, 4),
        in_specs=in_specs,
        out_specs=pl.BlockSpec((1, out_rows, LANES), lambda b, g: (b, 0, g)),
        out_shape=jax.ShapeDtypeStruct((BATCH, out_rows, 4 * LANES), _BF),
        scratch_shapes=[pltpu.VMEM((n_keys, Q2), _F32), pltpu.VMEM((n_keys, Q2), _F32),
                        pltpu.VMEM((8, Q2), _F32), pltpu.VMEM((8, Q2), _F32)],
        compiler_params=pltpu.CompilerParams(vmem_limit_bytes=VMEM_LIMIT),
        name=name,
    )(*arrays)


def _t_spec(f):
    return pl.BlockSpec((1, 1, LANES, S_ALL), f)


def _n_spec(f):
    return pl.BlockSpec((1, 1, S_ALL, LANES), f)


def _diff_attention(qdt, kd, vdt, lamv, g_subln):
    head = lambda b, g: (b, g, 0, 0)
    body = functools.partial(_dense_body, n_q=1, diff=True, ctx_tile=True, o_off=CTX_LEN)
    return _attn_call(body, "diff_attention",
                      [_t_spec(head), _n_spec(head), _t_spec(head),
                       pl.BlockSpec((4, A_HEAD_DIM), lambda b, g: (0, 0)),
                       pl.BlockSpec((1, LANES), lambda b, g: (0, 0))],
                      (qdt, kd, vdt, lamv, g_subln.reshape(1, -1)), S_ALL, S_ALL)


def _mla_attention(qmt, km, vmt):
    ha = lambda b, g: (b, 2 * g, 0, 0)
    hb = lambda b, g: (b, 2 * g + 1, 0, 0)
    body = functools.partial(_dense_body, n_q=2, diff=False, ctx_tile=True, o_off=CTX_LEN)
    return _attn_call(body, "mla_attention",
                      [_t_spec(ha), _t_spec(hb), _n_spec(ha), _n_spec(hb),
                       _t_spec(lambda b, g: (b, g, 0, 0))],
                      (qmt, qmt, km, km, vmt), S_ALL, S_ALL)


def _gqa_attention(qct, kc, vct):
    shared = lambda b, g: (b, 0, 0, 0)
    body = functools.partial(_dense_body, n_q=1, diff=False, ctx_tile=False, o_off=0)
    return _attn_call(body, "gqa_attention",
                      [_t_spec(lambda b, g: (b, g, 0, 0)), _n_spec(shared), _t_spec(shared)],
                      (qct, kc, vct), SEQ, S_ALL)


def _na_attention(qnt, kn, vnt, bias_t):
    head = lambda b, g: (b, g, 0, 0)
    return _attn_call(_na_body, "na_attention",
                      [_t_spec(head), _n_spec(head), _t_spec(head),
                       pl.BlockSpec((1, 3, NA_KEYS, Q2), lambda b, g: (g, 0, 0, 0))],
                      (qnt, kn, vnt, bias_t), SEQ, CTX_LEN + NA_KEYS)


def _merge_ffn_body(*refs, final):
    x = refs[0][...] if final else _stream_tile(refs[0], refs[1])
    (oa_ref, ob_ref, mod_ref, woa_ref, wob_ref, g_ref,
     w1_ref, w3_ref, w2_ref, gf_ref, o_ref) = refs[1 if final else 2:]
    m = mod_ref[...]
    stack = lambda r: r[...].reshape(NB * ROW_TILE, r.shape[-1])
    rows = lambda v, i: v[i * ROW_TILE:(i + 1) * ROW_TILE]
    attn = _dot(stack(oa_ref), woa_ref[...]) + _dot(stack(ob_ref), wob_ref[...])
    x1 = [x[i] + m[i, 2:3] * rows(attn, i) for i in range(NB)]
    h = _modulated_rows(x1, m, g_ref[...], 3, 4)
    a = _dot(h, w1_ref[...])
    b = _dot(h, w3_ref[...])
    u = (a * (1.0 / (1.0 + jnp.exp(-a))) * b).astype(_BF)
    f = _dot(u, w2_ref[...])
    for i in range(NB):
        x2 = x1[i] + m[i, 5:6] * rows(f, i)
        if final:
            x2 = _rms(x2) * gf_ref[...]
        o_ref[i] = x2


def _merge_ffn(streams, oa, ob, mods, woa, wob, g_ffn, w1, w3, w2, g_final, *, latent_only):
    off = 1 if latent_only else 0
    tiles = N_LAT_TILES if latent_only else N_TILES
    row = lambda w, o: pl.BlockSpec((NB, ROW_TILE, w), lambda b, t: (b, t + o, 0))
    mod = pl.BlockSpec((NB, 6, D_MODEL), lambda b, t: (_mod_row(b, t + off), 0, 0))
    x_specs = [row(D_MODEL, off)] if latent_only else _stream_specs()
    return pl.pallas_call(
        functools.partial(_merge_ffn_body, final=latent_only),
        grid=(BATCH // NB, tiles),
        in_specs=x_specs + [row(512, 0), row(512, 0), mod,
                  _const_spec(woa.shape), _const_spec(wob.shape), _const_spec((1, D_MODEL)),
                  _const_spec(w1.shape), _const_spec(w3.shape), _const_spec(w2.shape),
                  _const_spec((1, D_MODEL))],
        out_specs=row(D_MODEL, 0),
        out_shape=jax.ShapeDtypeStruct((BATCH, tiles * ROW_TILE, D_MODEL), _F32),
        compiler_params=pltpu.CompilerParams(vmem_limit_bytes=VMEM_LIMIT),
        name="merge_ffn_final" if latent_only else "merge_ffn",
    )(*streams, oa, ob, mods, woa, wob, g_ffn.reshape(1, -1), w1, w3, w2, g_final.reshape(1, -1))


def _rope_tables():
    t = np.arange(SEQ)
    rows = jnp.asarray(t // GRID_W, _F32)
    cols = jnp.asarray(t % GRID_W, _F32)

    def cs(rot_dim):
        axis_dim = rot_dim // 2
        inv = ROPE_THETA ** (-jnp.arange(0, axis_dim, 2, dtype=_F32) / axis_dim)
        ang = jnp.concatenate([rows[:, None] * inv, cols[:, None] * inv], axis=-1)
        return jnp.cos(ang), jnp.sin(ang)

    def with_ctx(tab, fill):
        return jnp.concatenate([jnp.full((CTX_LEN, LANES), fill, _F32), tab], axis=0)

    cos, sin = cs(A_HEAD_DIM)
    ca = with_ctx(jnp.tile(cos, (1, 4)), 1.0)
    sa = with_ctx(jnp.tile(jnp.concatenate([-sin, sin], axis=-1), (1, 2)), 0.0)
    cos, sin = cs(B_ROPE_DIM)
    ones = jnp.ones((SEQ, 64), _F32)
    zeros = jnp.zeros((SEQ, 64), _F32)
    cb = with_ctx(jnp.concatenate([ones, cos, cos, ones[:, :32]], axis=-1), 1.0)
    sb = with_ctx(jnp.concatenate([zeros, -sin, sin, zeros[:, :32]], axis=-1), 0.0)
    return ca, sa, cb, sb


def _na_bias_table(rpb):
    rows_n = SEQ // GRID_W
    span = 2 * GRID_W - 1
    left = GRID_W - NA_WIN_W
    p = jnp.pad(rpb * LOG2E, ((0, 0), (0, 0), (left, span - left - (2 * NA_WIN_W - 1))))
    flat = jnp.tile(p[:, :, ::-1], (1, 1, GRID_W + 1))[..., :GRID_W * (span + 1)]
    toe = flat.reshape(D_HEADS, 2 * NA_WIN_H - 1, GRID_W, span + 1)[:, :, ::-1, :GRID_W]
    kc = np.arange(GRID_W)[:, None]
    qc = np.arange(GRID_W)[None, :]
    cs = np.clip(qc - NA_WIN_W // 2, 0, GRID_W - NA_WIN_W)
    toe = jnp.where((kc >= cs) & (kc < cs + NA_WIN_W), toe, NEG_BIG)
    kinds = [(0, 0), (ROWS_PER_TILE, 0), (rows_n - ROWS_PER_TILE, rows_n - NA_KEY_ROWS)]
    d_first = [k0 - q0 + NA_WIN_H - 1 for q0, k0 in kinds]
    d_lo, d_hi = min(d_first), max(d_first) + NA_KEY_ROWS
    n_dr = 2 * NA_WIN_H - 1
    before, after = ROWS_PER_TILE - 1 - d_lo, d_hi - n_dr
    fill = lambda n: jnp.full((D_HEADS, n, GRID_W, GRID_W), NEG_BIG, _F32)
    tp = jnp.concatenate([fill(before), toe, fill(after)], axis=1).reshape(4, 2, before + n_dr + after, GRID_W, GRID_W)
    quads = jnp.concatenate([tp[:, w, ROWS_PER_TILE - 1 - qr:ROWS_PER_TILE - 1 - qr + d_hi - d_lo]
                             for w in (0, 1) for qr in range(ROWS_PER_TILE)], axis=-1)

    tiles = []
    for (q0, k0), d0 in zip(kinds, d_first):
        rs = np.clip(q0 + np.arange(ROWS_PER_TILE) - NA_WIN_H // 2, 0, rows_n - NA_WIN_H)[None, :]
        krow = (k0 + np.arange(NA_KEY_ROWS))[:, None]
        inside = (krow >= rs) & (krow < rs + NA_WIN_H)
        inside = np.tile(np.repeat(np.repeat(inside, GRID_W, axis=0), GRID_W, axis=1), (1, 2))
        block = quads[:, d0 - d_lo:d0 - d_lo + NA_KEY_ROWS].reshape(4, NA_KEYS, Q2)
        tiles.append(jnp.where(inside, block, NEG_BIG))
    return jnp.stack(tiles, axis=1)


def _pad_cols(w, groups, width, total, offset=0):
    k = w.shape[0]
    w = w.reshape(k, groups, width)
    w = jnp.pad(w, ((0, 0), (0, 0), (offset, total - width - offset)))
    return w.reshape(k, groups * total)


def kernel(x, c, ctx, c_ctx, l0_w_mod, l0_b_mod, l0_g_attn, l0_w_in, l0_lam_q1, l0_lam_k1, l0_lam_q2, l0_lam_k2, l0_g_subln, l0_g_cq, l0_w_uq, l0_g_ckv, l0_w_ukv, l0_w_out, l0_g_ffn, l0_w1, l0_w3, l0_w2, l1_w_mod, l1_b_mod, l1_g_attn, l1_w_in, l1_g_qc, l1_g_kc, l1_rpb, l1_w_out, l1_g_ffn, l1_w1, l1_w3, l1_w2, g_final):
    bf = lambda w: w.astype(_BF)
    cond = jnp.concatenate([c, jnp.tile(c_ctx[None], (NB, 1)),
                            jnp.zeros((8 - BATCH - NB, D_MODEL), _F32)], axis=0)
    tabs = _rope_tables()
    tabs_t = [t.T for t in tabs]

    mods0 = _ada_params(cond, l0_w_mod, l0_b_mod)
    w_in0 = bf(jnp.concatenate([l0_w_in[:, :1920],
                                _pad_cols(l0_w_in[:, 1920:], 1, B_ROPE_DIM, LANES, B_NOPE_DIM)], axis=1))
    w_uq = bf(_pad_cols(l0_w_uq, B_HEADS, B_NOPE_DIM + B_ROPE_DIM, LANES))
    ukv = l0_w_ukv.reshape(B_KV_RANK, B_HEADS, B_NOPE_DIM + B_V_DIM)
    w_ukv = bf(jnp.concatenate([_pad_cols(ukv[:, :, :B_NOPE_DIM].reshape(B_KV_RANK, -1), B_HEADS, B_NOPE_DIM, LANES),
                                ukv[:, :, B_NOPE_DIM:].reshape(B_KV_RANK, -1)], axis=1))
    qdt, kd, vdt, qmt, km, vmt = _project0(x, ctx, mods0, l0_g_attn, w_in0, tabs, tabs_t,
                                           l0_g_cq, w_uq, l0_g_ckv, w_ukv)
    lamv = jnp.stack([l0_lam_q1, l0_lam_k1, l0_lam_q2, l0_lam_k2])
    o_a = _diff_attention(qdt, kd, vdt, lamv, l0_g_subln)
    o_b = _mla_attention(qmt, km, vmt)
    w_out0 = bf(l0_w_out)
    xs = _merge_ffn((x, ctx), o_a, o_b, mods0, w_out0[:512], w_out0[512:], l0_g_ffn,
                    bf(l0_w1), bf(l0_w3), bf(l0_w2), g_final, latent_only=False)

    mods1 = _ada_params(cond, l1_w_mod, l1_b_mod)
    wq = l1_w_in[:, :512].reshape(D_MODEL, 2, 4, C_HEAD_DIM).transpose(0, 2, 1, 3).reshape(D_MODEL, 512)
    w_in1 = bf(jnp.concatenate([wq, l1_w_in[:, 512:]], axis=1))
    grp = np.arange(LANES) // C_HEAD_DIM
    bd = jnp.asarray((grp[:, None] == grp[None, :]) / C_HEAD_DIM, _BF)
    qct, kc, vct, qnt, kn, vnt = _project1(xs, mods1, l1_g_attn, w_in1, tabs, tabs_t, bd, l1_g_qc, l1_g_kc)
    o_c = _gqa_attention(qct, kc, vct)
    o_d = _na_attention(qnt, kn, vnt, _na_bias_table(l1_rpb))
    w_out1 = bf(l1_w_out)
    woc = w_out1[:512].reshape(2, 4, C_HEAD_DIM, D_MODEL).transpose(1, 0, 2, 3).reshape(512, D_MODEL)
    return _merge_ffn((xs,), o_c, o_d, mods1, woc, w_out1[512:], l1_g_ffn,
                      bf(l1_w1), bf(l1_w3), bf(l1_w2), g_final, latent_only=True)
```

```python
import functools
import math

import numpy as np
import jax
import jax.numpy as jnp
from jax import lax
from jax.experimental import pallas as pl
from jax.experimental.pallas import tpu as pltpu

D_MODEL = 1024
BATCH = 4
SEQ = 4096
GRID_W = 64
CTX_LEN = 256
ROPE_THETA = 10000.0
EPS = 1e-6
A_HEADS = 4
A_HEAD_DIM = 64
B_HEADS = 8
B_Q_RANK = 256
B_KV_RANK = 128
B_NOPE_DIM = 64
B_ROPE_DIM = 32
B_V_DIM = 64
C_HEADS = 8
C_KV_HEADS = 2
C_HEAD_DIM = 64
D_HEADS = 8
D_HEAD_DIM = 64
NA_WIN_H = 8
NA_WIN_W = 16
FFN_HIDDEN = 2816
LAMBDA_INIT = 0.8 - 0.6 * math.exp(-0.3 * 0)

S_ALL = CTX_LEN + SEQ
LANES = 128
HALF = LANES // 2
ROW_TILE = 256
NB = 2
N_TILES = S_ALL // ROW_TILE
N_LAT_TILES = SEQ // ROW_TILE
Q2 = 2 * ROW_TILE
KEY_CHUNK = 256
N_KEY_CHUNKS = S_ALL // KEY_CHUNK
SUM_ROWS = 16
CAST_SLABS = 16
ROWS_PER_TILE = ROW_TILE // GRID_W
NA_KEY_ROWS = 12
NA_KEYS = NA_KEY_ROWS * GRID_W
LOG2E = math.log2(math.e)
NEG_BIG = -1e30
VMEM_LIMIT = 52 * 1024 * 1024

assert ROW_TILE == CTX_LEN and S_ALL % ROW_TILE == 0 and N_LAT_TILES % 2 == 0

_BF = jnp.bfloat16
_F32 = jnp.float32


def _dot(a, b):
    return jnp.dot(a, b, preferred_element_type=_F32)


def _rms(x):
    return x * lax.rsqrt(jnp.mean(x * x, axis=-1, keepdims=True) + EPS)


def _lane(shape):
    return lax.broadcasted_iota(jnp.int32, shape, len(shape) - 1)


def _swap_halves(x, half):
    n = x.shape[-1]
    up = pltpu.roll(x, n - half, axis=1)
    down = pltpu.roll(x, half, axis=1)
    return jnp.where((_lane(x.shape) % (2 * half)) < half, up, down)


def _rope(x, c, s, half):
    return x * c + _swap_halves(x, half) * s


def _rope_t(x, c, s, half):
    parts = []
    for r in range(0, x.shape[0], 2 * half):
        parts += [x[r + half:r + 2 * half], x[r:r + half]]
    return x * c + jnp.concatenate(parts, axis=0) * s


def _dot_nt(a, b):
    return lax.dot_general(a, b, (((1,), (1,)), ((), ())), preferred_element_type=_F32)


def _group_mean_sq(x, bd):
    x2 = x * x
    hi = x2.astype(_BF)
    lo = (x2 - hi.astype(_F32)).astype(_BF)
    return _dot(hi, bd) + _dot(lo, bd)


def _ada_body(cond_ref, w_ref, b_ref, o_ref):
    c = cond_ref[...]
    a = (c * (1.0 / (1.0 + jnp.exp(-c)))).astype(_BF)
    o_ref[...] = _dot(a, w_ref[...].astype(_BF)) + b_ref[...]


def _ada_params(cond, w_mod, b_mod):
    n = w_mod.shape[1]
    tn = n // 4
    out = pl.pallas_call(
        _ada_body,
        grid=(n // tn,),
        in_specs=[pl.BlockSpec((8, D_MODEL), lambda j: (0, 0)),
                  pl.BlockSpec((D_MODEL, tn), lambda j: (0, j)),
                  pl.BlockSpec((1, tn), lambda j: (0, j))],
        out_specs=pl.BlockSpec((8, tn), lambda j: (0, j)),
        out_shape=jax.ShapeDtypeStruct((8, n), _F32),
        compiler_params=pltpu.CompilerParams(vmem_limit_bytes=VMEM_LIMIT),
        name="ada_params",
    )(cond, w_mod, b_mod.reshape(1, n))
    return out.reshape(8, 6, D_MODEL)


def _mod_row(b, t):
    return jnp.where(t == 0, BATCH // NB, b)


def _const_spec(shape):
    nd = len(shape)
    return pl.BlockSpec(shape, lambda *_: (0,) * nd, pipeline_mode=pl.Buffered(1))


def _modulated(x, m, g, shift_row, scale_row):
    h = _rms(x) * g
    return h * (1.0 + m[scale_row:scale_row + 1]) + m[shift_row:shift_row + 1]


def _modulated_rows(xs, m, g, shift_row, scale_row):
    return jnp.concatenate([_modulated(xs[i], m[i], g, shift_row, scale_row) for i in range(NB)],
                           axis=0).astype(_BF)


def _stream_tile(x_ref, ctx_ref):
    return jnp.where(pl.program_id(1) == 0, ctx_ref[...], x_ref[...])


def _stream_specs():
    return [pl.BlockSpec((NB, ROW_TILE, D_MODEL), lambda b, t: (b, jnp.maximum(t - 1, 0), 0)),
            pl.BlockSpec((NB, CTX_LEN, D_MODEL), lambda b, t: (b, 0, 0))]


def _proj0_body(x_ref, ctx_ref, mod_ref, g_ref, wn_ref, wt_ref,
                ca_ref, sa_ref, cb_ref, sb_ref, cat_ref, sat_ref, cbt_ref, sbt_ref,
                g_cq_ref, w_uqt_ref, g_ckv_ref, w_uk_ref, w_uvt_ref, *rest):
    n_cast = (len(rest) - 6) // 2
    qdt_ref, kd_ref, vdt_ref, qmt_ref, km_ref, vmt_ref = rest[n_cast:n_cast + 6]
    for src, dst in zip(rest[:n_cast], rest[n_cast + 6:]):
        dst[...] = src[...].astype(_BF)
    h = _modulated_rows(_stream_tile(x_ref, ctx_ref), mod_ref[...], g_ref[...], 0, 1)
    yn = _dot(h, wn_ref[...])
    yt = _dot_nt(wt_ref[...], h)
    cq = (_rms(yn[:, 512:768]) * g_cq_ref[...]).astype(_BF)
    ckv = (_rms(yn[:, 768:896]) * g_ckv_ref[...]).astype(_BF)
    qmt = _dot_nt(w_uqt_ref[...], cq)
    kn = _dot(ckv, w_uk_ref[...])
    vmt = _dot_nt(w_uvt_ref[...], ckv)
    ca, sa, cb, sb = ca_ref[...], sa_ref[...], cb_ref[...], sb_ref[...]
    cat, sat, cbt, sbt = cat_ref[...], sat_ref[...], cbt_ref[...], sbt_ref[...]
    qscale = A_HEAD_DIM ** -0.5 * LOG2E
    mscale = (B_NOPE_DIM + B_ROPE_DIM) ** -0.5 * LOG2E
    for i in range(NB):
        rows = slice(i * ROW_TILE, (i + 1) * ROW_TILE)
        for hd in range(A_HEADS):
            lo = hd * LANES
            qdt_ref[i, hd] = (_rope_t(yt[lo:lo + LANES, rows], cat, sat, A_HEAD_DIM // 2) * qscale).astype(_BF)
            kd_ref[i, hd] = _rope(yn[rows, lo:lo + LANES], ca, sa, A_HEAD_DIM // 2).astype(_BF)
            vdt_ref[i, hd] = yt[512 + lo:512 + lo + LANES, rows].astype(_BF)
        kr = _rope(yn[rows, 896:1024], cb, sb, B_ROPE_DIM // 2)
        for hd in range(B_HEADS):
            lo = hd * LANES
            qmt_ref[i, hd] = (_rope_t(qmt[lo:lo + LANES, rows], cbt, sbt, B_ROPE_DIM // 2) * mscale).astype(_BF)
            km_ref[i, hd] = (kn[rows, lo:lo + LANES] + kr).astype(_BF)
        for pr in range(B_HEADS // 2):
            vmt_ref[i, pr] = vmt[pr * LANES:(pr + 1) * LANES, rows].astype(_BF)


def _head_out(n_heads, transposed):
    if transposed:
        return (jax.ShapeDtypeStruct((BATCH, n_heads, LANES, S_ALL), _BF),
                pl.BlockSpec((NB, n_heads, LANES, ROW_TILE), lambda b, t: (b, 0, 0, t)))
    return (jax.ShapeDtypeStruct((BATCH, n_heads, S_ALL, LANES), _BF),
            pl.BlockSpec((NB, n_heads, ROW_TILE, LANES), lambda b, t: (b, 0, t, 0)))


def _row_spec(width):
    return pl.BlockSpec((NB, ROW_TILE, width), lambda b, t: (b, t, 0))


def _mod_spec():
    return pl.BlockSpec((NB, 6, D_MODEL), lambda b, t: (_mod_row(b, t), 0, 0))


def _table_spec():
    return pl.BlockSpec((ROW_TILE, LANES), lambda b, t: (t, 0))


def _table_t_spec():
    return pl.BlockSpec((LANES, ROW_TILE), lambda b, t: (0, t))


def _cast_spec(w):
    slab = lambda b, t: (jnp.minimum((b * N_TILES + t) * CAST_SLABS // (BATCH // NB * N_TILES), CAST_SLABS - 1), 0)
    return pl.BlockSpec((w.shape[0] // CAST_SLABS, w.shape[1]), slab)


def _project0(x, ctx, mods, g_attn, w_in, tabs, tabs_t, g_cq, w_uq, g_ckv, w_ukv, cast_weights):
    wn = jnp.concatenate([w_in[:, 512:1024], w_in[:, 1536:2048]], axis=1)
    wt = jnp.concatenate([w_in[:, 0:512], w_in[:, 1024:1536]], axis=1).T
    w_uqt = w_uq.T
    w_uk, w_uvt = w_ukv[:, :B_HEADS * LANES], w_ukv[:, B_HEADS * LANES:].T
    outs = [_head_out(4, True), _head_out(4, False), _head_out(4, True),
            _head_out(8, True), _head_out(8, False), _head_out(4, True)]
    return pl.pallas_call(
        _proj0_body,
        grid=(BATCH // NB, N_TILES),
        in_specs=_stream_specs() + [_mod_spec(), _const_spec((1, D_MODEL)),
                  _const_spec(wn.shape), _const_spec(wt.shape)]
                 + [_table_spec()] * 4 + [_table_t_spec()] * 4
                 + [_const_spec((1, B_Q_RANK)), _const_spec(w_uqt.shape),
                    _const_spec((1, B_KV_RANK)), _const_spec(w_uk.shape), _const_spec(w_uvt.shape)]
                 + [_cast_spec(w) for w in cast_weights],
        out_specs=[o[1] for o in outs] + [_cast_spec(w) for w in cast_weights],
        out_shape=[o[0] for o in outs] + [jax.ShapeDtypeStruct(w.shape, _BF) for w in cast_weights],
        compiler_params=pltpu.CompilerParams(vmem_limit_bytes=VMEM_LIMIT),
        name="project0",
    )(x, ctx, mods, g_attn.reshape(1, -1), wn, wt, *tabs, *tabs_t,
      g_cq.reshape(1, -1), w_uqt, g_ckv.reshape(1, -1), w_uk, w_uvt, *cast_weights)


def _proj1_body(x_ref, mod_ref, g_ref, wn_ref, wt_ref, ca_ref, sa_ref, cat_ref, sat_ref, bd_ref,
                g_qc_ref, g_kc_ref,
                qct_ref, kc_ref, vct_ref, qnt_ref, kn_ref, vnt_ref):
    h = _modulated_rows(x_ref[...], mod_ref[...], g_ref[...], 0, 1)
    yn = _dot(h, wn_ref[...])
    yt = _dot_nt(wt_ref[...], h)
    ca, sa, cat, sat = ca_ref[...], sa_ref[...], cat_ref[...], sat_ref[...]
    qscale = C_HEAD_DIM ** -0.5 * LOG2E
    nscale = D_HEAD_DIM ** -0.5 * LOG2E

    kc_all = yn[:, 0:LANES]
    kc_all = kc_all * lax.rsqrt(_group_mean_sq(kc_all, bd_ref[...]) + EPS) * g_kc_ref[...]
    q3 = yt[0:512].reshape(C_HEADS, C_HEAD_DIM, NB * ROW_TILE)
    q3 = q3 * lax.rsqrt(jnp.mean(q3 * q3, axis=1, keepdims=True) + EPS)
    qc_all = q3.reshape(512, NB * ROW_TILE) * g_qc_ref[...]
    for i in range(NB):
        rows = slice(i * ROW_TILE, (i + 1) * ROW_TILE)
        kc_ref[i, 0] = _rope(kc_all[rows], ca, sa, C_HEAD_DIM // 2).astype(_BF)
        vct_ref[i, 0] = yt[512:640, rows].astype(_BF)
        for pr in range(4):
            lo = pr * LANES
            qct_ref[i, pr] = (_rope_t(qc_all[lo:lo + LANES, rows], cat, sat, C_HEAD_DIM // 2) * qscale).astype(_BF)
            qnt_ref[i, pr] = (yt[640 + lo:640 + lo + LANES, rows] * nscale).astype(_BF)
            kn_ref[i, pr] = yn[rows, LANES + lo:2 * LANES + lo].astype(_BF)
            vnt_ref[i, pr] = yt[1152 + lo:1152 + lo + LANES, rows].astype(_BF)


def _project1(xs, mods, g_attn, w_in, tabs, tabs_t, bd, g_qc, g_kc):
    wn = jnp.concatenate([w_in[:, 512:640], w_in[:, 1280:1792]], axis=1)
    wt = jnp.concatenate([w_in[:, 0:512], w_in[:, 640:1280], w_in[:, 1792:2304]], axis=1).T
    outs = [_head_out(4, True), _head_out(1, False), _head_out(1, True),
            _head_out(4, True), _head_out(4, False), _head_out(4, True)]
    return pl.pallas_call(
        _proj1_body,
        grid=(BATCH // NB, N_TILES),
        in_specs=[_row_spec(D_MODEL), _mod_spec(), _const_spec((1, D_MODEL)),
                  _const_spec(wn.shape), _const_spec(wt.shape),
                  _table_spec(), _table_spec(), _table_t_spec(), _table_t_spec(),
                  _const_spec(bd.shape),
                  _const_spec((512, 1)), _const_spec((1, LANES))],
        out_specs=[o[1] for o in outs],
        out_shape=[o[0] for o in outs],
        compiler_params=pltpu.CompilerParams(vmem_limit_bytes=VMEM_LIMIT),
        name="project1",
    )(xs, mods, g_attn.reshape(1, -1), wn, wt, tabs[0], tabs[1], tabs_t[0], tabs_t[1], bd,
      jnp.tile(g_qc, 8).reshape(-1, 1), jnp.tile(g_kc, 2).reshape(1, -1))


def _split_q_cols(qt):
    z = jnp.zeros((HALF, qt.shape[1]), qt.dtype)
    return jnp.concatenate([jnp.concatenate([qt[:HALF], z], axis=0),
                            jnp.concatenate([z, qt[HALF:]], axis=0)], axis=1)


def _with_sum_rows(vt):
    n = vt.shape[1]
    return jnp.concatenate([vt, jnp.ones((1, n), vt.dtype), jnp.zeros((SUM_ROWS - 1, n), vt.dtype)], axis=0)


def _pv_t(vt, e, split):
    if not split:
        return _dot(_with_sum_rows(vt), e)
    return jnp.concatenate([_dot(_with_sum_rows(vt[:HALF]), e[:, :ROW_TILE]),
                            _dot(_with_sum_rows(vt[HALF:]), e[:, ROW_TILE:])], axis=0)


def _normalise_t(acc, split):
    if not split:
        return acc[:LANES] / acc[LANES:LANES + 1]
    hb = HALF + SUM_ROWS
    return jnp.concatenate([acc[:HALF] / acc[HALF:HALF + 1],
                            acc[hb:hb + HALF] / acc[hb + HALF:hb + HALF + 1]], axis=0)


def _pipelined_tiles(load_q, score_chunk, value_chunk, finalize, n_chunks, bufs, split):
    acc_shape = (LANES + 2 * SUM_ROWS, ROW_TILE) if split else (LANES + SUM_ROWS, Q2)

    def step(t_next, nxt, t_cur, cur):
        mx = acc = None
        if t_next is not None:
            qb = load_q(t_next)
        if t_cur is not None:
            m_cur = cur[1][0:1, :]
            acc = jnp.zeros(acc_shape, _F32)
        for c in range(n_chunks):
            rows = slice(c * KEY_CHUNK, (c + 1) * KEY_CHUNK)
            if t_next is not None:
                s = score_chunk(t_next, qb, c)
                nxt[0][rows, :] = s
                cm = jnp.max(s, axis=0, keepdims=True)
                mx = cm if mx is None else jnp.maximum(mx, cm)
            if t_cur is not None:
                e = jnp.exp2(cur[0][rows, :] - m_cur)
                acc = acc + value_chunk(t_cur, c, e.astype(_BF))
        if t_next is not None:
            nxt[1][...] = jnp.broadcast_to(mx, nxt[1].shape)
        return acc

    last = N_LAT_TILES - 1
    step(0, bufs[0], None, None)

    def pair(j, carry):
        finalize(jnp.maximum(2 * j - 1, 0), carry)
        finalize(2 * j, step(2 * j + 1, bufs[1], 2 * j, bufs[0]))
        return step(2 * j + 2, bufs[0], 2 * j + 1, bufs[1])

    carry = lax.fori_loop(0, N_LAT_TILES // 2 - 1, pair, jnp.ones(acc_shape, _F32))
    finalize(last - 2, carry)
    finalize(last - 1, step(last, bufs[1], last - 1, bufs[0]))
    finalize(last, step(None, None, last, bufs[1]))


def _dense_body(*refs, n_q, diff, ctx_tile, o_off):
    q_refs, k_refs, vt_ref = refs[:n_q], refs[n_q:2 * n_q], refs[2 * n_q]
    rest = refs[2 * n_q + 1:]
    if diff:
        lam_ref, g_ref, o_ref, s0, s1, m0, m1 = rest
        lv = lam_ref[...]
        lam = (jnp.exp(jnp.sum(lv[0:1] * lv[1:2], axis=-1, keepdims=True))
               - jnp.exp(jnp.sum(lv[2:3] * lv[3:4], axis=-1, keepdims=True)) + LAMBDA_INIT)
    else:
        o_ref, s0, s1, m0, m1 = rest

    def q_blocks(col0):
        if n_q == 1:
            return [_split_q_cols(q_refs[0][0, 0, :, pl.ds(col0, ROW_TILE)])]
        return [r[0, 0, :, pl.ds(col0, ROW_TILE)] for r in q_refs]

    def scores(qb, row0, rows):
        parts = [_dot(k_refs[i][0, 0, pl.ds(row0, rows), :], qb[i]) for i in range(n_q)]
        return parts[0] if n_q == 1 else jnp.concatenate(parts, axis=1)

    def finalize_rows(row0, acc):
        ot = _normalise_t(acc, not diff)
        if diff:
            o = (ot[:, :ROW_TILE] - lam * ot[:, ROW_TILE:]).T
            o = _rms(o) * g_ref[...] * (1.0 - LAMBDA_INIT)
        else:
            o = ot.T
        o_ref[0, pl.ds(row0, ROW_TILE), :] = o.astype(_BF)

    def load_q(t):
        return q_blocks(pl.multiple_of(CTX_LEN + t * ROW_TILE, ROW_TILE))

    def score_chunk(t, qb, c):
        return scores(qb, c * KEY_CHUNK, KEY_CHUNK)

    def value_chunk(t, c, e):
        return _pv_t(vt_ref[0, 0, :, c * KEY_CHUNK:(c + 1) * KEY_CHUNK], e, not diff)

    def finalize(t, acc):
        finalize_rows(pl.multiple_of(o_off + t * ROW_TILE, ROW_TILE), acc)

    _pipelined_tiles(load_q, score_chunk, value_chunk, finalize, N_KEY_CHUNKS, ((s0, m0), (s1, m1)), not diff)

    if ctx_tile:
        s = scores(q_blocks(0), 0, CTX_LEN)
        e = jnp.exp2(s - jnp.max(s, axis=0, keepdims=True))
        finalize_rows(0, _pv_t(vt_ref[0, 0, :, 0:CTX_LEN], e.astype(_BF), not diff))


def _na_body(qt_ref, k_ref, vt_ref, bias_ref, o_ref, s0, s1, m0, m1):
    def key_row0(t, c):
        first = jnp.clip(t - 1, 0, N_LAT_TILES - 3)
        return 0 if c == 0 else pl.multiple_of(CTX_LEN + (first + c - 1) * KEY_CHUNK, KEY_CHUNK)

    def load_q(t):
        return _split_q_cols(qt_ref[0, 0, :, pl.ds(pl.multiple_of(CTX_LEN + t * ROW_TILE, ROW_TILE), ROW_TILE)])

    def score_chunk(t, qb, c):
        s = _dot(k_ref[0, 0, pl.ds(key_row0(t, c), KEY_CHUNK), :], qb)
        if c > 0:
            kind = jnp.where(t == 0, 0, jnp.where(t == N_LAT_TILES - 1, 2, 1))
            s = s + bias_ref[0, kind, (c - 1) * KEY_CHUNK:c * KEY_CHUNK, :]
        return s

    def value_chunk(t, c, e):
        return _pv_t(vt_ref[0, 0, :, pl.ds(key_row0(t, c), KEY_CHUNK)], e, True)

    def finalize(t, acc):
        o = _normalise_t(acc, True).T
        o_ref[0, pl.ds(pl.multiple_of(t * ROW_TILE, ROW_TILE), ROW_TILE), :] = o.astype(_BF)

    _pipelined_tiles(load_q, score_chunk, value_chunk, finalize, 1 + NA_KEYS // KEY_CHUNK,
                     ((s0, m0), (s1, m1)), True)


def _attn_call(body, name, in_specs, arrays, out_rows, n_keys):
    return pl.pallas_call(
        body,
        grid=(BATCH, 4),
        in_specs=in_specs,
        out_specs=pl.BlockSpec((1, out_rows, LANES), lambda b, g: (b, 0, g)),
        out_shape=jax.ShapeDtypeStruct((BATCH, out_rows, 4 * LANES), _BF),
        scratch_shapes=[pltpu.VMEM((n_keys, Q2), _F32), pltpu.VMEM((n_keys, Q2), _F32),
                        pltpu.VMEM((8, Q2), _F32), pltpu.VMEM((8, Q2), _F32)],
        compiler_params=pltpu.CompilerParams(vmem_limit_bytes=VMEM_LIMIT),
        name=name,
    )(*arrays)


def _t_spec(f):
    return pl.BlockSpec((1, 1, LANES, S_ALL), f)


def _n_spec(f):
    return pl.BlockSpec((1, 1, S_ALL, LANES), f)


def _diff_attention(qdt, kd, vdt, lamv, g_subln):
    head = lambda b, g: (b, g, 0, 0)
    body = functools.partial(_dense_body, n_q=1, diff=True, ctx_tile=True, o_off=CTX_LEN)
    return _attn_call(body, "diff_attention",
                      [_t_spec(head), _n_spec(head), _t_spec(head),
                       pl.BlockSpec((4, A_HEAD_DIM), lambda b, g: (0, 0)),
                       pl.BlockSpec((1, LANES), lambda b, g: (0, 0))],
                      (qdt, kd, vdt, lamv, g_subln.reshape(1, -1)), S_ALL, S_ALL)


def _mla_attention(qmt, km, vmt):
    ha = lambda b, g: (b, 2 * g, 0, 0)
    hb = lambda b, g: (b, 2 * g + 1, 0, 0)
    body = functools.partial(_dense_body, n_q=2, diff=False, ctx_tile=True, o_off=CTX_LEN)
    return _attn_call(body, "mla_attention",
                      [_t_spec(ha), _t_spec(hb), _n_spec(ha), _n_spec(hb),
                       _t_spec(lambda b, g: (b, g, 0, 0))],
                      (qmt, qmt, km, km, vmt), S_ALL, S_ALL)


def _gqa_attention(qct, kc, vct):
    shared = lambda b, g: (b, 0, 0, 0)
    body = functools.partial(_dense_body, n_q=1, diff=False, ctx_tile=False, o_off=0)
    return _attn_call(body, "gqa_attention",
                      [_t_spec(lambda b, g: (b, g, 0, 0)), _n_spec(shared), _t_spec(shared)],
                      (qct, kc, vct), SEQ, S_ALL)


def _na_attention(qnt, kn, vnt, bias_t):
    head = lambda b, g: (b, g, 0, 0)
    return _attn_call(_na_body, "na_attention",
                      [_t_spec(head), _n_spec(head), _t_spec(head),
                       pl.BlockSpec((1, 3, NA_KEYS, Q2), lambda b, g: (g, 0, 0, 0))],
                      (qnt, kn, vnt, bias_t), SEQ, CTX_LEN + NA_KEYS)


def _merge_ffn_body(*refs, final):
    x = refs[0][...] if final else _stream_tile(refs[0], refs[1])
    (oa_ref, ob_ref, mod_ref, woa_ref, wob_ref, g_ref,
     w1_ref, w3_ref, w2_ref, gf_ref, o_ref) = refs[1 if final else 2:]
    m = mod_ref[...]
    stack = lambda r: r[...].reshape(NB * ROW_TILE, r.shape[-1])
    rows = lambda v, i: v[i * ROW_TILE:(i + 1) * ROW_TILE]
    attn = _dot(stack(oa_ref), woa_ref[...]) + _dot(stack(ob_ref), wob_ref[...])
    x1 = [x[i] + m[i, 2:3] * rows(attn, i) for i in range(NB)]
    h = _modulated_rows(x1, m, g_ref[...], 3, 4)
    a = _dot(h, w1_ref[...])
    b = _dot(h, w3_ref[...])
    u = (a * (1.0 / (1.0 + jnp.exp(-a))) * b).astype(_BF)
    f = _dot(u, w2_ref[...])
    for i in range(NB):
        x2 = x1[i] + m[i, 5:6] * rows(f, i)
        if final:
            x2 = _rms(x2) * gf_ref[...]
        o_ref[i] = x2


def _merge_ffn(streams, oa, ob, mods, w_out, g_ffn, w1, w3, w2, g_final, *, latent_only):
    half = lambda i: pl.BlockSpec((512, D_MODEL), lambda *_: (i, 0), pipeline_mode=pl.Buffered(1))
    off = 1 if latent_only else 0
    tiles = N_LAT_TILES if latent_only else N_TILES
    row = lambda w, o: pl.BlockSpec((NB, ROW_TILE, w), lambda b, t: (b, t + o, 0))
    mod = pl.BlockSpec((NB, 6, D_MODEL), lambda b, t: (_mod_row(b, t + off), 0, 0))
    x_specs = [row(D_MODEL, off)] if latent_only else _stream_specs()
    return pl.pallas_call(
        functools.partial(_merge_ffn_body, final=latent_only),
        grid=(BATCH // NB, tiles),
        in_specs=x_specs + [row(512, 0), row(512, 0), mod,
                  half(0), half(1), _const_spec((1, D_MODEL)),
                  _const_spec(w1.shape), _const_spec(w3.shape), _const_spec(w2.shape),
                  _const_spec((1, D_MODEL))],
        out_specs=row(D_MODEL, 0),
        out_shape=jax.ShapeDtypeStruct((BATCH, tiles * ROW_TILE, D_MODEL), _F32),
        compiler_params=pltpu.CompilerParams(vmem_limit_bytes=VMEM_LIMIT),
        name="merge_ffn_final" if latent_only else "merge_ffn",
    )(*streams, oa, ob, mods, w_out, w_out, g_ffn.reshape(1, -1), w1, w3, w2, g_final.reshape(1, -1))


def _rope_tables():
    t = np.arange(SEQ)
    rows = jnp.asarray(t // GRID_W, _F32)
    cols = jnp.asarray(t % GRID_W, _F32)

    def cs(rot_dim):
        axis_dim = rot_dim // 2
        inv = ROPE_THETA ** (-jnp.arange(0, axis_dim, 2, dtype=_F32) / axis_dim)
        ang = jnp.concatenate([rows[:, None] * inv, cols[:, None] * inv], axis=-1)
        return jnp.cos(ang), jnp.sin(ang)

    def with_ctx(tab, fill):
        return jnp.concatenate([jnp.full((CTX_LEN, LANES), fill, _F32), tab], axis=0)

    cos, sin = cs(A_HEAD_DIM)
    ca = with_ctx(jnp.tile(cos, (1, 4)), 1.0)
    sa = with_ctx(jnp.tile(jnp.concatenate([-sin, sin], axis=-1), (1, 2)), 0.0)
    cos, sin = cs(B_ROPE_DIM)
    ones = jnp.ones((SEQ, 64), _F32)
    zeros = jnp.zeros((SEQ, 64), _F32)
    cb = with_ctx(jnp.concatenate([ones, cos, cos, ones[:, :32]], axis=-1), 1.0)
    sb = with_ctx(jnp.concatenate([zeros, -sin, sin, zeros[:, :32]], axis=-1), 0.0)
    return ca, sa, cb, sb


def _na_bias_table(rpb):
    rows_n = SEQ // GRID_W
    span = 2 * GRID_W - 1
    left = GRID_W - NA_WIN_W
    p = jnp.pad(rpb * LOG2E, ((0, 0), (0, 0), (left, span - left - (2 * NA_WIN_W - 1))))
    flat = jnp.tile(p[:, :, ::-1], (1, 1, GRID_W + 1))[..., :GRID_W * (span + 1)]
    toe = flat.reshape(D_HEADS, 2 * NA_WIN_H - 1, GRID_W, span + 1)[:, :, ::-1, :GRID_W]
    kc = np.arange(GRID_W)[:, None]
    qc = np.arange(GRID_W)[None, :]
    cs = np.clip(qc - NA_WIN_W // 2, 0, GRID_W - NA_WIN_W)
    toe = jnp.where((kc >= cs) & (kc < cs + NA_WIN_W), toe, NEG_BIG)
    kinds = [(0, 0), (ROWS_PER_TILE, 0), (rows_n - ROWS_PER_TILE, rows_n - NA_KEY_ROWS)]
    d_first = [k0 - q0 + NA_WIN_H - 1 for q0, k0 in kinds]
    d_lo, d_hi = min(d_first), max(d_first) + NA_KEY_ROWS
    n_dr = 2 * NA_WIN_H - 1
    before, after = ROWS_PER_TILE - 1 - d_lo, d_hi - n_dr
    fill = lambda n: jnp.full((D_HEADS, n, GRID_W, GRID_W), NEG_BIG, _F32)
    tp = jnp.concatenate([fill(before), toe, fill(after)], axis=1).reshape(4, 2, before + n_dr + after, GRID_W, GRID_W)
    quads = jnp.concatenate([tp[:, w, ROWS_PER_TILE - 1 - qr:ROWS_PER_TILE - 1 - qr + d_hi - d_lo]
                             for w in (0, 1) for qr in range(ROWS_PER_TILE)], axis=-1)

    tiles = []
    for (q0, k0), d0 in zip(kinds, d_first):
        rs = np.clip(q0 + np.arange(ROWS_PER_TILE) - NA_WIN_H // 2, 0, rows_n - NA_WIN_H)[None, :]
        krow = (k0 + np.arange(NA_KEY_ROWS))[:, None]
        inside = (krow >= rs) & (krow < rs + NA_WIN_H)
        inside = np.tile(np.repeat(np.repeat(inside, GRID_W, axis=0), GRID_W, axis=1), (1, 2))
        block = quads[:, d0 - d_lo:d0 - d_lo + NA_KEY_ROWS].reshape(4, NA_KEYS, Q2)
        tiles.append(jnp.where(inside, block, NEG_BIG))
    return jnp.stack(tiles, axis=1)


def _pad_cols(w, groups, width, total, offset=0):
    k = w.shape[0]
    w = w.reshape(k, groups, width)
    w = jnp.pad(w, ((0, 0), (0, 0), (offset, total - width - offset)))
    return w.reshape(k, groups * total)


def kernel(x, c, ctx, c_ctx, l0_w_mod, l0_b_mod, l0_g_attn, l0_w_in, l0_lam_q1, l0_lam_k1, l0_lam_q2, l0_lam_k2, l0_g_subln, l0_g_cq, l0_w_uq, l0_g_ckv, l0_w_ukv, l0_w_out, l0_g_ffn, l0_w1, l0_w3, l0_w2, l1_w_mod, l1_b_mod, l1_g_attn, l1_w_in, l1_g_qc, l1_g_kc, l1_rpb, l1_w_out, l1_g_ffn, l1_w1, l1_w3, l1_w2, g_final):
    bf = lambda w: w.astype(_BF)
    cond = jnp.concatenate([c, jnp.tile(c_ctx[None], (NB, 1)),
                            jnp.zeros((8 - BATCH - NB, D_MODEL), _F32)], axis=0)
    tabs = _rope_tables()
    tabs_t = [t.T for t in tabs]

    mods0 = _ada_params(cond, l0_w_mod, l0_b_mod)
    w_in0 = bf(jnp.concatenate([l0_w_in[:, :1920],
                                _pad_cols(l0_w_in[:, 1920:], 1, B_ROPE_DIM, LANES, B_NOPE_DIM)], axis=1))
    w_uq = bf(_pad_cols(l0_w_uq, B_HEADS, B_NOPE_DIM + B_ROPE_DIM, LANES))
    ukv = l0_w_ukv.reshape(B_KV_RANK, B_HEADS, B_NOPE_DIM + B_V_DIM)
    w_ukv = bf(jnp.concatenate([_pad_cols(ukv[:, :, :B_NOPE_DIM].reshape(B_KV_RANK, -1), B_HEADS, B_NOPE_DIM, LANES),
                                ukv[:, :, B_NOPE_DIM:].reshape(B_KV_RANK, -1)], axis=1))
    woc = l1_w_out[:512].reshape(2, 4, C_HEAD_DIM, D_MODEL).transpose(1, 0, 2, 3).reshape(512, D_MODEL)
    w_out1 = jnp.concatenate([woc, l1_w_out[512:]], axis=0)
    (qdt, kd, vdt, qmt, km, vmt,
     w_out0, w1_0, w3_0, w2_0, w_out1, w1_1, w3_1, w2_1) = _project0(
        x, ctx, mods0, l0_g_attn, w_in0, tabs, tabs_t, l0_g_cq, w_uq, l0_g_ckv, w_ukv,
        [l0_w_out, l0_w1, l0_w3, l0_w2, w_out1, l1_w1, l1_w3, l1_w2])
    lamv = jnp.stack([l0_lam_q1, l0_lam_k1, l0_lam_q2, l0_lam_k2])
    o_a = _diff_attention(qdt, kd, vdt, lamv, l0_g_subln)
    o_b = _mla_attention(qmt, km, vmt)
    xs = _merge_ffn((x, ctx), o_a, o_b, mods0, w_out0, l0_g_ffn, w1_0, w3_0, w2_0, g_final, latent_only=False)

    mods1 = _ada_params(cond, l1_w_mod, l1_b_mod)
    wq = l1_w_in[:, :512].reshape(D_MODEL, 2, 4, C_HEAD_DIM).transpose(0, 2, 1, 3).reshape(D_MODEL, 512)
    w_in1 = bf(jnp.concatenate([wq, l1_w_in[:, 512:]], axis=1))
    grp = np.arange(LANES) // C_HEAD_DIM
    bd = jnp.asarray((grp[:, None] == grp[None, :]) / C_HEAD_DIM, _BF)
    qct, kc, vct, qnt, kn, vnt = _project1(xs, mods1, l1_g_attn, w_in1, tabs, tabs_t, bd, l1_g_qc, l1_g_kc)
    o_c = _gqa_attention(qct, kc, vct)
    o_d = _na_attention(qnt, kn, vnt, _na_bias_table(l1_rpb))
    return _merge_ffn((xs,), o_c, o_d, mods1, w_out1, l1_g_ffn, w1_1, w3_1, w2_1, g_final, latent_only=True)
```

```python
import functools
import math

import numpy as np
import jax
import jax.numpy as jnp
from jax import lax
from jax.experimental import pallas as pl
from jax.experimental.pallas import tpu as pltpu

D_MODEL = 1024
BATCH = 4
SEQ = 4096
GRID_W = 64
CTX_LEN = 256
ROPE_THETA = 10000.0
EPS = 1e-6
A_HEADS = 4
A_HEAD_DIM = 64
B_HEADS = 8
B_Q_RANK = 256
B_KV_RANK = 128
B_NOPE_DIM = 64
B_ROPE_DIM = 32
B_V_DIM = 64
C_HEADS = 8
C_KV_HEADS = 2
C_HEAD_DIM = 64
D_HEADS = 8
D_HEAD_DIM = 64
NA_WIN_H = 8
NA_WIN_W = 16
FFN_HIDDEN = 2816
LAMBDA_INIT = 0.8 - 0.6 * math.exp(-0.3 * 0)

S_ALL = CTX_LEN + SEQ
LANES = 128
HALF = LANES // 2
ROW_TILE = 256
NB = 2
N_TILES = S_ALL // ROW_TILE
N_LAT_TILES = SEQ // ROW_TILE
Q2 = 2 * ROW_TILE
KEY_CHUNK = 256
N_KEY_CHUNKS = S_ALL // KEY_CHUNK
SUM_ROWS = 16
ROWS_PER_TILE = ROW_TILE // GRID_W
NA_KEY_ROWS = 12
NA_KEYS = NA_KEY_ROWS * GRID_W
LOG2E = math.log2(math.e)
NEG_BIG = -1e30
VMEM_LIMIT = 52 * 1024 * 1024

assert ROW_TILE == CTX_LEN and S_ALL % ROW_TILE == 0 and N_LAT_TILES % 2 == 0

_BF = jnp.bfloat16
_F32 = jnp.float32


def _dot(a, b):
    return jnp.dot(a, b, preferred_element_type=_F32)


def _rms(x):
    return x * lax.rsqrt(jnp.mean(x * x, axis=-1, keepdims=True) + EPS)


def _lane(shape):
    return lax.broadcasted_iota(jnp.int32, shape, len(shape) - 1)


def _swap_halves(x, half):
    n = x.shape[-1]
    up = pltpu.roll(x, n - half, axis=1)
    down = pltpu.roll(x, half, axis=1)
    return jnp.where((_lane(x.shape) % (2 * half)) < half, up, down)


def _rope(x, c, s, half):
    return x * c + _swap_halves(x, half) * s


def _rope_t(x, c, s, half):
    parts = []
    for r in range(0, x.shape[0], 2 * half):
        parts += [x[r + half:r + 2 * half], x[r:r + half]]
    return x * c + jnp.concatenate(parts, axis=0) * s


def _dot_nt(a, b):
    return lax.dot_general(a, b, (((1,), (1,)), ((), ())), preferred_element_type=_F32)


def _group_mean_sq(x, bd):
    x2 = x * x
    hi = x2.astype(_BF)
    lo = (x2 - hi.astype(_F32)).astype(_BF)
    return _dot(hi, bd) + _dot(lo, bd)


def _ada_body(cond_ref, w_ref, b_ref, o_ref):
    c = cond_ref[...]
    a = (c * (1.0 / (1.0 + jnp.exp(-c)))).astype(_BF)
    o_ref[...] = _dot(a, w_ref[...].astype(_BF)) + b_ref[...]


def _ada_params(cond, w_mod, b_mod):
    n = w_mod.shape[1]
    tn = n // 4
    out = pl.pallas_call(
        _ada_body,
        grid=(n // tn,),
        in_specs=[pl.BlockSpec((8, D_MODEL), lambda j: (0, 0)),
                  pl.BlockSpec((D_MODEL, tn), lambda j: (0, j)),
                  pl.BlockSpec((1, tn), lambda j: (0, j))],
        out_specs=pl.BlockSpec((8, tn), lambda j: (0, j)),
        out_shape=jax.ShapeDtypeStruct((8, n), _F32),
        compiler_params=pltpu.CompilerParams(vmem_limit_bytes=VMEM_LIMIT),
        name="ada_params",
    )(cond, w_mod, b_mod.reshape(1, n))
    return out.reshape(8, 6, D_MODEL)


def _mod_row(b, t):
    return jnp.where(t == 0, BATCH // NB, b)


def _const_spec(shape):
    nd = len(shape)
    return pl.BlockSpec(shape, lambda *_: (0,) * nd, pipeline_mode=pl.Buffered(1))


def _modulated(x, m, g, shift_row, scale_row):
    h = _rms(x) * g
    return h * (1.0 + m[scale_row:scale_row + 1]) + m[shift_row:shift_row + 1]


def _modulated_rows(xs, m, g, shift_row, scale_row):
    return jnp.concatenate([_modulated(xs[i], m[i], g, shift_row, scale_row) for i in range(NB)],
                           axis=0).astype(_BF)


def _stream_tile(x_ref, ctx_ref):
    return jnp.where(pl.program_id(1) == 0, ctx_ref[...], x_ref[...])


def _stream_specs():
    return [pl.BlockSpec((NB, ROW_TILE, D_MODEL), lambda b, t: (b, jnp.maximum(t - 1, 0), 0)),
            pl.BlockSpec((NB, CTX_LEN, D_MODEL), lambda b, t: (b, 0, 0))]


def _proj0_body(x_ref, ctx_ref, mod_ref, g_ref, wn_ref, wt_ref,
                ca_ref, sa_ref, cb_ref, sb_ref, cat_ref, sat_ref, cbt_ref, sbt_ref,
                g_cq_ref, w_uqt_ref, g_ckv_ref, w_uk_ref, w_uvt_ref,
                qdt_ref, kd_ref, vdt_ref, qmt_ref, km_ref, vmt_ref):
    h = _modulated_rows(_stream_tile(x_ref, ctx_ref), mod_ref[...], g_ref[...], 0, 1)
    yn = _dot(h, wn_ref[...])
    yt = _dot_nt(wt_ref[...], h)
    cq = (_rms(yn[:, 512:768]) * g_cq_ref[...]).astype(_BF)
    ckv = (_rms(yn[:, 768:896]) * g_ckv_ref[...]).astype(_BF)
    qmt = _dot_nt(w_uqt_ref[...], cq)
    kn = _dot(ckv, w_uk_ref[...])
    vmt = _dot_nt(w_uvt_ref[...], ckv)
    ca, sa, cb, sb = ca_ref[...], sa_ref[...], cb_ref[...], sb_ref[...]
    cat, sat, cbt, sbt = cat_ref[...], sat_ref[...], cbt_ref[...], sbt_ref[...]
    qscale = A_HEAD_DIM ** -0.5 * LOG2E
    mscale = (B_NOPE_DIM + B_ROPE_DIM) ** -0.5 * LOG2E
    for i in range(NB):
        rows = slice(i * ROW_TILE, (i + 1) * ROW_TILE)
        for hd in range(A_HEADS):
            lo = hd * LANES
            qdt_ref[i, hd] = (_rope_t(yt[lo:lo + LANES, rows], cat, sat, A_HEAD_DIM // 2) * qscale).astype(_BF)
            kd_ref[i, hd] = _rope(yn[rows, lo:lo + LANES], ca, sa, A_HEAD_DIM // 2).astype(_BF)
            vdt_ref[i, hd] = yt[512 + lo:512 + lo + LANES, rows].astype(_BF)
        kr = _rope(yn[rows, 896:1024], cb, sb, B_ROPE_DIM // 2)
        for hd in range(B_HEADS):
            lo = hd * LANES
            qmt_ref[i, hd] = (_rope_t(qmt[lo:lo + LANES, rows], cbt, sbt, B_ROPE_DIM // 2) * mscale).astype(_BF)
            km_ref[i, hd] = (kn[rows, lo:lo + LANES] + kr).astype(_BF)
        for pr in range(B_HEADS // 2):
            vmt_ref[i, pr] = vmt[pr * LANES:(pr + 1) * LANES, rows].astype(_BF)


def _head_out(n_heads, transposed):
    if transposed:
        return (jax.ShapeDtypeStruct((BATCH, n_heads, LANES, S_ALL), _BF),
                pl.BlockSpec((NB, n_heads, LANES, ROW_TILE), lambda b, t: (b, 0, 0, t)))
    return (jax.ShapeDtypeStruct((BATCH, n_heads, S_ALL, LANES), _BF),
            pl.BlockSpec((NB, n_heads, ROW_TILE, LANES), lambda b, t: (b, 0, t, 0)))


def _row_spec(width):
    return pl.BlockSpec((NB, ROW_TILE, width), lambda b, t: (b, t, 0))


def _mod_spec():
    return pl.BlockSpec((NB, 6, D_MODEL), lambda b, t: (_mod_row(b, t), 0, 0))


def _table_spec():
    return pl.BlockSpec((ROW_TILE, LANES), lambda b, t: (t, 0))


def _table_t_spec():
    return pl.BlockSpec((LANES, ROW_TILE), lambda b, t: (0, t))


def _project0(x, ctx, mods, g_attn, w_in, tabs, tabs_t, g_cq, w_uq, g_ckv, w_ukv):
    wn = jnp.concatenate([w_in[:, 512:1024], w_in[:, 1536:2048]], axis=1)
    wt = jnp.concatenate([w_in[:, 0:512], w_in[:, 1024:1536]], axis=1).T
    w_uqt = w_uq.T
    w_uk, w_uvt = w_ukv[:, :B_HEADS * LANES], w_ukv[:, B_HEADS * LANES:].T
    outs = [_head_out(4, True), _head_out(4, False), _head_out(4, True),
            _head_out(8, True), _head_out(8, False), _head_out(4, True)]
    return pl.pallas_call(
        _proj0_body,
        grid=(BATCH // NB, N_TILES),
        in_specs=_stream_specs() + [_mod_spec(), _const_spec((1, D_MODEL)),
                  _const_spec(wn.shape), _const_spec(wt.shape)]
                 + [_table_spec()] * 4 + [_table_t_spec()] * 4
                 + [_const_spec((1, B_Q_RANK)), _const_spec(w_uqt.shape),
                    _const_spec((1, B_KV_RANK)), _const_spec(w_uk.shape), _const_spec(w_uvt.shape)],
        out_specs=[o[1] for o in outs],
        out_shape=[o[0] for o in outs],
        compiler_params=pltpu.CompilerParams(vmem_limit_bytes=VMEM_LIMIT),
        name="project0",
    )(x, ctx, mods, g_attn.reshape(1, -1), wn, wt, *tabs, *tabs_t,
      g_cq.reshape(1, -1), w_uqt, g_ckv.reshape(1, -1), w_uk, w_uvt)


def _proj1_body(x_ref, mod_ref, g_ref, wn_ref, wt_ref, ca_ref, sa_ref, cat_ref, sat_ref, bd_ref,
                g_qc_ref, g_kc_ref,
                qct_ref, kc_ref, vct_ref, qnt_ref, kn_ref, vnt_ref):
    h = _modulated_rows(x_ref[...], mod_ref[...], g_ref[...], 0, 1)
    yn = _dot(h, wn_ref[...])
    yt = _dot_nt(wt_ref[...], h)
    ca, sa, cat, sat = ca_ref[...], sa_ref[...], cat_ref[...], sat_ref[...]
    qscale = C_HEAD_DIM ** -0.5 * LOG2E
    nscale = D_HEAD_DIM ** -0.5 * LOG2E

    kc_all = yn[:, 0:LANES]
    kc_all = kc_all * lax.rsqrt(_group_mean_sq(kc_all, bd_ref[...]) + EPS) * g_kc_ref[...]
    q3 = yt[0:512].reshape(C_HEADS, C_HEAD_DIM, NB * ROW_TILE)
    q3 = q3 * lax.rsqrt(jnp.mean(q3 * q3, axis=1, keepdims=True) + EPS)
    qc_all = q3.reshape(512, NB * ROW_TILE) * g_qc_ref[...]
    for i in range(NB):
        rows = slice(i * ROW_TILE, (i + 1) * ROW_TILE)
        kc_ref[i, 0] = _rope(kc_all[rows], ca, sa, C_HEAD_DIM // 2).astype(_BF)
        vct_ref[i, 0] = yt[512:640, rows].astype(_BF)
        for pr in range(4):
            lo = pr * LANES
            qct_ref[i, pr] = (_rope_t(qc_all[lo:lo + LANES, rows], cat, sat, C_HEAD_DIM // 2) * qscale).astype(_BF)
            qnt_ref[i, pr] = (yt[640 + lo:640 + lo + LANES, rows] * nscale).astype(_BF)
            kn_ref[i, pr] = yn[rows, LANES + lo:2 * LANES + lo].astype(_BF)
            vnt_ref[i, pr] = yt[1152 + lo:1152 + lo + LANES, rows].astype(_BF)


def _project1(xs, mods, g_attn, w_in, tabs, tabs_t, bd, g_qc, g_kc):
    wn = jnp.concatenate([w_in[:, 512:640], w_in[:, 1280:1792]], axis=1)
    wt = jnp.concatenate([w_in[:, 0:512], w_in[:, 640:1280], w_in[:, 1792:2304]], axis=1).T
    outs = [_head_out(4, True), _head_out(1, False), _head_out(1, True),
            _head_out(4, True), _head_out(4, False), _head_out(4, True)]
    return pl.pallas_call(
        _proj1_body,
        grid=(BATCH // NB, N_TILES),
        in_specs=[_row_spec(D_MODEL), _mod_spec(), _const_spec((1, D_MODEL)),
                  _const_spec(wn.shape), _const_spec(wt.shape),
                  _table_spec(), _table_spec(), _table_t_spec(), _table_t_spec(),
                  _const_spec(bd.shape),
                  _const_spec((512, 1)), _const_spec((1, LANES))],
        out_specs=[o[1] for o in outs],
        out_shape=[o[0] for o in outs],
        compiler_params=pltpu.CompilerParams(vmem_limit_bytes=VMEM_LIMIT),
        name="project1",
    )(xs, mods, g_attn.reshape(1, -1), wn, wt, tabs[0], tabs[1], tabs_t[0], tabs_t[1], bd,
      jnp.tile(g_qc, 8).reshape(-1, 1), jnp.tile(g_kc, 2).reshape(1, -1))


def _split_q_cols(qt):
    z = jnp.zeros((HALF, qt.shape[1]), qt.dtype)
    return jnp.concatenate([jnp.concatenate([qt[:HALF], z], axis=0),
                            jnp.concatenate([z, qt[HALF:]], axis=0)], axis=1)


def _with_sum_rows(vt):
    n = vt.shape[1]
    return jnp.concatenate([vt, jnp.ones((1, n), vt.dtype), jnp.zeros((SUM_ROWS - 1, n), vt.dtype)], axis=0)


def _pv_t(vt, e, split):
    if not split:
        return _dot(_with_sum_rows(vt), e)
    return jnp.concatenate([_dot(_with_sum_rows(vt[:HALF]), e[:, :ROW_TILE]),
                            _dot(_with_sum_rows(vt[HALF:]), e[:, ROW_TILE:])], axis=0)


def _normalise_t(acc, split):
    if not split:
        return acc[:LANES] / acc[LANES:LANES + 1]
    hb = HALF + SUM_ROWS
    return jnp.concatenate([acc[:HALF] / acc[HALF:HALF + 1],
                            acc[hb:hb + HALF] / acc[hb + HALF:hb + HALF + 1]], axis=0)


def _pipelined_tiles(load_q, score_chunk, value_chunk, finalize, n_chunks, bufs, split):
    acc_shape = (LANES + 2 * SUM_ROWS, ROW_TILE) if split else (LANES + SUM_ROWS, Q2)
    bufs = (bufs[0] + (0,), bufs[1] + (LANES,))

    def step(t_next, nxt, t_cur, cur):
        mx = acc = None
        if t_next is not None:
            qb = load_q(t_next)
        if t_cur is not None:
            m_cur = cur[1][0:1, :]
            acc = jnp.zeros(acc_shape, _F32)
        for c in range(n_chunks):
            rows = slice(c * KEY_CHUNK, (c + 1) * KEY_CHUNK)
            if t_next is not None:
                s = score_chunk(t_next, qb, c)
                nxt[0][rows, nxt[2]:nxt[2] + Q2] = s
                cm = jnp.max(s, axis=0, keepdims=True)
                mx = cm if mx is None else jnp.maximum(mx, cm)
            if t_cur is not None:
                e = jnp.exp2(cur[0][rows, cur[2]:cur[2] + Q2] - m_cur)
                acc = acc + value_chunk(t_cur, c, e.astype(_BF))
        if t_next is not None:
            nxt[1][...] = jnp.broadcast_to(mx, nxt[1].shape)
        return acc

    last = N_LAT_TILES - 1
    step(0, bufs[0], None, None)

    def pair(j, carry):
        finalize(jnp.maximum(2 * j - 1, 0), carry)
        finalize(2 * j, step(2 * j + 1, bufs[1], 2 * j, bufs[0]))
        return step(2 * j + 2, bufs[0], 2 * j + 1, bufs[1])

    carry = lax.fori_loop(0, N_LAT_TILES // 2 - 1, pair, jnp.ones(acc_shape, _F32))
    finalize(last - 2, carry)
    finalize(last - 1, step(last, bufs[1], last - 1, bufs[0]))
    finalize(last, step(None, None, last, bufs[1]))


def _dense_body(*refs, n_q, diff, ctx_tile, o_off):
    q_refs, k_refs, vt_ref = refs[:n_q], refs[n_q:2 * n_q], refs[2 * n_q]
    rest = refs[2 * n_q + 1:]
    if diff:
        lam_ref, g_ref, o_ref, s0, s1, m0, m1 = rest
        lv = lam_ref[...]
        lam = (jnp.exp(jnp.sum(lv[0:1] * lv[1:2], axis=-1, keepdims=True))
               - jnp.exp(jnp.sum(lv[2:3] * lv[3:4], axis=-1, keepdims=True)) + LAMBDA_INIT)
    else:
        o_ref, s0, s1, m0, m1 = rest

    def q_blocks(col0):
        if n_q == 1:
            return [_split_q_cols(q_refs[0][0, 0, :, pl.ds(col0, ROW_TILE)])]
        return [r[0, 0, :, pl.ds(col0, ROW_TILE)] for r in q_refs]

    def scores(qb, row0, rows):
        parts = [_dot(k_refs[i][0, 0, pl.ds(row0, rows), :], qb[i]) for i in range(n_q)]
        return parts[0] if n_q == 1 else jnp.concatenate(parts, axis=1)

    def finalize_rows(row0, acc):
        ot = _normalise_t(acc, not diff)
        if diff:
            o = (ot[:, :ROW_TILE] - lam * ot[:, ROW_TILE:]).T
            o = _rms(o) * g_ref[...] * (1.0 - LAMBDA_INIT)
        else:
            o = ot.T
        o_ref[0, pl.ds(row0, ROW_TILE), :] = o.astype(_BF)

    def load_q(t):
        return q_blocks(pl.multiple_of(CTX_LEN + t * ROW_TILE, ROW_TILE))

    def score_chunk(t, qb, c):
        return scores(qb, c * KEY_CHUNK, KEY_CHUNK)

    def value_chunk(t, c, e):
        return _pv_t(vt_ref[0, 0, :, c * KEY_CHUNK:(c + 1) * KEY_CHUNK], e, not diff)

    def finalize(t, acc):
        finalize_rows(pl.multiple_of(o_off + t * ROW_TILE, ROW_TILE), acc)

    _pipelined_tiles(load_q, score_chunk, value_chunk, finalize, N_KEY_CHUNKS, ((s0, m0), (s1, m1)), not diff)

    if ctx_tile:
        s = scores(q_blocks(0), 0, CTX_LEN)
        e = jnp.exp2(s - jnp.max(s, axis=0, keepdims=True))
        finalize_rows(0, _pv_t(vt_ref[0, 0, :, 0:CTX_LEN], e.astype(_BF), not diff))


def _na_body(qt_ref, k_ref, vt_ref, bias_ref, o_ref, s0, s1, m0, m1):
    def key_row0(t, c):
        first = jnp.clip(t - 1, 0, N_LAT_TILES - 3)
        return 0 if c == 0 else pl.multiple_of(CTX_LEN + (first + c - 1) * KEY_CHUNK, KEY_CHUNK)

    def load_q(t):
        return _split_q_cols(qt_ref[0, 0, :, pl.ds(pl.multiple_of(CTX_LEN + t * ROW_TILE, ROW_TILE), ROW_TILE)])

    def score_chunk(t, qb, c):
        s = _dot(k_ref[0, 0, pl.ds(key_row0(t, c), KEY_CHUNK), :], qb)
        if c > 0:
            kind = jnp.where(t == 0, 0, jnp.where(t == N_LAT_TILES - 1, 2, 1))
            s = s + bias_ref[0, kind, (c - 1) * KEY_CHUNK:c * KEY_CHUNK, :]
        return s

    def value_chunk(t, c, e):
        return _pv_t(vt_ref[0, 0, :, pl.ds(key_row0(t, c), KEY_CHUNK)], e, True)

    def finalize(t, acc):
        o = _normalise_t(acc, True).T
        o_ref[0, pl.ds(pl.multiple_of(t * ROW_TILE, ROW_TILE), ROW_TILE), :] = o.astype(_BF)

    _pipelined_tiles(load_q, score_chunk, value_chunk, finalize, 1 + NA_KEYS // KEY_CHUNK,
                     ((s0, m0), (s1, m1)), True)


def _attn_call(body, name, in_specs, arrays, out_rows, n_keys):
    return pl.pallas_call(
        body,
        grid=(BATCH, 4),
        in_specs=in_specs,
        out_specs=pl.BlockSpec((1, out_rows, LANES), lambda b, g: (b, 0, g)),
        out_shape=jax.ShapeDtypeStruct((BATCH, out_rows, 4 * LANES), _BF),
        scratch_shapes=[pltpu.VMEM((n_keys, Q2 + LANES), _F32), pltpu.VMEM((n_keys, Q2 + LANES), _F32),
                        pltpu.VMEM((8, Q2), _F32), pltpu.VMEM((8, Q2), _F32)],
        compiler_params=pltpu.CompilerParams(vmem_limit_bytes=VMEM_LIMIT),
        name=name,
    )(*arrays)


def _t_spec(f):
    return pl.BlockSpec((1, 1, LANES, S_ALL), f)


def _n_spec(f):
    return pl.BlockSpec((1, 1, S_ALL, LANES), f)


def _diff_attention(qdt, kd, vdt, lamv, g_subln):
    head = lambda b, g: (b, g, 0, 0)
    body = functools.partial(_dense_body, n_q=1, diff=True, ctx_tile=True, o_off=CTX_LEN)
    return _attn_call(body, "diff_attention",
                      [_t_spec(head), _n_spec(head), _t_spec(head),
                       pl.BlockSpec((4, A_HEAD_DIM), lambda b, g: (0, 0)),
                       pl.BlockSpec((1, LANES), lambda b, g: (0, 0))],
                      (qdt, kd, vdt, lamv, g_subln.reshape(1, -1)), S_ALL, S_ALL)


def _mla_attention(qmt, km, vmt):
    ha = lambda b, g: (b, 2 * g, 0, 0)
    hb = lambda b, g: (b, 2 * g + 1, 0, 0)
    body = functools.partial(_dense_body, n_q=2, diff=False, ctx_tile=True, o_off=CTX_LEN)
    return _attn_call(body, "mla_attention",
                      [_t_spec(ha), _t_spec(hb), _n_spec(ha), _n_spec(hb),
                       _t_spec(lambda b, g: (b, g, 0, 0))],
                      (qmt, qmt, km, km, vmt), S_ALL, S_ALL)


def _gqa_attention(qct, kc, vct):
    shared = lambda b, g: (b, 0, 0, 0)
    body = functools.partial(_dense_body, n_q=1, diff=False, ctx_tile=False, o_off=0)
    return _attn_call(body, "gqa_attention",
                      [_t_spec(lambda b, g: (b, g, 0, 0)), _n_spec(shared), _t_spec(shared)],
                      (qct, kc, vct), SEQ, S_ALL)


def _na_attention(qnt, kn, vnt, bias_t):
    head = lambda b, g: (b, g, 0, 0)
    return _attn_call(_na_body, "na_attention",
                      [_t_spec(head), _n_spec(head), _t_spec(head),
                       pl.BlockSpec((1, 3, NA_KEYS, Q2), lambda b, g: (g, 0, 0, 0))],
                      (qnt, kn, vnt, bias_t), SEQ, CTX_LEN + NA_KEYS)


def _merge_ffn_body(*refs, final):
    x = refs[0][...] if final else _stream_tile(refs[0], refs[1])
    (oa_ref, ob_ref, mod_ref, woa_ref, wob_ref, g_ref,
     w1_ref, w3_ref, w2_ref, gf_ref, o_ref) = refs[1 if final else 2:]
    m = mod_ref[...]
    stack = lambda r: r[...].reshape(NB * ROW_TILE, r.shape[-1])
    rows = lambda v, i: v[i * ROW_TILE:(i + 1) * ROW_TILE]
    attn = _dot(stack(oa_ref), woa_ref[...]) + _dot(stack(ob_ref), wob_ref[...])
    x1 = [x[i] + m[i, 2:3] * rows(attn, i) for i in range(NB)]
    h = _modulated_rows(x1, m, g_ref[...], 3, 4)
    a = _dot(h, w1_ref[...])
    b = _dot(h, w3_ref[...])
    u = (a * (1.0 / (1.0 + jnp.exp(-a))) * b).astype(_BF)
    f = _dot(u, w2_ref[...])
    for i in range(NB):
        x2 = x1[i] + m[i, 5:6] * rows(f, i)
        if final:
            x2 = _rms(x2) * gf_ref[...]
        o_ref[i] = x2


def _merge_ffn(streams, oa, ob, mods, woa, wob, g_ffn, w1, w3, w2, g_final, *, latent_only):
    off = 1 if latent_only else 0
    tiles = N_LAT_TILES if latent_only else N_TILES
    row = lambda w, o: pl.BlockSpec((NB, ROW_TILE, w), lambda b, t: (b, t + o, 0))
    mod = pl.BlockSpec((NB, 6, D_MODEL), lambda b, t: (_mod_row(b, t + off), 0, 0))
    x_specs = [row(D_MODEL, off)] if latent_only else _stream_specs()
    return pl.pallas_call(
        functools.partial(_merge_ffn_body, final=latent_only),
        grid=(BATCH // NB, tiles),
        in_specs=x_specs + [row(512, 0), row(512, 0), mod,
                  _const_spec(woa.shape), _const_spec(wob.shape), _const_spec((1, D_MODEL)),
                  _const_spec(w1.shape), _const_spec(w3.shape), _const_spec(w2.shape),
                  _const_spec((1, D_MODEL))],
        out_specs=row(D_MODEL, 0),
        out_shape=jax.ShapeDtypeStruct((BATCH, tiles * ROW_TILE, D_MODEL), _F32),
        compiler_params=pltpu.CompilerParams(vmem_limit_bytes=VMEM_LIMIT),
        name="merge_ffn_final" if latent_only else "merge_ffn",
    )(*streams, oa, ob, mods, woa, wob, g_ffn.reshape(1, -1), w1, w3, w2, g_final.reshape(1, -1))


def _rope_tables():
    t = np.arange(SEQ)
    rows = jnp.asarray(t // GRID_W, _F32)
    cols = jnp.asarray(t % GRID_W, _F32)

    def cs(rot_dim):
        axis_dim = rot_dim // 2
        inv = ROPE_THETA ** (-jnp.arange(0, axis_dim, 2, dtype=_F32) / axis_dim)
        ang = jnp.concatenate([rows[:, None] * inv, cols[:, None] * inv], axis=-1)
        return jnp.cos(ang), jnp.sin(ang)

    def with_ctx(tab, fill):
        return jnp.concatenate([jnp.full((CTX_LEN, LANES), fill, _F32), tab], axis=0)

    cos, sin = cs(A_HEAD_DIM)
    ca = with_ctx(jnp.tile(cos, (1, 4)), 1.0)
    sa = with_ctx(jnp.tile(jnp.concatenate([-sin, sin], axis=-1), (1, 2)), 0.0)
    cos, sin = cs(B_ROPE_DIM)
    ones = jnp.ones((SEQ, 64), _F32)
    zeros = jnp.zeros((SEQ, 64), _F32)
    cb = with_ctx(jnp.concatenate([ones, cos, cos, ones[:, :32]], axis=-1), 1.0)
    sb = with_ctx(jnp.concatenate([zeros, -sin, sin, zeros[:, :32]], axis=-1), 0.0)
    return ca, sa, cb, sb


def _na_bias_table(rpb):
    rows_n = SEQ // GRID_W
    span = 2 * GRID_W - 1
    left = GRID_W - NA_WIN_W
    p = jnp.pad(rpb * LOG2E, ((0, 0), (0, 0), (left, span - left - (2 * NA_WIN_W - 1))))
    flat = jnp.tile(p[:, :, ::-1], (1, 1, GRID_W + 1))[..., :GRID_W * (span + 1)]
    toe = flat.reshape(D_HEADS, 2 * NA_WIN_H - 1, GRID_W, span + 1)[:, :, ::-1, :GRID_W]
    kc = np.arange(GRID_W)[:, None]
    qc = np.arange(GRID_W)[None, :]
    cs = np.clip(qc - NA_WIN_W // 2, 0, GRID_W - NA_WIN_W)
    toe = jnp.where((kc >= cs) & (kc < cs + NA_WIN_W), toe, NEG_BIG)
    kinds = [(0, 0), (ROWS_PER_TILE, 0), (rows_n - ROWS_PER_TILE, rows_n - NA_KEY_ROWS)]
    d_first = [k0 - q0 + NA_WIN_H - 1 for q0, k0 in kinds]
    d_lo, d_hi = min(d_first), max(d_first) + NA_KEY_ROWS
    n_dr = 2 * NA_WIN_H - 1
    before, after = ROWS_PER_TILE - 1 - d_lo, d_hi - n_dr
    fill = lambda n: jnp.full((D_HEADS, n, GRID_W, GRID_W), NEG_BIG, _F32)
    tp = jnp.concatenate([fill(before), toe, fill(after)], axis=1).reshape(4, 2, before + n_dr + after, GRID_W, GRID_W)
    quads = jnp.concatenate([tp[:, w, ROWS_PER_TILE - 1 - qr:ROWS_PER_TILE - 1 - qr + d_hi - d_lo]
                             for w in (0, 1) for qr in range(ROWS_PER_TILE)], axis=-1)

    tiles = []
    for (q0, k0), d0 in zip(kinds, d_first):
        rs = np.clip(q0 + np.arange(ROWS_PER_TILE) - NA_WIN_H // 2, 0, rows_n - NA_WIN_H)[None, :]
        krow = (k0 + np.arange(NA_KEY_ROWS))[:, None]
        inside = (krow >= rs) & (krow < rs + NA_WIN_H)
        inside = np.tile(np.repeat(np.repeat(inside, GRID_W, axis=0), GRID_W, axis=1), (1, 2))
        block = quads[:, d0 - d_lo:d0 - d_lo + NA_KEY_ROWS].reshape(4, NA_KEYS, Q2)
        tiles.append(jnp.where(inside, block, NEG_BIG))
    return jnp.stack(tiles, axis=1)


def _pad_cols(w, groups, width, total, offset=0):
    k = w.shape[0]
    w = w.reshape(k, groups, width)
    w = jnp.pad(w, ((0, 0), (0, 0), (offset, total - width - offset)))
    return w.reshape(k, groups * total)


def kernel(x, c, ctx, c_ctx, l0_w_mod, l0_b_mod, l0_g_attn, l0_w_in, l0_lam_q1, l0_lam_k1, l0_lam_q2, l0_lam_k2, l0_g_subln, l0_g_cq, l0_w_uq, l0_g_ckv, l0_w_ukv, l0_w_out, l0_g_ffn, l0_w1, l0_w3, l0_w2, l1_w_mod, l1_b_mod, l1_g_attn, l1_w_in, l1_g_qc, l1_g_kc, l1_rpb, l1_w_out, l1_g_ffn, l1_w1, l1_w3, l1_w2, g_final):
    bf = lambda w: w.astype(_BF)
    cond = jnp.concatenate([c, jnp.tile(c_ctx[None], (NB, 1)),
                            jnp.zeros((8 - BATCH - NB, D_MODEL), _F32)], axis=0)
    tabs = _rope_tables()
    tabs_t = [t.T for t in tabs]

    mods0 = _ada_params(cond, l0_w_mod, l0_b_mod)
    w_in0 = bf(jnp.concatenate([l0_w_in[:, :1920],
                                _pad_cols(l0_w_in[:, 1920:], 1, B_ROPE_DIM, LANES, B_NOPE_DIM)], axis=1))
    w_uq = bf(_pad_cols(l0_w_uq, B_HEADS, B_NOPE_DIM + B_ROPE_DIM, LANES))
    ukv = l0_w_ukv.reshape(B_KV_RANK, B_HEADS, B_NOPE_DIM + B_V_DIM)
    w_ukv = bf(jnp.concatenate([_pad_cols(ukv[:, :, :B_NOPE_DIM].reshape(B_KV_RANK, -1), B_HEADS, B_NOPE_DIM, LANES),
                                ukv[:, :, B_NOPE_DIM:].reshape(B_KV_RANK, -1)], axis=1))
    qdt, kd, vdt, qmt, km, vmt = _project0(x, ctx, mods0, l0_g_attn, w_in0, tabs, tabs_t,
                                           l0_g_cq, w_uq, l0_g_ckv, w_ukv)
    lamv = jnp.stack([l0_lam_q1, l0_lam_k1, l0_lam_q2, l0_lam_k2])
    o_a = _diff_attention(qdt, kd, vdt, lamv, l0_g_subln)
    o_b = _mla_attention(qmt, km, vmt)
    w_out0 = bf(l0_w_out)
    xs = _merge_ffn((x, ctx), o_a, o_b, mods0, w_out0[:512], w_out0[512:], l0_g_ffn,
                    bf(l0_w1), bf(l0_w3), bf(l0_w2), g_final, latent_only=False)

    mods1 = _ada_params(cond, l1_w_mod, l1_b_mod)
    wq = l1_w_in[:, :512].reshape(D_MODEL, 2, 4, C_HEAD_DIM).transpose(0, 2, 1, 3).reshape(D_MODEL, 512)
    w_in1 = bf(jnp.concatenate([wq, l1_w_in[:, 512:]], axis=1))
    grp = np.arange(LANES) // C_HEAD_DIM
    bd = jnp.asarray((grp[:, None] == grp[None, :]) / C_HEAD_DIM, _BF)
    qct, kc, vct, qnt, kn, vnt = _project1(xs, mods1, l1_g_attn, w_in1, tabs, tabs_t, bd, l1_g_qc, l1_g_kc)
    o_c = _gqa_attention(qct, kc, vct)
    o_d = _na_attention(qnt, kn, vnt, _na_bias_table(l1_rpb))
    w_out1 = bf(l1_w_out)
    woc = w_out1[:512].reshape(2, 4, C_HEAD_DIM, D_MODEL).transpose(1, 0, 2, 3).reshape(512, D_MODEL)
    return _merge_ffn((xs,), o_c, o_d, mods1, woc, w_out1[512:], l1_g_ffn,
                      bf(l1_w1), bf(l1_w3), bf(l1_w2), g_final, latent_only=True)
```

```python
import functools
import math

import numpy as np
import jax
import jax.numpy as jnp
from jax import lax
from jax.experimental import pallas as pl
from jax.experimental.pallas import tpu as pltpu

D_MODEL = 1024
BATCH = 4
SEQ = 4096
GRID_W = 64
CTX_LEN = 256
ROPE_THETA = 10000.0
EPS = 1e-6
A_HEADS = 4
A_HEAD_DIM = 64
B_HEADS = 8
B_Q_RANK = 256
B_KV_RANK = 128
B_NOPE_DIM = 64
B_ROPE_DIM = 32
B_V_DIM = 64
C_HEADS = 8
C_KV_HEADS = 2
C_HEAD_DIM = 64
D_HEADS = 8
D_HEAD_DIM = 64
NA_WIN_H = 8
NA_WIN_W = 16
FFN_HIDDEN = 2816
LAMBDA_INIT = 0.8 - 0.6 * math.exp(-0.3 * 0)

S_ALL = CTX_LEN + SEQ
LANES = 128
HALF = LANES // 2
ROW_TILE = 256
NB = 2
N_TILES = S_ALL // ROW_TILE
N_LAT_TILES = SEQ // ROW_TILE
Q2 = 2 * ROW_TILE
KEY_CHUNK = 256
N_KEY_CHUNKS = S_ALL // KEY_CHUNK
SUM_ROWS = 16
ROWS_PER_TILE = ROW_TILE // GRID_W
NA_KEY_ROWS = 12
NA_KEYS = NA_KEY_ROWS * GRID_W
LOG2E = math.log2(math.e)
NEG_BIG = -1e30
VMEM_LIMIT = 52 * 1024 * 1024

N_PAIRS = 4
MIX_W = N_PAIRS * LANES
KC_W = C_KV_HEADS * C_HEAD_DIM


def _cuts(*widths):
    return tuple(int(v) for v in np.cumsum((0,) + widths))


P0 = _cuts(MIX_W, MIX_W, MIX_W, B_Q_RANK, B_KV_RANK, LANES)
N0 = _cuts(MIX_W, B_Q_RANK, B_KV_RANK, LANES)
T0 = _cuts(MIX_W, MIX_W)
P1 = _cuts(MIX_W, KC_W, KC_W, MIX_W, MIX_W, MIX_W)
N1 = _cuts(KC_W, MIX_W)
T1 = _cuts(MIX_W, KC_W, MIX_W, MIX_W)

assert ROW_TILE == CTX_LEN and S_ALL % ROW_TILE == 0 and N_LAT_TILES % 2 == 0
assert MIX_W == A_HEADS * 2 * A_HEAD_DIM == B_HEADS * B_V_DIM == C_HEADS * C_HEAD_DIM == D_HEADS * D_HEAD_DIM
assert KC_W == LANES

_BF = jnp.bfloat16
_F32 = jnp.float32


def _dot(a, b):
    return jnp.dot(a, b, preferred_element_type=_F32)


def _rms(x):
    return x * lax.rsqrt(jnp.mean(x * x, axis=-1, keepdims=True) + EPS)


def _lane(shape):
    return lax.broadcasted_iota(jnp.int32, shape, len(shape) - 1)


def _swap_halves(x, half):
    n = x.shape[-1]
    up = pltpu.roll(x, n - half, axis=1)
    down = pltpu.roll(x, half, axis=1)
    return jnp.where((_lane(x.shape) % (2 * half)) < half, up, down)


def _rope(x, c, s, half):
    return x * c + _swap_halves(x, half) * s


def _rope_t(x, c, s, half):
    parts = []
    for r in range(0, x.shape[0], 2 * half):
        parts += [x[r + half:r + 2 * half], x[r:r + half]]
    return x * c + jnp.concatenate(parts, axis=0) * s


def _dot_nt(a, b):
    return lax.dot_general(a, b, (((1,), (1,)), ((), ())), preferred_element_type=_F32)


def _group_mean_sq(x, bd):
    x2 = x * x
    hi = x2.astype(_BF)
    lo = (x2 - hi.astype(_F32)).astype(_BF)
    return _dot(hi, bd) + _dot(lo, bd)


def _ada_body(cond_ref, w_ref, b_ref, o_ref):
    c = cond_ref[...]
    a = (c * (1.0 / (1.0 + jnp.exp(-c)))).astype(_BF)
    o_ref[...] = _dot(a, w_ref[...].astype(_BF)) + b_ref[...]


def _ada_params(cond, w_mod, b_mod):
    n = w_mod.shape[1]
    tn = n // 4
    out = pl.pallas_call(
        _ada_body,
        grid=(n // tn,),
        in_specs=[pl.BlockSpec((8, D_MODEL), lambda j: (0, 0)),
                  pl.BlockSpec((D_MODEL, tn), lambda j: (0, j)),
                  pl.BlockSpec((1, tn), lambda j: (0, j))],
        out_specs=pl.BlockSpec((8, tn), lambda j: (0, j)),
        out_shape=jax.ShapeDtypeStruct((8, n), _F32),
        compiler_params=pltpu.CompilerParams(vmem_limit_bytes=VMEM_LIMIT),
        name="ada_params",
    )(cond, w_mod, b_mod.reshape(1, n))
    return out.reshape(8, 6, D_MODEL)


def _mod_row(b, t):
    return jnp.where(t == 0, BATCH // NB, b)


def _const_spec(shape):
    nd = len(shape)
    return pl.BlockSpec(shape, lambda *_: (0,) * nd, pipeline_mode=pl.Buffered(1))


def _modulated(x, m, g, shift_row, scale_row):
    h = _rms(x) * g
    return h * (1.0 + m[scale_row:scale_row + 1]) + m[shift_row:shift_row + 1]


def _modulated_rows(xs, m, g, shift_row, scale_row):
    return jnp.concatenate([_modulated(xs[i], m[i], g, shift_row, scale_row) for i in range(NB)],
                           axis=0).astype(_BF)


def _stream_tile(x_ref, ctx_ref):
    return jnp.where(pl.program_id(1) == 0, ctx_ref[...], x_ref[...])


def _stream_specs():
    return [pl.BlockSpec((NB, ROW_TILE, D_MODEL), lambda b, t: (b, jnp.maximum(t - 1, 0), 0)),
            pl.BlockSpec((NB, CTX_LEN, D_MODEL), lambda b, t: (b, 0, 0))]


def _proj0_body(x_ref, ctx_ref, mod_ref, g_ref, wn_ref, wt_ref,
                ca_ref, sa_ref, cb_ref, sb_ref, cat_ref, sat_ref, cbt_ref, sbt_ref,
                g_cq_ref, w_uqt_ref, g_ckv_ref, w_uk_ref, w_uvt_ref,
                qdt_ref, kd_ref, vdt_ref, qmt_ref, km_ref, vmt_ref):
    h = _modulated_rows(_stream_tile(x_ref, ctx_ref), mod_ref[...], g_ref[...], 0, 1)
    yn = _dot(h, wn_ref[...])
    yt = _dot_nt(wt_ref[...], h)
    cq = (_rms(yn[:, N0[1]:N0[2]]) * g_cq_ref[...]).astype(_BF)
    ckv = (_rms(yn[:, N0[2]:N0[3]]) * g_ckv_ref[...]).astype(_BF)
    qmt = _dot_nt(w_uqt_ref[...], cq)
    kn = _dot(ckv, w_uk_ref[...])
    vmt = _dot_nt(w_uvt_ref[...], ckv)
    ca, sa, cb, sb = ca_ref[...], sa_ref[...], cb_ref[...], sb_ref[...]
    cat, sat, cbt, sbt = cat_ref[...], sat_ref[...], cbt_ref[...], sbt_ref[...]
    qscale = A_HEAD_DIM ** -0.5 * LOG2E
    mscale = (B_NOPE_DIM + B_ROPE_DIM) ** -0.5 * LOG2E
    for i in range(NB):
        rows = slice(i * ROW_TILE, (i + 1) * ROW_TILE)
        for hd in range(A_HEADS):
            lo = hd * LANES
            qdt_ref[i, hd] = (_rope_t(yt[lo:lo + LANES, rows], cat, sat, A_HEAD_DIM // 2) * qscale).astype(_BF)
            kd_ref[i, hd] = _rope(yn[rows, lo:lo + LANES], ca, sa, A_HEAD_DIM // 2).astype(_BF)
            vdt_ref[i, hd] = yt[T0[1] + lo:T0[1] + lo + LANES, rows].astype(_BF)
        kr = _rope(yn[rows, N0[3]:N0[4]], cb, sb, B_ROPE_DIM // 2)
        for hd in range(B_HEADS):
            lo = hd * LANES
            qmt_ref[i, hd] = (_rope_t(qmt[lo:lo + LANES, rows], cbt, sbt, B_ROPE_DIM // 2) * mscale).astype(_BF)
            km_ref[i, hd] = (kn[rows, lo:lo + LANES] + kr).astype(_BF)
        for pr in range(B_HEADS // 2):
            vmt_ref[i, pr] = vmt[pr * LANES:(pr + 1) * LANES, rows].astype(_BF)


def _head_out(n_heads, transposed):
    if transposed:
        return (jax.ShapeDtypeStruct((BATCH, n_heads, LANES, S_ALL), _BF),
                pl.BlockSpec((NB, n_heads, LANES, ROW_TILE), lambda b, t: (b, 0, 0, t)))
    return (jax.ShapeDtypeStruct((BATCH, n_heads, S_ALL, LANES), _BF),
            pl.BlockSpec((NB, n_heads, ROW_TILE, LANES), lambda b, t: (b, 0, t, 0)))


def _row_spec(width):
    return pl.BlockSpec((NB, ROW_TILE, width), lambda b, t: (b, t, 0))


def _mod_spec():
    return pl.BlockSpec((NB, 6, D_MODEL), lambda b, t: (_mod_row(b, t), 0, 0))


def _table_spec():
    return pl.BlockSpec((ROW_TILE, LANES), lambda b, t: (t, 0))


def _table_t_spec():
    return pl.BlockSpec((LANES, ROW_TILE), lambda b, t: (0, t))


def _project0(x, ctx, mods, g_attn, w_in, tabs, tabs_t, g_cq, w_uq, g_ckv, w_ukv):
    wn = jnp.concatenate([w_in[:, P0[1]:P0[2]], w_in[:, P0[3]:P0[6]]], axis=1)
    wt = jnp.concatenate([w_in[:, P0[0]:P0[1]], w_in[:, P0[2]:P0[3]]], axis=1).T
    w_uqt = w_uq.T
    w_uk, w_uvt = w_ukv[:, :B_HEADS * LANES], w_ukv[:, B_HEADS * LANES:].T
    outs = [_head_out(A_HEADS, True), _head_out(A_HEADS, False), _head_out(A_HEADS, True),
            _head_out(B_HEADS, True), _head_out(B_HEADS, False), _head_out(B_HEADS // 2, True)]
    return pl.pallas_call(
        _proj0_body,
        grid=(BATCH // NB, N_TILES),
        in_specs=_stream_specs() + [_mod_spec(), _const_spec((1, D_MODEL)),
                  _const_spec(wn.shape), _const_spec(wt.shape)]
                 + [_table_spec()] * 4 + [_table_t_spec()] * 4
                 + [_const_spec((1, B_Q_RANK)), _const_spec(w_uqt.shape),
                    _const_spec((1, B_KV_RANK)), _const_spec(w_uk.shape), _const_spec(w_uvt.shape)],
        out_specs=[o[1] for o in outs],
        out_shape=[o[0] for o in outs],
        compiler_params=pltpu.CompilerParams(vmem_limit_bytes=VMEM_LIMIT),
        name="project0",
    )(x, ctx, mods, g_attn.reshape(1, -1), wn, wt, *tabs, *tabs_t,
      g_cq.reshape(1, -1), w_uqt, g_ckv.reshape(1, -1), w_uk, w_uvt)


def _proj1_body(x_ref, mod_ref, g_ref, wn_ref, wt_ref, ca_ref, sa_ref, cat_ref, sat_ref, bd_ref,
                g_qc_ref, g_kc_ref,
                qct_ref, kc_ref, vct_ref, qnt_ref, kn_ref, vnt_ref):
    h = _modulated_rows(x_ref[...], mod_ref[...], g_ref[...], 0, 1)
    yn = _dot(h, wn_ref[...])
    yt = _dot_nt(wt_ref[...], h)
    ca, sa, cat, sat = ca_ref[...], sa_ref[...], cat_ref[...], sat_ref[...]
    qscale = C_HEAD_DIM ** -0.5 * LOG2E
    nscale = D_HEAD_DIM ** -0.5 * LOG2E

    kc_all = yn[:, 0:LANES]
    kc_all = kc_all * lax.rsqrt(_group_mean_sq(kc_all, bd_ref[...]) + EPS) * g_kc_ref[...]
    q3 = yt[T1[0]:T1[1]].reshape(C_HEADS, C_HEAD_DIM, NB * ROW_TILE)
    q3 = q3 * lax.rsqrt(jnp.mean(q3 * q3, axis=1, keepdims=True) + EPS)
    qc_all = q3.reshape(MIX_W, NB * ROW_TILE) * g_qc_ref[...]
    for i in range(NB):
        rows = slice(i * ROW_TILE, (i + 1) * ROW_TILE)
        kc_ref[i, 0] = _rope(kc_all[rows], ca, sa, C_HEAD_DIM // 2).astype(_BF)
        vct_ref[i, 0] = yt[T1[1]:T1[2], rows].astype(_BF)
        for pr in range(N_PAIRS):
            lo = pr * LANES
            qct_ref[i, pr] = (_rope_t(qc_all[lo:lo + LANES, rows], cat, sat, C_HEAD_DIM // 2) * qscale).astype(_BF)
            qnt_ref[i, pr] = (yt[T1[2] + lo:T1[2] + lo + LANES, rows] * nscale).astype(_BF)
            kn_ref[i, pr] = yn[rows, N1[1] + lo:N1[1] + lo + LANES].astype(_BF)
            vnt_ref[i, pr] = yt[T1[3] + lo:T1[3] + lo + LANES, rows].astype(_BF)


def _project1(xs, mods, g_attn, w_in, tabs, tabs_t, bd, g_qc, g_kc):
    wn = jnp.concatenate([w_in[:, P1[1]:P1[2]], w_in[:, P1[4]:P1[5]]], axis=1)
    wt = jnp.concatenate([w_in[:, P1[0]:P1[1]], w_in[:, P1[2]:P1[4]], w_in[:, P1[5]:P1[6]]], axis=1).T
    outs = [_head_out(N_PAIRS, True), _head_out(1, False), _head_out(1, True),
            _head_out(N_PAIRS, True), _head_out(N_PAIRS, False), _head_out(N_PAIRS, True)]
    return pl.pallas_call(
        _proj1_body,
        grid=(BATCH // NB, N_TILES),
        in_specs=[_row_spec(D_MODEL), _mod_spec(), _const_spec((1, D_MODEL)),
                  _const_spec(wn.shape), _const_spec(wt.shape),
                  _table_spec(), _table_spec(), _table_t_spec(), _table_t_spec(),
                  _const_spec(bd.shape),
                  _const_spec((MIX_W, 1)), _const_spec((1, LANES))],
        out_specs=[o[1] for o in outs],
        out_shape=[o[0] for o in outs],
        compiler_params=pltpu.CompilerParams(vmem_limit_bytes=VMEM_LIMIT),
        name="project1",
    )(xs, mods, g_attn.reshape(1, -1), wn, wt, tabs[0], tabs[1], tabs_t[0], tabs_t[1], bd,
      jnp.tile(g_qc, 8).reshape(-1, 1), jnp.tile(g_kc, 2).reshape(1, -1))


def _split_q_cols(qt):
    z = jnp.zeros((HALF, qt.shape[1]), qt.dtype)
    return jnp.concatenate([jnp.concatenate([qt[:HALF], z], axis=0),
                            jnp.concatenate([z, qt[HALF:]], axis=0)], axis=1)


def _with_sum_rows(vt):
    n = vt.shape[1]
    return jnp.concatenate([vt, jnp.ones((1, n), vt.dtype), jnp.zeros((SUM_ROWS - 1, n), vt.dtype)], axis=0)


def _pv_t(vt, e, split):
    if not split:
        return _dot(_with_sum_rows(vt), e)
    return jnp.concatenate([_dot(_with_sum_rows(vt[:HALF]), e[:, :ROW_TILE]),
                            _dot(_with_sum_rows(vt[HALF:]), e[:, ROW_TILE:])], axis=0)


def _normalise_t(acc, split):
    if not split:
        return acc[:LANES] / acc[LANES:LANES + 1]
    hb = HALF + SUM_ROWS
    return jnp.concatenate([acc[:HALF] / acc[HALF:HALF + 1],
                            acc[hb:hb + HALF] / acc[hb + HALF:hb + HALF + 1]], axis=0)


def _pipelined_tiles(load_q, score_chunk, value_chunk, finalize, n_chunks, bufs, split):
    acc_shape = (LANES + 2 * SUM_ROWS, ROW_TILE) if split else (LANES + SUM_ROWS, Q2)

    def step(t_next, nxt, t_cur, cur):
        mx = acc = None
        if t_next is not None:
            qb = load_q(t_next)
        if t_cur is not None:
            m_cur = cur[1][0:1, :]
            acc = jnp.zeros(acc_shape, _F32)
        for c in range(n_chunks):
            rows = slice(c * KEY_CHUNK, (c + 1) * KEY_CHUNK)
            if t_next is not None:
                s = score_chunk(t_next, qb, c)
                nxt[0][rows, :] = s
                cm = jnp.max(s, axis=0, keepdims=True)
                mx = cm if mx is None else jnp.maximum(mx, cm)
            if t_cur is not None:
                e = jnp.exp2(cur[0][rows, :] - m_cur)
                acc = acc + value_chunk(t_cur, c, e.astype(_BF))
        if t_next is not None:
            nxt[1][...] = jnp.broadcast_to(mx, nxt[1].shape)
        return acc

    last = N_LAT_TILES - 1
    step(0, bufs[0], None, None)

    def pair(j, carry):
        finalize(jnp.maximum(2 * j - 1, 0), carry)
        finalize(2 * j, step(2 * j + 1, bufs[1], 2 * j, bufs[0]))
        return step(2 * j + 2, bufs[0], 2 * j + 1, bufs[1])

    carry = lax.fori_loop(0, N_LAT_TILES // 2 - 1, pair, jnp.ones(acc_shape, _F32))
    finalize(last - 2, carry)
    finalize(last - 1, step(last, bufs[1], last - 1, bufs[0]))
    finalize(last, step(None, None, last, bufs[1]))


def _dense_body(*refs, n_q, diff, ctx_tile, o_off):
    q_refs, k_refs, vt_ref = refs[:n_q], refs[n_q:2 * n_q], refs[2 * n_q]
    rest = refs[2 * n_q + 1:]
    if diff:
        lam_ref, g_ref, o_ref, s0, s1, m0, m1 = rest
        lv = lam_ref[...]
        lam = (jnp.exp(jnp.sum(lv[0:1] * lv[1:2], axis=-1, keepdims=True))
               - jnp.exp(jnp.sum(lv[2:3] * lv[3:4], axis=-1, keepdims=True)) + LAMBDA_INIT)
    else:
        o_ref, s0, s1, m0, m1 = rest

    def q_blocks(col0):
        if n_q == 1:
            return [_split_q_cols(q_refs[0][0, 0, :, pl.ds(col0, ROW_TILE)])]
        return [r[0, 0, :, pl.ds(col0, ROW_TILE)] for r in q_refs]

    def scores(qb, row0, rows):
        parts = [_dot(k_refs[i][0, 0, pl.ds(row0, rows), :], qb[i]) for i in range(n_q)]
        return parts[0] if n_q == 1 else jnp.concatenate(parts, axis=1)

    def finalize_rows(row0, acc):
        ot = _normalise_t(acc, not diff)
        if diff:
            o = (ot[:, :ROW_TILE] - lam * ot[:, ROW_TILE:]).T
            o = _rms(o) * g_ref[...] * (1.0 - LAMBDA_INIT)
        else:
            o = ot.T
        o_ref[0, pl.ds(row0, ROW_TILE), :] = o.astype(_BF)

    def load_q(t):
        return q_blocks(pl.multiple_of(CTX_LEN + t * ROW_TILE, ROW_TILE))

    def score_chunk(t, qb, c):
        return scores(qb, c * KEY_CHUNK, KEY_CHUNK)

    def value_chunk(t, c, e):
        return _pv_t(vt_ref[0, 0, :, c * KEY_CHUNK:(c + 1) * KEY_CHUNK], e, not diff)

    def finalize(t, acc):
        finalize_rows(pl.multiple_of(o_off + t * ROW_TILE, ROW_TILE), acc)

    _pipelined_tiles(load_q, score_chunk, value_chunk, finalize, N_KEY_CHUNKS, ((s0, m0), (s1, m1)), not diff)

    if ctx_tile:
        s = scores(q_blocks(0), 0, CTX_LEN)
        e = jnp.exp2(s - jnp.max(s, axis=0, keepdims=True))
        finalize_rows(0, _pv_t(vt_ref[0, 0, :, 0:CTX_LEN], e.astype(_BF), not diff))


def _na_body(qt_ref, k_ref, vt_ref, bias_ref, o_ref, s0, s1, m0, m1):
    def key_row0(t, c):
        first = jnp.clip(t - 1, 0, N_LAT_TILES - 3)
        return 0 if c == 0 else pl.multiple_of(CTX_LEN + (first + c - 1) * KEY_CHUNK, KEY_CHUNK)

    def load_q(t):
        return _split_q_cols(qt_ref[0, 0, :, pl.ds(pl.multiple_of(CTX_LEN + t * ROW_TILE, ROW_TILE), ROW_TILE)])

    def score_chunk(t, qb, c):
        s = _dot(k_ref[0, 0, pl.ds(key_row0(t, c), KEY_CHUNK), :], qb)
        if c > 0:
            kind = jnp.where(t == 0, 0, jnp.where(t == N_LAT_TILES - 1, 2, 1))
            s = s + bias_ref[0, kind, (c - 1) * KEY_CHUNK:c * KEY_CHUNK, :]
        return s

    def value_chunk(t, c, e):
        return _pv_t(vt_ref[0, 0, :, pl.ds(key_row0(t, c), KEY_CHUNK)], e, True)

    def finalize(t, acc):
        o = _normalise_t(acc, True).T
        o_ref[0, pl.ds(pl.multiple_of(t * ROW_TILE, ROW_TILE), ROW_TILE), :] = o.astype(_BF)

    _pipelined_tiles(load_q, score_chunk, value_chunk, finalize, 1 + NA_KEYS // KEY_CHUNK,
                     ((s0, m0), (s1, m1)), True)


def _attn_call(body, name, in_specs, arrays, out_rows, n_keys):
    return pl.pallas_call(
        body,
        grid=(BATCH, N_PAIRS),
        in_specs=in_specs,
        out_specs=pl.BlockSpec((1, out_rows, LANES), lambda b, g: (b, 0, g)),
        out_shape=jax.ShapeDtypeStruct((BATCH, out_rows, MIX_W), _BF),
        scratch_shapes=[pltpu.VMEM((n_keys, Q2), _F32), pltpu.VMEM((n_keys, Q2), _F32),
                        pltpu.VMEM((8, Q2), _F32), pltpu.VMEM((8, Q2), _F32)],
        compiler_params=pltpu.CompilerParams(vmem_limit_bytes=VMEM_LIMIT),
        name=name,
    )(*arrays)


def _t_spec(f):
    return pl.BlockSpec((1, 1, LANES, S_ALL), f)


def _n_spec(f):
    return pl.BlockSpec((1, 1, S_ALL, LANES), f)


def _diff_attention(qdt, kd, vdt, lamv, g_subln):
    head = lambda b, g: (b, g, 0, 0)
    body = functools.partial(_dense_body, n_q=1, diff=True, ctx_tile=True, o_off=CTX_LEN)
    return _attn_call(body, "diff_attention",
                      [_t_spec(head), _n_spec(head), _t_spec(head),
                       pl.BlockSpec((4, A_HEAD_DIM), lambda b, g: (0, 0)),
                       pl.BlockSpec((1, LANES), lambda b, g: (0, 0))],
                      (qdt, kd, vdt, lamv, g_subln.reshape(1, -1)), S_ALL, S_ALL)


def _mla_attention(qmt, km, vmt):
    ha = lambda b, g: (b, 2 * g, 0, 0)
    hb = lambda b, g: (b, 2 * g + 1, 0, 0)
    body = functools.partial(_dense_body, n_q=2, diff=False, ctx_tile=True, o_off=CTX_LEN)
    return _attn_call(body, "mla_attention",
                      [_t_spec(ha), _t_spec(hb), _n_spec(ha), _n_spec(hb),
                       _t_spec(lambda b, g: (b, g, 0, 0))],
                      (qmt, qmt, km, km, vmt), S_ALL, S_ALL)


def _gqa_attention(qct, kc, vct):
    shared = lambda b, g: (b, 0, 0, 0)
    body = functools.partial(_dense_body, n_q=1, diff=False, ctx_tile=False, o_off=0)
    return _attn_call(body, "gqa_attention",
                      [_t_spec(lambda b, g: (b, g, 0, 0)), _n_spec(shared), _t_spec(shared)],
                      (qct, kc, vct), SEQ, S_ALL)


def _na_attention(qnt, kn, vnt, bias_t):
    head = lambda b, g: (b, g, 0, 0)
    return _attn_call(_na_body, "na_attention",
                      [_t_spec(head), _n_spec(head), _t_spec(head),
                       pl.BlockSpec((1, 3, NA_KEYS, Q2), lambda b, g: (g, 0, 0, 0))],
                      (qnt, kn, vnt, bias_t), SEQ, CTX_LEN + NA_KEYS)


def _merge_ffn_body(*refs, final):
    x = refs[0][...] if final else _stream_tile(refs[0], refs[1])
    (oa_ref, ob_ref, mod_ref, woa_ref, wob_ref, g_ref,
     w1_ref, w3_ref, w2_ref, gf_ref, o_ref) = refs[1 if final else 2:]
    m = mod_ref[...]
    stack = lambda r: r[...].reshape(NB * ROW_TILE, r.shape[-1])
    rows = lambda v, i: v[i * ROW_TILE:(i + 1) * ROW_TILE]
    attn = _dot(stack(oa_ref), woa_ref[...]) + _dot(stack(ob_ref), wob_ref[...])
    x1 = [x[i] + m[i, 2:3] * rows(attn, i) for i in range(NB)]
    h = _modulated_rows(x1, m, g_ref[...], 3, 4)
    a = _dot(h, w1_ref[...])
    b = _dot(h, w3_ref[...])
    u = (a * (1.0 / (1.0 + jnp.exp(-a))) * b).astype(_BF)
    f = _dot(u, w2_ref[...])
    for i in range(NB):
        x2 = x1[i] + m[i, 5:6] * rows(f, i)
        if final:
            x2 = _rms(x2) * gf_ref[...]
        o_ref[i] = x2


def _merge_ffn(streams, oa, ob, mods, woa, wob, g_ffn, w1, w3, w2, g_final, *, latent_only):
    off = 1 if latent_only else 0
    tiles = N_LAT_TILES if latent_only else N_TILES
    row = lambda w, o: pl.BlockSpec((NB, ROW_TILE, w), lambda b, t: (b, t + o, 0))
    mod = pl.BlockSpec((NB, 6, D_MODEL), lambda b, t: (_mod_row(b, t + off), 0, 0))
    x_specs = [row(D_MODEL, off)] if latent_only else _stream_specs()
    return pl.pallas_call(
        functools.partial(_merge_ffn_body, final=latent_only),
        grid=(BATCH // NB, tiles),
        in_specs=x_specs + [row(MIX_W, 0), row(MIX_W, 0), mod,
                  _const_spec(woa.shape), _const_spec(wob.shape), _const_spec((1, D_MODEL)),
                  _const_spec(w1.shape), _const_spec(w3.shape), _const_spec(w2.shape),
                  _const_spec((1, D_MODEL))],
        out_specs=row(D_MODEL, 0),
        out_shape=jax.ShapeDtypeStruct((BATCH, tiles * ROW_TILE, D_MODEL), _F32),
        compiler_params=pltpu.CompilerParams(vmem_limit_bytes=VMEM_LIMIT),
        name="merge_ffn_final" if latent_only else "merge_ffn",
    )(*streams, oa, ob, mods, woa, wob, g_ffn.reshape(1, -1), w1, w3, w2, g_final.reshape(1, -1))


def _rope_tables():
    t = np.arange(SEQ)
    rows = jnp.asarray(t // GRID_W, _F32)
    cols = jnp.asarray(t % GRID_W, _F32)

    def cs(rot_dim):
        axis_dim = rot_dim // 2
        inv = ROPE_THETA ** (-jnp.arange(0, axis_dim, 2, dtype=_F32) / axis_dim)
        ang = jnp.concatenate([rows[:, None] * inv, cols[:, None] * inv], axis=-1)
        return jnp.cos(ang), jnp.sin(ang)

    def with_ctx(tab, fill):
        return jnp.concatenate([jnp.full((CTX_LEN, LANES), fill, _F32), tab], axis=0)

    cos, sin = cs(A_HEAD_DIM)
    ca = with_ctx(jnp.tile(cos, (1, LANES // (A_HEAD_DIM // 2))), 1.0)
    sa = with_ctx(jnp.tile(jnp.concatenate([-sin, sin], axis=-1), (1, 2)), 0.0)
    cos, sin = cs(B_ROPE_DIM)
    ones = jnp.ones((SEQ, 64), _F32)
    zeros = jnp.zeros((SEQ, 64), _F32)
    cb = with_ctx(jnp.concatenate([ones, cos, cos, ones[:, :32]], axis=-1), 1.0)
    sb = with_ctx(jnp.concatenate([zeros, -sin, sin, zeros[:, :32]], axis=-1), 0.0)
    return ca, sa, cb, sb


def _na_bias_table(rpb):
    rows_n = SEQ // GRID_W
    span = 2 * GRID_W - 1
    left = GRID_W - NA_WIN_W
    p = jnp.pad(rpb * LOG2E, ((0, 0), (0, 0), (left, span - left - (2 * NA_WIN_W - 1))))
    flat = jnp.tile(p[:, :, ::-1], (1, 1, GRID_W + 1))[..., :GRID_W * (span + 1)]
    toe = flat.reshape(D_HEADS, 2 * NA_WIN_H - 1, GRID_W, span + 1)[:, :, ::-1, :GRID_W]
    kc = np.arange(GRID_W)[:, None]
    qc = np.arange(GRID_W)[None, :]
    cs = np.clip(qc - NA_WIN_W // 2, 0, GRID_W - NA_WIN_W)
    toe = jnp.where((kc >= cs) & (kc < cs + NA_WIN_W), toe, NEG_BIG)
    kinds = [(0, 0), (ROWS_PER_TILE, 0), (rows_n - ROWS_PER_TILE, rows_n - NA_KEY_ROWS)]
    d_first = [k0 - q0 + NA_WIN_H - 1 for q0, k0 in kinds]
    d_lo, d_hi = min(d_first), max(d_first) + NA_KEY_ROWS
    n_dr = 2 * NA_WIN_H - 1
    before, after = ROWS_PER_TILE - 1 - d_lo, d_hi - n_dr
    fill = lambda n: jnp.full((D_HEADS, n, GRID_W, GRID_W), NEG_BIG, _F32)
    tp = jnp.concatenate([fill(before), toe, fill(after)], axis=1).reshape(
        N_PAIRS, 2, before + n_dr + after, GRID_W, GRID_W)
    quads = jnp.concatenate([tp[:, w, ROWS_PER_TILE - 1 - qr:ROWS_PER_TILE - 1 - qr + d_hi - d_lo]
                             for w in (0, 1) for qr in range(ROWS_PER_TILE)], axis=-1)

    tiles = []
    for (q0, k0), d0 in zip(kinds, d_first):
        rs = np.clip(q0 + np.arange(ROWS_PER_TILE) - NA_WIN_H // 2, 0, rows_n - NA_WIN_H)[None, :]
        krow = (k0 + np.arange(NA_KEY_ROWS))[:, None]
        inside = (krow >= rs) & (krow < rs + NA_WIN_H)
        inside = np.tile(np.repeat(np.repeat(inside, GRID_W, axis=0), GRID_W, axis=1), (1, 2))
        block = quads[:, d0 - d_lo:d0 - d_lo + NA_KEY_ROWS].reshape(N_PAIRS, NA_KEYS, Q2)
        tiles.append(jnp.where(inside, block, NEG_BIG))
    return jnp.stack(tiles, axis=1)


def _pad_cols(w, groups, width, total, offset=0):
    k = w.shape[0]
    w = w.reshape(k, groups, width)
    w = jnp.pad(w, ((0, 0), (0, 0), (offset, total - width - offset)))
    return w.reshape(k, groups * total)


def kernel(x, c, ctx, c_ctx, l0_w_mod, l0_b_mod, l0_g_attn, l0_w_in, l0_lam_q1, l0_lam_k1, l0_lam_q2, l0_lam_k2, l0_g_subln, l0_g_cq, l0_w_uq, l0_g_ckv, l0_w_ukv, l0_w_out, l0_g_ffn, l0_w1, l0_w3, l0_w2, l1_w_mod, l1_b_mod, l1_g_attn, l1_w_in, l1_g_qc, l1_g_kc, l1_rpb, l1_w_out, l1_g_ffn, l1_w1, l1_w3, l1_w2, g_final):
    bf = lambda w: w.astype(_BF)
    cond = jnp.concatenate([c, jnp.tile(c_ctx[None], (NB, 1)),
                            jnp.zeros((8 - BATCH - NB, D_MODEL), _F32)], axis=0)
    tabs = _rope_tables()
    tabs_t = [t.T for t in tabs]

    mods0 = _ada_params(cond, l0_w_mod, l0_b_mod)
    w_in0 = bf(jnp.concatenate([l0_w_in[:, :P0[5]],
                                _pad_cols(l0_w_in[:, P0[5]:], 1, B_ROPE_DIM, LANES, B_NOPE_DIM)], axis=1))
    w_uq = bf(_pad_cols(l0_w_uq, B_HEADS, B_NOPE_DIM + B_ROPE_DIM, LANES))
    ukv = l0_w_ukv.reshape(B_KV_RANK, B_HEADS, B_NOPE_DIM + B_V_DIM)
    w_ukv = bf(jnp.concatenate([_pad_cols(ukv[:, :, :B_NOPE_DIM].reshape(B_KV_RANK, -1), B_HEADS, B_NOPE_DIM, LANES),
                                ukv[:, :, B_NOPE_DIM:].reshape(B_KV_RANK, -1)], axis=1))
    qdt, kd, vdt, qmt, km, vmt = _project0(x, ctx, mods0, l0_g_attn, w_in0, tabs, tabs_t,
                                           l0_g_cq, w_uq, l0_g_ckv, w_ukv)
    lamv = jnp.stack([l0_lam_q1, l0_lam_k1, l0_lam_q2, l0_lam_k2])
    o_a = _diff_attention(qdt, kd, vdt, lamv, l0_g_subln)
    o_b = _mla_attention(qmt, km, vmt)
    w_out0 = bf(l0_w_out)
    xs = _merge_ffn((x, ctx), o_a, o_b, mods0, w_out0[:MIX_W], w_out0[MIX_W:], l0_g_ffn,
                    bf(l0_w1), bf(l0_w3), bf(l0_w2), g_final, latent_only=False)

    mods1 = _ada_params(cond, l1_w_mod, l1_b_mod)
    per_group = C_HEADS // C_KV_HEADS
    wq = l1_w_in[:, :MIX_W].reshape(D_MODEL, C_KV_HEADS, per_group, C_HEAD_DIM)
    w_in1 = bf(jnp.concatenate([wq.transpose(0, 2, 1, 3).reshape(D_MODEL, MIX_W), l1_w_in[:, MIX_W:]], axis=1))
    grp = np.arange(LANES) // C_HEAD_DIM
    bd = jnp.asarray((grp[:, None] == grp[None, :]) / C_HEAD_DIM, _BF)
    qct, kc, vct, qnt, kn, vnt = _project1(xs, mods1, l1_g_attn, w_in1, tabs, tabs_t, bd, l1_g_qc, l1_g_kc)
    o_c = _gqa_attention(qct, kc, vct)
    o_d = _na_attention(qnt, kn, vnt, _na_bias_table(l1_rpb))
    w_out1 = bf(l1_w_out)
    woc = w_out1[:MIX_W].reshape(C_KV_HEADS, per_group, C_HEAD_DIM, D_MODEL).transpose(1, 0, 2, 3)
    return _merge_ffn((xs,), o_c, o_d, mods1, woc.reshape(MIX_W, D_MODEL), w_out1[MIX_W:], l1_g_ffn,
                      bf(l1_w1), bf(l1_w3), bf(l1_w2), g_final, latent_only=True)
```
